```python
import jax, jax.numpy as jnp
from jax import lax
import numpy as np

D_MODEL = 1024
BATCH = 8
SEQ = 4096
DEPTH = 4

N_A = DEPTH // 2
N_B = DEPTH - N_A
POOL_WINDOWS = (2, 4, 8, 16)
POOL_GROUPS = len(POOL_WINDOWS)
GC = D_MODEL // POOL_GROUPS
N_HEADS = 16
HEAD_DIM = D_MODEL // N_HEADS
D_FF = 4 * D_MODEL
Q_BLOCK = 128
EPS = 1e-6

kernel_name = "yoco_pool_stickbreak_trunk"


def rms_norm(x, g):
    xf = x.astype(jnp.float32)
    y = xf * lax.rsqrt(jnp.mean(xf * xf, axis=-1, keepdims=True) + EPS)
    return (y * g.astype(jnp.float32)).astype(x.dtype)


def pool_mixer(h, w_grp, scale):
    B, S, D = h.shape
    hf = h.astype(jnp.float32)
    cp = jnp.concatenate([jnp.zeros((B, 1, D), jnp.float32), lax.cumsum(hf, axis=1)], axis=1)
    pos = jnp.arange(S, dtype=jnp.int32)
    outs = []
    for g, w in enumerate(POOL_WINDOWS):
        sl = slice(g * GC, (g + 1) * GC)
        upper = cp[:, 1:, sl]
        lower = jnp.concatenate([jnp.zeros((B, w - 1, GC), jnp.float32), cp[:, :S + 1 - w, sl]], axis=1)
        count = jnp.minimum(pos + 1, w).astype(jnp.float32)[None, :, None]
        outs.append((upper - lower) / count - hf[:, :, sl])
    y = jnp.stack(outs, axis=2)
    y = jnp.einsum('bsgc,gcd->bsgd', y, w_grp.astype(jnp.float32)).reshape(B, S, D)
    return (y * scale.astype(jnp.float32)).astype(h.dtype)


def stick_breaking_attention(q, k, v):
    S = q.shape[2]
    inv_sqrt_d = 1.0 / float(np.sqrt(HEAD_DIM))
    outs = []
    for i in range(S // Q_BLOCK):
        kv_len = (i + 1) * Q_BLOCK
        q_blk = q[:, :, i * Q_BLOCK:kv_len]
        k_blk = k[:, :, :kv_len]
        v_blk = v[:, :, :kv_len]
        z = jnp.einsum('bhqd,bhkd->bhqk', q_blk, k_blk,
                       preferred_element_type=jnp.float32) * inv_sqrt_d
        t_idx = i * Q_BLOCK + jnp.arange(Q_BLOCK, dtype=jnp.int32)[:, None]
        s_idx = jnp.arange(kv_len, dtype=jnp.int32)[None, :]
        mask = s_idx < t_idx
        log1m = jnp.where(mask, -jax.nn.softplus(z), 0.0)
        excl = lax.cumsum(log1m, axis=3, reverse=True) - log1m
        a = jnp.where(mask, jnp.exp(jax.nn.log_sigmoid(z) + excl), 0.0)
        outs.append(jnp.einsum('bhqk,bhkd->bhqd', a, v_blk.astype(jnp.float32)))
    return jnp.concatenate(outs, axis=2).astype(q.dtype)


def sq_relu_mlp(h, w_up, w_down):
    u = jnp.matmul(h, w_up)
    return jnp.matmul(jnp.square(jax.nn.relu(u)), w_down)


def _fwd_setup_inputs(seed: int = 0) -> dict:
    key = jax.random.key(seed)
    ks = jax.random.split(key, 16)
    D, HD = D_MODEL, N_HEADS * HEAD_DIM
    nrm = lambda k, shape, fan_in: jax.random.normal(k, shape, jnp.float32) * (fan_in ** -0.5)
    gain = lambda k, shape: 1.0 + 0.05 * jax.random.normal(k, shape, jnp.float32)
    return {
        "x": jax.random.normal(ks[0], (BATCH, SEQ, D), jnp.float32),
        "pool_w": nrm(ks[1], (N_A, POOL_GROUPS, GC, GC), GC),
        "pool_scale": 0.5 + jax.random.uniform(ks[2], (N_A, D), jnp.float32),
        "w_q": nrm(ks[3], (N_B, D, HD), D),
        "w_kv": nrm(ks[4], (D, 2 * HD), D),
        "kv_norm_g": gain(ks[5], (D,)),
        "w_o": nrm(ks[6], (N_B, HD, D), HD),
        "w_up": nrm(ks[7], (DEPTH, D, D_FF), D),
        "w_down": nrm(ks[8], (DEPTH, D_FF, D), D_FF),
        "mix_pre_g": gain(ks[9], (DEPTH, D)),
        "mix_post_g": gain(ks[10], (DEPTH, D)),
        "mlp_pre_g": gain(ks[11], (DEPTH, D)),
        "mlp_post_g": gain(ks[12], (DEPTH, D)),
    }


def _fwd_reference(x, pool_w, pool_scale, w_q, w_kv, kv_norm_g, w_o, w_up, w_down,
              mix_pre_g, mix_post_g, mlp_pre_g, mlp_post_g):
    B, S, D = x.shape
    HD = N_HEADS * HEAD_DIM
    k = v = None
    for layer in range(DEPTH):
        h = rms_norm(x, mix_pre_g[layer])
        if layer < N_A:
            m = pool_mixer(h, pool_w[layer], pool_scale[layer])
        else:
            j = layer - N_A
            q = jnp.matmul(h, w_q[j]).reshape(B, S, N_HEADS, HEAD_DIM).transpose(0, 2, 1, 3)
            o = stick_breaking_attention(q, k, v)
            m = jnp.matmul(o.transpose(0, 2, 1, 3).reshape(B, S, HD), w_o[j])
        x = x + rms_norm(m, mix_post_g[layer])
        h = rms_norm(x, mlp_pre_g[layer])
        x = x + rms_norm(sq_relu_mlp(h, w_up[layer], w_down[layer]), mlp_post_g[layer])
        if layer == N_A - 1:
            kv = jnp.matmul(rms_norm(x, kv_norm_g), w_kv).reshape(B, S, 2, N_HEADS, HEAD_DIM)
            k = kv[:, :, 0].transpose(0, 2, 1, 3)
            v = kv[:, :, 1].transpose(0, 2, 1, 3)
    return x


import jax as _jax
import jax.numpy as _jnp

TWIN_FORMAT = 'train_step'
FWD_PARAMS = ['x', 'pool_w', 'pool_scale', 'w_q', 'w_kv', 'kv_norm_g', 'w_o', 'w_up', 'w_down', 'mix_pre_g', 'mix_post_g', 'mlp_pre_g', 'mlp_post_g']
TWIN_WEIGHTS = ['pool_w', 'pool_scale', 'w_q', 'w_kv', 'kv_norm_g', 'w_o', 'w_up', 'w_down', 'mix_pre_g', 'mix_post_g', 'mlp_pre_g', 'mlp_post_g']
TWIN_DIFF_INPUT = 'x'
TWIN_INPUTS = ['x', 'pool_w', 'pool_scale', 'w_q', 'w_kv', 'kv_norm_g', 'w_o', 'w_up', 'w_down', 'mix_pre_g', 'mix_post_g', 'mlp_pre_g', 'mlp_post_g', 'loss_target', 'm_pool_w', 'm_pool_scale', 'm_w_q', 'm_w_kv', 'm_kv_norm_g', 'm_w_o', 'm_w_up', 'm_w_down', 'm_mix_pre_g', 'm_mix_post_g', 'm_mlp_pre_g', 'm_mlp_post_g', 'v_pool_w', 'v_pool_scale', 'v_w_q', 'v_w_kv', 'v_kv_norm_g', 'v_w_o', 'v_w_up', 'v_w_down', 'v_mix_pre_g', 'v_mix_post_g', 'v_mlp_pre_g', 'v_mlp_post_g']
TWIN_OUTPUTS = ['loss', 'grad_x', 'grad_pool_w', 'grad_pool_scale', 'grad_w_q', 'grad_w_kv', 'grad_kv_norm_g', 'grad_w_o', 'grad_w_up', 'grad_w_down', 'grad_mix_pre_g', 'grad_mix_post_g', 'grad_mlp_pre_g', 'grad_mlp_post_g', 'delta_pool_w', 'delta_pool_scale', 'delta_w_q', 'delta_w_kv', 'delta_kv_norm_g', 'delta_w_o', 'delta_w_up', 'delta_w_down', 'delta_mix_pre_g', 'delta_mix_post_g', 'delta_mlp_pre_g', 'delta_mlp_post_g', 'new_m_pool_w', 'new_m_pool_scale', 'new_m_w_q', 'new_m_w_kv', 'new_m_kv_norm_g', 'new_m_w_o', 'new_m_w_up', 'new_m_w_down', 'new_m_mix_pre_g', 'new_m_mix_post_g', 'new_m_mlp_pre_g', 'new_m_mlp_post_g', 'new_v_pool_w', 'new_v_pool_scale', 'new_v_w_q', 'new_v_w_kv', 'new_v_kv_norm_g', 'new_v_w_o', 'new_v_w_up', 'new_v_w_down', 'new_v_mix_pre_g', 'new_v_mix_post_g', 'new_v_mlp_pre_g', 'new_v_mlp_post_g']
TWIN_LEAF_KINDS = {'loss': 'loss', 'grad_x': 'grad_x', 'grad_pool_w': 'grad_w', 'grad_pool_scale': 'grad_w', 'grad_w_q': 'grad_w', 'grad_w_kv': 'grad_w', 'grad_kv_norm_g': 'grad_w', 'grad_w_o': 'grad_w', 'grad_w_up': 'grad_w', 'grad_w_down': 'grad_w', 'grad_mix_pre_g': 'grad_w', 'grad_mix_post_g': 'grad_w', 'grad_mlp_pre_g': 'grad_w', 'grad_mlp_post_g': 'grad_w', 'delta_pool_w': 'delta_w', 'delta_pool_scale': 'delta_w', 'delta_w_q': 'delta_w', 'delta_w_kv': 'delta_w', 'delta_kv_norm_g': 'delta_w', 'delta_w_o': 'delta_w', 'delta_w_up': 'delta_w', 'delta_w_down': 'delta_w', 'delta_mix_pre_g': 'delta_w', 'delta_mix_post_g': 'delta_w', 'delta_mlp_pre_g': 'delta_w', 'delta_mlp_post_g': 'delta_w', 'new_m_pool_w': 'new_m', 'new_m_pool_scale': 'new_m', 'new_m_w_q': 'new_m', 'new_m_w_kv': 'new_m', 'new_m_kv_norm_g': 'new_m', 'new_m_w_o': 'new_m', 'new_m_w_up': 'new_m', 'new_m_w_down': 'new_m', 'new_m_mix_pre_g': 'new_m', 'new_m_mix_post_g': 'new_m', 'new_m_mlp_pre_g': 'new_m', 'new_m_mlp_post_g': 'new_m', 'new_v_pool_w': 'new_v', 'new_v_pool_scale': 'new_v', 'new_v_w_q': 'new_v', 'new_v_w_kv': 'new_v', 'new_v_kv_norm_g': 'new_v', 'new_v_w_o': 'new_v', 'new_v_w_up': 'new_v', 'new_v_w_down': 'new_v', 'new_v_mix_pre_g': 'new_v', 'new_v_mix_post_g': 'new_v', 'new_v_mlp_pre_g': 'new_v', 'new_v_mlp_post_g': 'new_v'}


def _forward(args):
    return _fwd_reference(*[args[k] for k in FWD_PARAMS])


def _output_shape():
    out = _jax.eval_shape(lambda: _forward(_fwd_setup_inputs(0)))
    return out.shape, out.dtype

N_MICROBATCH = 1
ADAM_LR = 0.001
ADAM_B1 = 0.9
ADAM_B2 = 0.999
ADAM_EPS = 1e-08
ADAM_WD = 0.01
ADAM_STEP = 10
PER_EXAMPLE_BATCH_AXIS = {'x': 0, 'loss_target': 0}
SHARED_INPUTS = []
_WEIGHT_DTYPES = {'pool_w': _jnp.float32, 'pool_scale': _jnp.float32, 'w_q': _jnp.float32, 'w_kv': _jnp.float32, 'kv_norm_g': _jnp.float32, 'w_o': _jnp.float32, 'w_up': _jnp.float32, 'w_down': _jnp.float32, 'mix_pre_g': _jnp.float32, 'mix_post_g': _jnp.float32, 'mlp_pre_g': _jnp.float32, 'mlp_post_g': _jnp.float32}
MOMENT_SCALE = {'pool_w': 3.083275e+00, 'pool_scale': 4.494687e+00, 'w_q': 3.152583e-01, 'w_kv': 1.341736e+01, 'kv_norm_g': 1.857987e+01, 'w_o': 1.387325e+01, 'w_up': 2.898046e+00, 'w_down': 1.743329e+01, 'mix_pre_g': 2.369071e+00, 'mix_post_g': 3.687636e+01, 'mlp_pre_g': 6.215655e+00, 'mlp_post_g': 3.796052e+01}


def _to_microbatches(a, axis):
    t = _jnp.moveaxis(a, axis, 0)
    t = t.reshape((N_MICROBATCH, t.shape[0] // N_MICROBATCH) + t.shape[1:])
    return _jnp.moveaxis(t, 1, axis + 1)


def setup_inputs(seed: int = 0) -> dict:
    inp = _fwd_setup_inputs(seed)
    key = _jax.random.fold_in(_jax.random.key(seed), 7919)
    shape, _ = _output_shape()
    out = dict(inp)
    out["loss_target"] = _jax.random.normal(_jax.random.fold_in(key, 0), shape, _jnp.float32)
    for i, name in enumerate(TWIN_WEIGHTS):
        w = inp[name].astype(_jnp.float32)
        if MOMENT_SCALE is None:
            s = _jnp.sqrt(_jnp.mean(_jnp.square(w)) + 1e-30)
        else:
            s = MOMENT_SCALE[name]
        km, kv = _jax.random.split(_jax.random.fold_in(key, i + 1))
        out[name] = w
        out["m_" + name] = s * _jax.random.normal(km, w.shape, _jnp.float32)
        out["v_" + name] = (s * s) * _jax.random.uniform(kv, w.shape, _jnp.float32, 0.5, 1.5)
    if N_MICROBATCH > 1:
        for name, axis in PER_EXAMPLE_BATCH_AXIS.items():
            out[name] = _to_microbatches(out[name], axis)
    return {'x': out['x'], 'pool_w': out['pool_w'], 'pool_scale': out['pool_scale'], 'w_q': out['w_q'], 'w_kv': out['w_kv'], 'kv_norm_g': out['kv_norm_g'], 'w_o': out['w_o'], 'w_up': out['w_up'], 'w_down': out['w_down'], 'mix_pre_g': out['mix_pre_g'], 'mix_post_g': out['mix_post_g'], 'mlp_pre_g': out['mlp_pre_g'], 'mlp_post_g': out['mlp_post_g'], 'loss_target': out['loss_target'], 'm_pool_w': out['m_pool_w'], 'm_pool_scale': out['m_pool_scale'], 'm_w_q': out['m_w_q'], 'm_w_kv': out['m_w_kv'], 'm_kv_norm_g': out['m_kv_norm_g'], 'm_w_o': out['m_w_o'], 'm_w_up': out['m_w_up'], 'm_w_down': out['m_w_down'], 'm_mix_pre_g': out['m_mix_pre_g'], 'm_mix_post_g': out['m_mix_post_g'], 'm_mlp_pre_g': out['m_mlp_pre_g'], 'm_mlp_post_g': out['m_mlp_post_g'], 'v_pool_w': out['v_pool_w'], 'v_pool_scale': out['v_pool_scale'], 'v_w_q': out['v_w_q'], 'v_w_kv': out['v_w_kv'], 'v_kv_norm_g': out['v_kv_norm_g'], 'v_w_o': out['v_w_o'], 'v_w_up': out['v_w_up'], 'v_w_down': out['v_w_down'], 'v_mix_pre_g': out['v_mix_pre_g'], 'v_mix_post_g': out['v_mix_post_g'], 'v_mlp_pre_g': out['v_mlp_pre_g'], 'v_mlp_post_g': out['v_mlp_post_g']}


def _loss(weights, diff, rest, loss_target):
    with _jax.named_scope("forward"):
        args = {**rest, TWIN_DIFF_INPUT: diff, **{k: w.astype(_WEIGHT_DTYPES[k]) for k, w in weights.items()}}
        y = _forward(args)
    with _jax.named_scope("loss_head"):
        err = _jnp.square(y.astype(_jnp.float32) - loss_target)
        return 0.5 * _jnp.sum(_jnp.mean(err, axis=-1)) if err.ndim else 0.5 * err


def _adamw(w, g, m, v):
    m = ADAM_B1 * m + (1.0 - ADAM_B1) * g
    v = ADAM_B2 * v + (1.0 - ADAM_B2) * _jnp.square(g)
    m_hat = m / (1.0 - ADAM_B1 ** ADAM_STEP)
    v_hat = v / (1.0 - ADAM_B2 ** ADAM_STEP)
    delta = -ADAM_LR * (m_hat / (_jnp.sqrt(v_hat) + ADAM_EPS) + ADAM_WD * w)
    return delta, m, v


def reference(x, pool_w, pool_scale, w_q, w_kv, kv_norm_g, w_o, w_up, w_down, mix_pre_g, mix_post_g, mlp_pre_g, mlp_post_g, loss_target, m_pool_w, m_pool_scale, m_w_q, m_w_kv, m_kv_norm_g, m_w_o, m_w_up, m_w_down, m_mix_pre_g, m_mix_post_g, m_mlp_pre_g, m_mlp_post_g, v_pool_w, v_pool_scale, v_w_q, v_w_kv, v_kv_norm_g, v_w_o, v_w_up, v_w_down, v_mix_pre_g, v_mix_post_g, v_mlp_pre_g, v_mlp_post_g):
    given = dict(x=x, pool_w=pool_w, pool_scale=pool_scale, w_q=w_q, w_kv=w_kv, kv_norm_g=kv_norm_g, w_o=w_o, w_up=w_up, w_down=w_down, mix_pre_g=mix_pre_g, mix_post_g=mix_post_g, mlp_pre_g=mlp_pre_g, mlp_post_g=mlp_post_g, loss_target=loss_target, m_pool_w=m_pool_w, m_pool_scale=m_pool_scale, m_w_q=m_w_q, m_w_kv=m_w_kv, m_kv_norm_g=m_kv_norm_g, m_w_o=m_w_o, m_w_up=m_w_up, m_w_down=m_w_down, m_mix_pre_g=m_mix_pre_g, m_mix_post_g=m_mix_post_g, m_mlp_pre_g=m_mlp_pre_g, m_mlp_post_g=m_mlp_post_g, v_pool_w=v_pool_w, v_pool_scale=v_pool_scale, v_w_q=v_w_q, v_w_kv=v_w_kv, v_kv_norm_g=v_kv_norm_g, v_w_o=v_w_o, v_w_up=v_w_up, v_w_down=v_w_down, v_mix_pre_g=v_mix_pre_g, v_mix_post_g=v_mix_post_g, v_mlp_pre_g=v_mlp_pre_g, v_mlp_post_g=v_mlp_post_g)
    weights = {n: given[n] for n in TWIN_WEIGHTS}
    shared = {n: given[n] for n in SHARED_INPUTS}
    per_example = {n: given[n] for n in ['x']}
    grad_fn = _jax.value_and_grad(_loss, argnums=(0, 1))

    def one_microbatch(ex, loss_target):
        ex = dict(ex)
        diff = ex.pop(TWIN_DIFF_INPUT)
        return grad_fn(weights, diff, {**shared, **ex}, loss_target)

    if N_MICROBATCH == 1:
        loss, (grad_w, grad_x) = one_microbatch(per_example, given["loss_target"])
    else:
        def body(carry, xs):
            loss_sum, grad_sum = carry
            l_k, (gw_k, gx_k) = one_microbatch(xs[0], xs[1])
            with _jax.named_scope("update"):
                return (loss_sum + l_k, _jax.tree.map(_jnp.add, grad_sum, gw_k)), gx_k

        init = (_jnp.zeros((), _jnp.float32), _jax.tree.map(_jnp.zeros_like, weights))
        (loss, grad_w), grad_x = _jax.lax.scan(body, init, (per_example, given["loss_target"]))
    with _jax.named_scope("update"):
        delta_w, new_m, new_v = {}, {}, {}
        for n in TWIN_WEIGHTS:
            delta_w[n], new_m[n], new_v[n] = _adamw(weights[n], grad_w[n], given["m_" + n], given["v_" + n])
    return (loss, grad_x, *[grad_w[n] for n in TWIN_WEIGHTS], *[delta_w[n] for n in TWIN_WEIGHTS],
            *[new_m[n] for n in TWIN_WEIGHTS], *[new_v[n] for n in TWIN_WEIGHTS])
```

```python
import functools

import jax
import jax.numpy as jnp
from jax import lax
from jax.experimental import pallas as pl
from jax.experimental.pallas import tpu as pltpu

F32 = jnp.float32
BF16 = jnp.bfloat16

EPS = 1e-6
HEAD_DIM = 64
LANES = 128
POOL_WINDOWS = (2, 4, 8, 16)
HALO = 16
N_DEV = 8
MESH_AXES = ("x", "y", "c")

ADAM_LR = 0.001
ADAM_B1 = 0.9
ADAM_B2 = 0.999
ADAM_EPS = 1e-08
ADAM_WD = 0.01
ADAM_STEP = 10

ROW_TILE = 256
ATT_TILE = 256
GAIN_ROWS = 40


def _tile(n, want):
    return want if n % want == 0 else n


def matmul(a, b, *, name, ta=False, tb=False, a_layer=None, b_layer=None, out_dtype=F32,
           a_fn=None, epi=None, epi_in=None, out_scale=None, out_into=None, tm=512, tn=512, tk=1024):
    a2 = a.shape[1:] if a_layer is not None else a.shape
    b2 = b.shape[1:] if b_layer is not None else b.shape
    (K, M) = a2 if ta else a2[::-1]
    if not ta:
        M, K = a2
    if tb:
        N, Kb = b2
    else:
        Kb, N = b2
    assert K == Kb, (a.shape, b.shape)
    tm, tn, tk = _tile(M, tm), _tile(N, tn), _tile(K, tk)
    nk = K // tk
    grid = (M // tm, N // tn, nk)

    def lead(layer, shape, imap):
        if layer is None:
            return pl.BlockSpec(shape, imap)
        return pl.BlockSpec((None,) + shape, lambda i, j, k: (layer,) + imap(i, j, k))

    a_spec = lead(a_layer, (tk, tm) if ta else (tm, tk), (lambda i, j, k: (k, i)) if ta else (lambda i, j, k: (i, k)))
    b_spec = lead(b_layer, (tn, tk) if tb else (tk, tn), (lambda i, j, k: (j, k)) if tb else (lambda i, j, k: (k, j)))
    in_specs = [a_spec, b_spec]
    operands = [a, b]
    if epi is not None:
        in_specs.append(pl.BlockSpec((tm, tn), lambda i, j, k: (i, j)))
        operands.append(epi_in)
    aliases = {}
    if out_into is not None:
        buf, layer = out_into
        assert buf.shape[1:] == (M, N) and buf.dtype == out_dtype
        out_shape = jax.ShapeDtypeStruct(buf.shape, buf.dtype)
        out_spec = pl.BlockSpec((None, tm, tn), lambda i, j, k: (layer, i, j))
        in_specs.append(pl.BlockSpec(memory_space=pl.ANY))
        operands.append(buf)
        aliases = {len(operands) - 1: 0}
    else:
        out_shape = jax.ShapeDtypeStruct((M, N), out_dtype)
        out_spec = pl.BlockSpec((tm, tn), lambda i, j, k: (i, j))
    dims = (((0 if ta else 1,), (1 if tb else 0,)), ((), ()))
    n_in = len(operands)

    def body(*refs):
        a_ref, b_ref = refs[0], refs[1]
        e_ref = refs[2] if epi is not None else None
        o_ref, acc_ref = refs[n_in], refs[n_in + 1]
        k = pl.program_id(2)

        @pl.when(k == 0)
        def _():
            acc_ref[...] = jnp.zeros_like(acc_ref)

        av = a_ref[...]
        if a_fn is not None:
            av = a_fn(av)
        acc_ref[...] += lax.dot_general(av.astype(BF16), b_ref[...].astype(BF16), dims,
                                        preferred_element_type=F32)

        @pl.when(k == nk - 1)
        def _():
            r = acc_ref[...]
            if epi is not None:
                r = epi(r, e_ref[...])
            if out_scale is not None:
                r = r * out_scale
            o_ref[...] = r.astype(out_dtype)

    return pl.pallas_call(
        body, name=name, grid=grid, in_specs=in_specs, out_specs=out_spec, out_shape=out_shape,
        scratch_shapes=[pltpu.VMEM((tm, tn), F32)], input_output_aliases=aliases,
        compiler_params=pltpu.CompilerParams(dimension_semantics=("parallel", "parallel", "arbitrary")),
    )(*operands)


def _relu2(u):
    r = jnp.maximum(u, 0.0)
    return r * r


def _relu2_grad(acc, u):
    return acc * (2.0 * jnp.maximum(u, 0.0))


def rowwise(fn, rows, vecs, out_rows, n_acc, *, name, tile=ROW_TILE):
    S = rows[0].shape[0]
    tile = _tile(S, tile)
    n_rows, n_vecs, n_out = len(rows), len(vecs), len(out_rows)
    acc_cols = [None] * n_acc

    def body(*refs):
        ins = [r[...] for r in refs[:n_rows + n_vecs]]
        outs = refs[n_rows + n_vecs:]
        ro, ac = fn(*ins)
        assert len(ro) == n_out and len(ac) == n_acc
        for r, o in zip(outs[:n_out], ro):
            r[...] = o.astype(r.dtype)
        i = pl.program_id(0)
        for r, a in zip(outs[n_out:], ac):
            @pl.when(i == 0)
            def _():
                r[...] = jnp.zeros_like(r)
            r[...] += a

    acc_shapes = jax.eval_shape(
        lambda *xs: fn(*xs)[1],
        *[jax.ShapeDtypeStruct((tile, r.shape[1]), r.dtype) for r in rows],
        *[jax.ShapeDtypeStruct(v.shape, v.dtype) for v in vecs])
    in_specs = [pl.BlockSpec((tile, r.shape[1]), lambda i: (i, 0)) for r in rows]
    in_specs += [pl.BlockSpec(v.shape, lambda i: (0, 0)) for v in vecs]
    out_specs = [pl.BlockSpec((tile, c), lambda i: (i, 0)) for c, _ in out_rows]
    out_specs += [pl.BlockSpec(a.shape, lambda i: (0, 0)) for a in acc_shapes]
    out_shape = [jax.ShapeDtypeStruct((S, c), dt) for c, dt in out_rows]
    out_shape += [jax.ShapeDtypeStruct(a.shape, F32) for a in acc_shapes]
    del acc_cols
    return pl.pallas_call(
        body, name=name, grid=(S // tile,), in_specs=in_specs, out_specs=out_specs, out_shape=out_shape,
        compiler_params=pltpu.CompilerParams(dimension_semantics=("arbitrary",)),
    )(*rows, *vecs)


def _rms(x, g):
    r = lax.rsqrt(jnp.mean(x * x, axis=-1, keepdims=True) + EPS)
    return x * r * g


def _rms_bwd(x, g, dy):
    r = lax.rsqrt(jnp.mean(x * x, axis=-1, keepdims=True) + EPS)
    xh = x * r
    dyg = dy * g
    dx = r * (dyg - xh * jnp.mean(dyg * xh, axis=-1, keepdims=True))
    dg = jnp.sum(dy * xh, axis=0, keepdims=True)
    return dx, dg


def norm_only(x, g, dtype, *, name):
    D = x.shape[1]
    return rowwise(lambda xv, gv: ([_rms(xv, gv)], []), [x], [g], [(D, dtype)], 0, name=name)[0]


def residual_norms(x, m, g_post, next_gs, next_dtypes, *, name):
    D = x.shape[1]

    def fn(xv, mv, gp, *gs):
        xn = xv + _rms(mv, gp)
        return [xn] + [_rms(xn, g) for g in gs], []

    return rowwise(fn, [x, m], [g_post] + list(next_gs), [(D, F32)] + [(D, dt) for dt in next_dtypes], 0, name=name)


def residual_loss(x, m, g_post, target, *, name):
    D = x.shape[1]

    def fn(xv, mv, tv, gp):
        e = xv + _rms(mv, gp) - tv
        return [e * (1.0 / D)], [jnp.sum(e * e, axis=0, keepdims=True) * (0.5 / D)]

    return rowwise(fn, [x, m, target], [g_post], [(D, F32)], 1, name=name)


def post_norm_bwd(d, g_post, dy, dtype, *, name):
    D = d.shape[1]

    def fn(dv, dyv, gp):
        dd, dg = _rms_bwd(dv, gp, dyv)
        return [dd], [dg]

    return rowwise(fn, [d, dy], [g_post], [(D, dtype)], 1, name=name)


def mid_bwd(dy, x_mid, dh2, m, g_mlp_pre, g_mix_post, dm_dtype, *, name):
    D = dy.shape[1]

    def fn(dyv, xm, dh, mv, gpre, gpost):
        dx, dg_pre = _rms_bwd(xm, gpre, dh)
        dxm = dyv + dx
        dm, dg_post = _rms_bwd(mv, gpost, dxm)
        return [dxm, dm], [dg_pre, dg_post]

    return rowwise(fn, [dy, x_mid, dh2, m], [g_mlp_pre, g_mix_post], [(D, F32), (D, dm_dtype)], 2, name=name)


def pre_norm_bwd(dxm, x, dhs, gs, *, name):
    D = x.shape[1]
    n = len(dhs)

    def fn(dxv, xv, *rest):
        dh, g = rest[:n], rest[n:]
        out, accs = dxv, []
        for k in range(n):
            dx, dg = _rms_bwd(xv, g[k], dh[k])
            out = out + dx
            accs.append(dg)
        return [out], accs

    return rowwise(fn, [dxm, x] + list(dhs), list(gs), [(D, F32)], n, name=name)


def concat_cast(a, b, dtype, *, name):
    Da, Db = a.shape[1], b.shape[1]
    return rowwise(lambda av, bv: ([jnp.concatenate([av, bv], axis=1)], []), [a, b], [], [(Da + Db, dtype)], 0,
                   name=name)[0]


def _window_sum(e, window, total_rows, backward):
    s, k = e, 1
    while k < window:
        s = s + pltpu.roll(s, (total_rows - k) if backward else k, 0)
        k *= 2
    return s


def pool_fwd(h, w, scale, *, name):
    S, D = h.shape
    G = len(POOL_WINDOWS)
    GC = D // G
    tile = _tile(S, ROW_TILE)
    hb = tile // HALO

    def body(hc_ref, hp_ref, w_ref, sc_ref, o_ref):
        i = pl.program_id(0)
        prev = jnp.where(i > 0, hp_ref[...], 0.0)
        ext = jnp.concatenate([prev, hc_ref[...]], axis=0)
        t = i * tile + lax.broadcasted_iota(jnp.int32, (tile, 1), 0)
        outs = []
        for g, window in enumerate(POOL_WINDOWS):
            e = ext[:, g * GC:(g + 1) * GC]
            s = _window_sum(e, window, HALO + tile, False)[HALO:, :]
            cnt = jnp.minimum(t + 1, window).astype(F32)
            y = s / cnt - e[HALO:, :]
            outs.append(jnp.dot(y.astype(BF16), w_ref[g], preferred_element_type=F32))
        o_ref[...] = jnp.concatenate(outs, axis=1) * sc_ref[...]

    return pl.pallas_call(
        body, name=name, grid=(S // tile,),
        in_specs=[pl.BlockSpec((tile, D), lambda i: (i, 0)),
                  pl.BlockSpec((HALO, D), lambda i: (jnp.maximum(i * hb - 1, 0), 0)),
                  pl.BlockSpec((G, GC, GC), lambda i: (0, 0, 0)),
                  pl.BlockSpec((1, D), lambda i: (0, 0))],
        out_specs=pl.BlockSpec((tile, D), lambda i: (i, 0)),
        out_shape=jax.ShapeDtypeStruct((S, D), F32),
        compiler_params=pltpu.CompilerParams(dimension_semantics=("parallel",)),
    )(h, h, w, scale)


def pool_bwd(h, dm, w, scale, *, name):
    S, D = h.shape
    G = len(POOL_WINDOWS)
    GC = D // G
    tile = _tile(S, ROW_TILE)
    hb = tile // HALO
    n_tiles = S // tile
    last_halo = S // HALO - 1

    def body(hc_ref, hp_ref, dmc_ref, dmn_ref, w_ref, sc_ref, dh_ref, dw_ref, dsc_ref):
        i = pl.program_id(0)

        @pl.when(i == 0)
        def _():
            dw_ref[...] = jnp.zeros_like(dw_ref)
            dsc_ref[...] = jnp.zeros_like(dsc_ref)

        prev = jnp.where(i > 0, hp_ref[...], 0.0)
        ext = jnp.concatenate([prev, hc_ref[...]], axis=0)
        nxt = jnp.where(i < n_tiles - 1, dmn_ref[...], 0.0)
        dmc = dmc_ref[...]
        dm_ext = jnp.concatenate([dmc, nxt], axis=0)
        t = i * tile + lax.broadcasted_iota(jnp.int32, (tile, 1), 0)
        t_ext = i * tile + lax.broadcasted_iota(jnp.int32, (tile + HALO, 1), 0)
        dhs, dscs = [], []
        for g, window in enumerate(POOL_WINDOWS):
            cols = slice(g * GC, (g + 1) * GC)
            e = ext[:, cols]
            s = _window_sum(e, window, HALO + tile, False)[HALO:, :]
            y = (s / jnp.minimum(t + 1, window).astype(F32) - e[HALO:, :]).astype(BF16)
            wg = w_ref[g]
            ypre = jnp.dot(y, wg, preferred_element_type=F32)
            dscs.append(jnp.sum(dmc[:, cols] * ypre, axis=0, keepdims=True))
            dyp = (dm_ext[:, cols] * sc_ref[:, cols]).astype(BF16)
            dw_ref[g] += lax.dot_general(y, dyp[:tile, :], (((0,), (0,)), ((), ())), preferred_element_type=F32)
            dy = lax.dot_general(dyp, wg, (((1,), (1,)), ((), ())), preferred_element_type=F32)
            r = dy / jnp.minimum(t_ext + 1, window).astype(F32)
            sr = _window_sum(r, window, tile + HALO, True)
            dhs.append(sr[:tile, :] - dy[:tile, :])
        dh_ref[...] = jnp.concatenate(dhs, axis=1)
        dsc_ref[...] += jnp.concatenate(dscs, axis=1)

    return pl.pallas_call(
        body, name=name, grid=(n_tiles,),
        in_specs=[pl.BlockSpec((tile, D), lambda i: (i, 0)),
                  pl.BlockSpec((HALO, D), lambda i: (jnp.maximum(i * hb - 1, 0), 0)),
                  pl.BlockSpec((tile, D), lambda i: (i, 0)),
                  pl.BlockSpec((HALO, D), lambda i: (jnp.minimum((i + 1) * hb, last_halo), 0)),
                  pl.BlockSpec((G, GC, GC), lambda i: (0, 0, 0)),
                  pl.BlockSpec((1, D), lambda i: (0, 0))],
        out_specs=[pl.BlockSpec((tile, D), lambda i: (i, 0)),
                   pl.BlockSpec((G, GC, GC), lambda i: (0, 0, 0)),
                   pl.BlockSpec((1, D), lambda i: (0, 0))],
        out_shape=[jax.ShapeDtypeStruct((S, D), F32), jax.ShapeDtypeStruct((G, GC, GC), F32),
                   jax.ShapeDtypeStruct((1, D), F32)],
        compiler_params=pltpu.CompilerParams(dimension_semantics=("arbitrary",)),
    )(h, h, dm, dm, w, scale)


def _split_dot(a, u):
    hi = a.astype(BF16)
    lo = (a - hi.astype(F32)).astype(BF16)
    return jnp.dot(hi, u, preferred_element_type=F32) + jnp.dot(lo, u, preferred_element_type=F32)


def _scores(qh, kj, i, j, T):
    z = lax.dot_general(qh, kj, (((1,), (1,)), ((), ())), preferred_element_type=F32)
    rows = lax.broadcasted_iota(jnp.int32, (T, T), 0)
    cols = lax.broadcasted_iota(jnp.int32, (T, T), 1)
    mask = (j * T + cols) < (i * T + rows)
    sp = jnp.maximum(z, 0.0) + jnp.log1p(jnp.exp(-jnp.abs(z)))
    return z, mask, sp, jnp.where(mask, -sp, 0.0)


def _tri(T, later):
    rows = lax.broadcasted_iota(jnp.int32, (T, T), 0)
    cols = lax.broadcasted_iota(jnp.int32, (T, T), 1)
    return jnp.where((rows > cols) if later else (rows < cols), 1.0, 0.0).astype(BF16)


def attn_fwd(q, kv, *, name):
    S, D = q.shape
    P = D // LANES
    T = _tile(S, ATT_TILE)

    def body(q_ref, k_ref, v_ref, o_ref):
        i = pl.program_id(1)
        lane = lax.broadcasted_iota(jnp.int32, (1, LANES), 1)
        u_later = _tri(T, True)
        q2 = q_ref[...]
        out = jnp.zeros((T, LANES), F32)
        for hh in range(LANES // HEAD_DIM):
            hm = (lane // HEAD_DIM) == hh
            qh = jnp.where(hm, q2, jnp.zeros_like(q2))

            def step(n, carry):
                c, acc = carry
                j = i - n
                off = pl.multiple_of(j * T, T)
                kj = k_ref[pl.ds(off, T), :]
                vj = v_ref[pl.ds(off, T), :]
                vj = jnp.where(hm, vj, jnp.zeros_like(vj))
                z, mask, sp, lg = _scores(qh, kj, i, j, T)
                e = _split_dot(lg, u_later) + c
                a = jnp.where(mask, jnp.exp(z - sp + e), 0.0)
                acc = acc + jnp.dot(a.astype(BF16), vj, preferred_element_type=F32)
                return c + jnp.sum(lg, axis=1, keepdims=True), acc

            _, acc = lax.fori_loop(0, i + 1, step, (jnp.zeros((T, 1), F32), jnp.zeros((T, LANES), F32)))
            out = out + acc
        o_ref[...] = out.astype(o_ref.dtype)

    return pl.pallas_call(
        body, name=name, grid=(P, S // T),
        in_specs=[pl.BlockSpec((T, LANES), lambda p, i: (i, p)),
                  pl.BlockSpec((S, LANES), lambda p, i: (0, p)),
                  pl.BlockSpec((S, LANES), lambda p, i: (0, P + p))],
        out_specs=pl.BlockSpec((T, LANES), lambda p, i: (i, p)),
        out_shape=jax.ShapeDtypeStruct((S, D), BF16),
        compiler_params=pltpu.CompilerParams(dimension_semantics=("parallel", "arbitrary")),
    )(q, kv, kv)


def attn_bwd(q, kv, do, dk_in, dv_in, *, name):
    S, D = q.shape
    P = D // LANES
    T = _tile(S, ATT_TILE)
    nb = S // T

    def body(q_ref, k_ref, v_ref, do_ref, dki_ref, dvi_ref, dq_ref, dk_ref, dv_ref, g_scr, s_scr):
        i = pl.program_id(1)

        @pl.when(i == 0)
        def _():
            dk_ref[...] = dki_ref[...]
            dv_ref[...] = dvi_ref[...]

        lane = lax.broadcasted_iota(jnp.int32, (1, LANES), 1)
        u_later = _tri(T, True)
        u_earlier = _tri(T, False)
        q2 = q_ref[...]
        do2 = do_ref[...]
        dq = jnp.zeros((T, LANES), F32)
        for hh in range(LANES // HEAD_DIM):
            hm = (lane // HEAD_DIM) == hh
            qh = jnp.where(hm, q2, jnp.zeros_like(q2))
            doh = jnp.where(hm, do2, jnp.zeros_like(do2))

            def step1(n, c):
                j = i - n
                off = pl.multiple_of(j * T, T)
                kj = k_ref[pl.ds(off, T), :]
                vj = v_ref[pl.ds(off, T), :]
                z, mask, sp, lg = _scores(qh, kj, i, j, T)
                e = _split_dot(lg, u_later) + c
                a = jnp.where(mask, jnp.exp(z - sp + e), 0.0)
                da = lax.dot_general(doh, vj, (((1,), (1,)), ((), ())), preferred_element_type=F32)
                g_scr[j] = da * a
                s_scr[j] = jnp.where(mask, jnp.exp(z - sp), 0.0)
                dv_ref[pl.ds(off, T), :] += lax.dot_general(a.astype(BF16), doh, (((0,), (0,)), ((), ())),
                                                           preferred_element_type=F32)
                return c + jnp.sum(lg, axis=1, keepdims=True)

            lax.fori_loop(0, i + 1, step1, jnp.zeros((T, 1), F32))

            def step2(j, carry):
                c, acc = carry
                off = pl.multiple_of(j * T, T)
                kj = k_ref[pl.ds(off, T), :]
                kj = jnp.where(hm, kj, jnp.zeros_like(kj))
                g = g_scr[j]
                sg = s_scr[j]
                cum = _split_dot(g, u_earlier) + c
                dz = (g - sg * (g + cum)).astype(BF16)
                acc = acc + jnp.dot(dz, kj, preferred_element_type=F32)
                dk_ref[pl.ds(off, T), :] += lax.dot_general(dz, qh, (((0,), (0,)), ((), ())),
                                                           preferred_element_type=F32)
                return c + jnp.sum(g, axis=1, keepdims=True), acc

            _, acc = lax.fori_loop(0, i + 1, step2, (jnp.zeros((T, 1), F32), jnp.zeros((T, LANES), F32)))
            dq = dq + acc
        dq_ref[...] = (dq * (HEAD_DIM ** -0.5)).astype(dq_ref.dtype)

    return pl.pallas_call(
        body, name=name, grid=(P, nb),
        in_specs=[pl.BlockSpec((T, LANES), lambda p, i: (i, p)),
                  pl.BlockSpec((S, LANES), lambda p, i: (0, p)),
                  pl.BlockSpec((S, LANES), lambda p, i: (0, P + p)),
                  pl.BlockSpec((T, LANES), lambda p, i: (i, p)),
                  pl.BlockSpec((S, LANES), lambda p, i: (0, p)),
                  pl.BlockSpec((S, LANES), lambda p, i: (0, p))],
        out_specs=[pl.BlockSpec((T, LANES), lambda p, i: (i, p)),
                   pl.BlockSpec((S, LANES), lambda p, i: (0, p)),
                   pl.BlockSpec((S, LANES), lambda p, i: (0, p))],
        out_shape=[jax.ShapeDtypeStruct((S, D), BF16), jax.ShapeDtypeStruct((S, D), F32),
                   jax.ShapeDtypeStruct((S, D), F32)],
        scratch_shapes=[pltpu.VMEM((nb, T, T), F32), pltpu.VMEM((nb, T, T), F32)],
        compiler_params=pltpu.CompilerParams(dimension_semantics=("parallel", "arbitrary")),
    )(q, kv, kv, do, dk_in, dv_in)


def _window(ref, axis, dev, n):
    return ref.at[(slice(None),) * axis + (pl.ds(dev * n, n),)]


def all_gather(ops, *, name):
    n_ops = len(ops)
    out_shape = []
    for a, ax in ops:
        shp = list(a.shape)
        shp[ax] *= N_DEV
        out_shape.append(jax.ShapeDtypeStruct(tuple(shp), a.dtype))

    def body(*refs):
        ins, outs = refs[:n_ops], refs[n_ops:2 * n_ops]
        send_sems, recv_sems, local_sems = refs[2 * n_ops:]
        x, y, c = (lax.axis_index(n) for n in MESH_AXES)
        me, sibling = (x, y, c), (x, y, 1 - c)
        chips = [(1 - x, y), (x, 1 - y), (1 - x, 1 - y)]

        def rows(o, dev):
            px, py, pc = dev
            ax = ops[o][1]
            return _window(outs[o], ax, 4 * px + 2 * py + pc, ops[o][0].shape[ax])

        def copy(o, k, block, to, src=None):
            return pltpu.make_async_remote_copy(
                src_ref=rows(o, block) if src is None else src, dst_ref=rows(o, block),
                send_sem=send_sems.at[o, k], recv_sem=recv_sems.at[o, k],
                device_id=to, device_id_type=pl.DeviceIdType.MESH)

        mine, first, passed = [], [], []
        for o in range(n_ops):
            cp = pltpu.make_async_copy(ins[o], rows(o, me), local_sems.at[o])
            cp.start()
            mine.append(cp)
            first.append(copy(o, 0, me, sibling, src=ins[o]))
            first += [copy(o, 1 + j, me, (*chip, c), src=ins[o]) for j, chip in enumerate(chips)]
        for cp in first:
            cp.start()
        for j, chip in enumerate(chips):
            for o in range(n_ops):
                copy(o, 1 + j, (*chip, c), me).wait_recv()
                cp = copy(o, 4 + j, (*chip, c), sibling)
                cp.start()
                passed.append(cp)
        for o in range(n_ops):
            copy(o, 0, sibling, me).wait_recv()
            for j, chip in enumerate(chips):
                copy(o, 4 + j, (*chip, 1 - c), me).wait_recv()
        for cp in first + passed:
            cp.wait_send()
        for cp in mine:
            cp.wait()

    any_spec = pl.BlockSpec(memory_space=pl.ANY)
    return pl.pallas_call(
        body, name=name, in_specs=[any_spec] * n_ops, out_specs=[any_spec] * n_ops, out_shape=out_shape,
        scratch_shapes=[pltpu.SemaphoreType.DMA((n_ops, 7)), pltpu.SemaphoreType.DMA((n_ops, 7)),
                        pltpu.SemaphoreType.DMA((n_ops,))],
        compiler_params=pltpu.CompilerParams(has_side_effects=True),
    )(*[a for a, _ in ops])


def scatter_blocks(ops, *, name):
    n_ops = len(ops)
    blocks = []
    for a, ax in ops:
        shp = list(a.shape)
        shp[ax] //= N_DEV
        blocks.append(tuple(shp))
    out_shape = [jax.ShapeDtypeStruct((N_DEV,) + b, a.dtype) for b, (a, _) in zip(blocks, ops)]

    def body(*refs):
        ins, outs = refs[:n_ops], refs[n_ops:2 * n_ops]
        send_sems, recv_sems, local_sems = refs[2 * n_ops:]
        x, y, c = (lax.axis_index(n) for n in MESH_AXES)
        me = 4 * x + 2 * y + c

        def block(o, dev):
            ax = ops[o][1]
            return _window(ins[o], ax, dev, blocks[o][ax])

        def peer_of(r):
            px = 1 - x if r & 4 else x
            py = 1 - y if r & 2 else y
            pc = 1 - c if r & 1 else c
            return (px, py, pc), 4 * px + 2 * py + pc

        sends, local = [], []
        for o in range(n_ops):
            cp = pltpu.make_async_copy(block(o, me), outs[o].at[me], local_sems.at[o])
            cp.start()
            local.append(cp)
            for r in range(1, N_DEV):
                peer, peer_id = peer_of(r)
                cp = pltpu.make_async_remote_copy(
                    src_ref=block(o, peer_id), dst_ref=outs[o].at[me],
                    send_sem=send_sems.at[o, r - 1], recv_sem=recv_sems.at[o, r - 1],
                    device_id=peer, device_id_type=pl.DeviceIdType.MESH)
                cp.start()
                sends.append(cp)
        for o in range(n_ops):
            for r in range(1, N_DEV):
                peer, peer_id = peer_of(r)
                pltpu.make_async_remote_copy(
                    src_ref=block(o, me), dst_ref=outs[o].at[peer_id],
                    send_sem=send_sems.at[o, r - 1], recv_sem=recv_sems.at[o, r - 1],
                    device_id=peer, device_id_type=pl.DeviceIdType.MESH).wait_recv()
        for cp in sends:
            cp.wait_send()
        for cp in local:
            cp.wait()

    any_spec = pl.BlockSpec(memory_space=pl.ANY)
    return pl.pallas_call(
        body, name=name, in_specs=[any_spec] * n_ops, out_specs=[any_spec] * n_ops, out_shape=out_shape,
        scratch_shapes=[pltpu.SemaphoreType.DMA((n_ops, 7)), pltpu.SemaphoreType.DMA((n_ops, 7)),
                        pltpu.SemaphoreType.DMA((n_ops,))],
        compiler_params=pltpu.CompilerParams(has_side_effects=True),
    )(*[a for a, _ in ops])


def _adamw(w, g, m, v):
    m = ADAM_B1 * m + (1.0 - ADAM_B1) * g
    v = ADAM_B2 * v + (1.0 - ADAM_B2) * (g * g)
    m_hat = m / (1.0 - ADAM_B1 ** ADAM_STEP)
    v_hat = v / (1.0 - ADAM_B2 ** ADAM_STEP)
    delta = -ADAM_LR * (m_hat / (jnp.sqrt(v_hat) + ADAM_EPS) + ADAM_WD * w)
    return delta, m, v


def reduce_adamw(parts, w, m, v, *, name):
    shape = w.shape
    C = shape[-1]
    R = w.size // C
    tile = _tile(R, 256)

    def body(p_ref, w_ref, m_ref, v_ref, g_ref, d_ref, nm_ref, nv_ref):
        g = p_ref[0].astype(F32)
        for s in range(1, N_DEV):
            g = g + p_ref[s].astype(F32)
        d, nm, nv = _adamw(w_ref[...], g, m_ref[...], v_ref[...])
        g_ref[...] = g
        d_ref[...] = d
        nm_ref[...] = nm
        nv_ref[...] = nv

    row = pl.BlockSpec((tile, C), lambda i: (i, 0))
    outs = pl.pallas_call(
        body, name=name, grid=(R // tile,),
        in_specs=[pl.BlockSpec((N_DEV, tile, C), lambda i: (0, i, 0)), row, row, row],
        out_specs=[row] * 4, out_shape=[jax.ShapeDtypeStruct((R, C), F32)] * 4,
        compiler_params=pltpu.CompilerParams(dimension_semantics=("parallel",)),
    )(parts.reshape(N_DEV, R, C), w.reshape(R, C), m.reshape(R, C), v.reshape(R, C))
    return [o.reshape(shape) for o in outs]


def gains_adamw(parts, groups, *, name):
    n = len(groups)

    def body(*refs):
        p_ref = refs[0]
        ins, outs = refs[1:1 + 3 * n], refs[1 + 3 * n:]
        for k in range(n):
            w_ref, m_ref, v_ref = ins[3 * k:3 * k + 3]
            L = w_ref.shape[0]
            g = p_ref[pl.ds(8 * k, L), :]
            for s in range(1, N_DEV):
                g = g + p_ref[pl.ds(s * GAIN_ROWS + 8 * k, L), :]
            d, nm, nv = _adamw(w_ref[...], g, m_ref[...], v_ref[...])
            for r, val in zip(outs[4 * k:4 * k + 4], (g, d, nm, nv)):
                r[...] = val

    flat = [a for grp in groups for a in grp]
    out_shape = [jax.ShapeDtypeStruct(grp[0].shape, F32) for grp in groups for _ in range(4)]
    outs = pl.pallas_call(body, name=name, out_shape=out_shape)(parts, *flat)
    return [outs[4 * k:4 * k + 4] for k in range(n)]


def kernel(x, pool_w, pool_scale, w_q, w_kv, kv_norm_g, w_o, w_up, w_down, mix_pre_g, mix_post_g, mlp_pre_g, mlp_post_g, loss_target, m_pool_w, m_pool_scale, m_w_q, m_w_kv, m_kv_norm_g, m_w_o, m_w_up, m_w_down, m_mix_pre_g, m_mix_post_g, m_mlp_pre_g, m_mlp_post_g, v_pool_w, v_pool_scale, v_w_q, v_w_kv, v_kv_norm_g, v_w_o, v_w_up, v_w_down, v_mix_pre_g, v_mix_post_g, v_mlp_pre_g, v_mlp_post_g):
    _, S, D = x.shape
    x0 = x.reshape(S, D)
    target = loss_target.reshape(S, D)
    depth = w_up.shape[0]
    n_pool = pool_w.shape[0]
    F = w_up.shape[2] * N_DEV
    G = pool_w.shape[1]
    GC = D // G

    def vec(a, l):
        return a[l].reshape(1, D)

    gathered = all_gather(
        [(w_q.astype(BF16), 1), (w_kv.astype(BF16), 1), (w_o.astype(BF16), 1), (w_up.astype(BF16), 2),
         (w_down.astype(BF16), 1), (pool_w.astype(BF16), 2), (pool_scale, 1)], name="gather_weights")
    wq, wkv, wo, wup, wdn, pw, psc = gathered

    saved = []
    xs = x0
    h1 = norm_only(xs, vec(mix_pre_g, 0), F32, name="norm_in")
    kv = hk = None
    dy = loss_rows = None
    for l in range(depth):
        is_pool = l < n_pool
        st = {"x": xs, "h1": h1}
        if is_pool:
            m = pool_fwd(h1, pw[l], psc[l].reshape(1, D), name=f"pool_fwd{l}")
        else:
            j = l - n_pool
            q = matmul(h1, wq, b_layer=j, out_dtype=BF16, out_scale=HEAD_DIM ** -0.5, name=f"q_proj{j}")
            o = attn_fwd(q, kv, name=f"attn_fwd{j}")
            m = matmul(o, wo, b_layer=j, name=f"o_proj{j}")
            st.update(q=q, o=o)
        x_mid, h2 = residual_norms(xs, m, vec(mix_post_g, l), [vec(mlp_pre_g, l)], [BF16], name=f"mix_out{l}")
        u = matmul(h2, wup, b_layer=l, name=f"mlp_up{l}")
        d = matmul(u, wdn, b_layer=l, a_fn=_relu2, tn=D, name=f"mlp_down{l}")
        st.update(m=m, x_mid=x_mid, h2=h2, u=u, d=d)
        saved.append(st)
        if l == depth - 1:
            dy, loss_rows = residual_loss(x_mid, d, vec(mlp_post_g, l), target, name="loss")
        elif l == n_pool - 1:
            xs, h1, hk = residual_norms(x_mid, d, vec(mlp_post_g, l), [vec(mix_pre_g, l + 1), kv_norm_g.reshape(1, D)],
                                        [BF16, BF16], name=f"mlp_out{l}")
            kv = matmul(hk, wkv, out_dtype=BF16, name="kv_proj")
        else:
            nxt_dt = F32 if l + 1 < n_pool else BF16
            xs, h1 = residual_norms(x_mid, d, vec(mlp_post_g, l), [vec(mix_pre_g, l + 1)], [nxt_dt], name=f"mlp_out{l}")
    loss = lax.psum(jnp.sum(loss_rows), MESH_AXES)

    g_wq = lax.empty((depth - n_pool, D, D), BF16)
    g_wo = lax.empty((depth - n_pool, D, D), BF16)
    g_wup = lax.empty((depth, D, F), BF16)
    g_wdn = lax.empty((depth, F, D), BF16)
    g_pw, g_psc = [None] * n_pool, [None] * n_pool
    gains = {k: [None] * depth for k in ("mix_pre", "mix_post", "mlp_pre", "mlp_post")}
    dk = jnp.zeros((S, D), F32)
    dv = jnp.zeros((S, D), F32)
    g_wkv = g_kvn = None
    for l in reversed(range(depth)):
        st = saved[l]
        is_pool = l < n_pool
        dd, gains["mlp_post"][l] = post_norm_bwd(st["d"], vec(mlp_post_g, l), dy, BF16, name=f"mlp_out_bwd{l}")
        du = matmul(dd, wdn, b_layer=l, tb=True, out_dtype=BF16, epi=_relu2_grad, epi_in=st["u"], name=f"mlp_du{l}")
        g_wdn = matmul(st["u"], dd, ta=True, a_fn=_relu2, out_dtype=BF16, out_into=(g_wdn, l), name=f"mlp_dwdn{l}")
        g_wup = matmul(st["h2"], du, ta=True, out_dtype=BF16, out_into=(g_wup, l), name=f"mlp_dwup{l}")
        dh2 = matmul(du, wup, b_layer=l, tb=True, tn=D, name=f"mlp_dh{l}")
        dxm, dm, gains["mlp_pre"][l], gains["mix_post"][l] = mid_bwd(
            dy, st["x_mid"], dh2, st["m"], vec(mlp_pre_g, l), vec(mix_post_g, l), F32 if is_pool else BF16,
            name=f"mid_bwd{l}")
        if is_pool:
            dh1, g_pw[l], g_psc[l] = pool_bwd(st["h1"], dm, pw[l], psc[l].reshape(1, D), name=f"pool_bwd{l}")
        else:
            j = l - n_pool
            do = matmul(dm, wo, b_layer=j, tb=True, out_dtype=BF16, name=f"o_proj_dx{j}")
            g_wo = matmul(st["o"], dm, ta=True, out_dtype=BF16, out_into=(g_wo, j), name=f"o_proj_dw{j}")
            dq, dk, dv = attn_bwd(st["q"], kv, do, dk, dv, name=f"attn_bwd{j}")
            g_wq = matmul(st["h1"], dq, ta=True, out_dtype=BF16, out_into=(g_wq, j), name=f"q_proj_dw{j}")
            dh1 = matmul(dq, wq, b_layer=j, tb=True, name=f"q_proj_dx{j}")
        if l == n_pool:
            dkv = concat_cast(dk, dv, BF16, name="dkv_pack")
            g_wkv = matmul(hk, dkv, ta=True, out_dtype=BF16, name="kv_proj_dw")
            dhk = matmul(dkv, wkv, tb=True, tn=D, name="kv_proj_dx")
            dy, gains["mix_pre"][l], g_kvn = pre_norm_bwd(
                dxm, st["x"], [dh1, dhk], [vec(mix_pre_g, l), kv_norm_g.reshape(1, D)], name=f"mix_in_bwd{l}")
        else:
            dy, gains["mix_pre"][l] = pre_norm_bwd(dxm, st["x"], [dh1], [vec(mix_pre_g, l)], name=f"mix_in_bwd{l}")
    grad_x = dy.reshape(x.shape)

    g_pool_w = jnp.stack(g_pw)
    g_pool_scale = jnp.concatenate(g_psc, axis=0)
    zero_rows = jnp.zeros((8 - depth, D), F32)
    gain_rows = []
    for k in ("mix_pre", "mix_post", "mlp_pre", "mlp_post"):
        gain_rows += gains[k] + [zero_rows]
    gain_rows += [g_kvn, jnp.zeros((7, D), F32)]
    gain_pack = jnp.concatenate(gain_rows, axis=0)
    gain_parts = all_gather([(gain_pack, 0)], name="gather_gain_grads")[0]
    parts = scatter_blocks(
        [(g_wq, 1), (g_wkv, 1), (g_wo, 1), (g_wup, 2), (g_wdn, 1), (g_pool_w, 2), (g_pool_scale, 1)],
        name="exchange_grads")
    big = [(w_q, m_w_q, v_w_q), (w_kv, m_w_kv, v_w_kv), (w_o, m_w_o, v_w_o), (w_up, m_w_up, v_w_up),
           (w_down, m_w_down, v_w_down), (pool_w, m_pool_w, v_pool_w), (pool_scale, m_pool_scale, v_pool_scale)]
    names = ["w_q", "w_kv", "w_o", "w_up", "w_down", "pool_w", "pool_scale"]
    res = {n: reduce_adamw(p, *wmv, name=f"adamw_{n}") for n, p, wmv in zip(names, parts, big)}
    gain_groups = [(mix_pre_g, m_mix_pre_g, v_mix_pre_g), (mix_post_g, m_mix_post_g, v_mix_post_g),
                   (mlp_pre_g, m_mlp_pre_g, v_mlp_pre_g), (mlp_post_g, m_mlp_post_g, v_mlp_post_g),
                   (kv_norm_g.reshape(1, D), m_kv_norm_g.reshape(1, D), v_kv_norm_g.reshape(1, D))]
    gres = gains_adamw(gain_parts, gain_groups, name="adamw_gains")
    for n, r in zip(["mix_pre_g", "mix_post_g", "mlp_pre_g", "mlp_post_g"], gres[:4]):
        res[n] = r
    res["kv_norm_g"] = [a.reshape(D) for a in gres[4]]

    order = ["pool_w", "pool_scale", "w_q", "w_kv", "kv_norm_g", "w_o", "w_up", "w_down",
             "mix_pre_g", "mix_post_g", "mlp_pre_g", "mlp_post_g"]
    out = [loss, grad_x]
    for k in range(4):
        out += [res[n][k] for n in order]
    return tuple(out)
```

```python
import functools

import jax
import jax.numpy as jnp
from jax import lax
from jax.experimental import pallas as pl
from jax.experimental.pallas import tpu as pltpu

F32 = jnp.float32
BF16 = jnp.bfloat16

EPS = 1e-6
HEAD_DIM = 64
LANES = 128
POOL_WINDOWS = (2, 4, 8, 16)
HALO = 16
N_DEV = 8
MESH_AXES = ("x", "y", "c")

ADAM_LR = 0.001
ADAM_B1 = 0.9
ADAM_B2 = 0.999
ADAM_EPS = 1e-08
ADAM_WD = 0.01
ADAM_STEP = 10

ROW_TILE = 256
ATT_TILE = 256
ROW_CHUNK = 32
GAIN_ROWS = 40


def _tile(n, want):
    return want if n % want == 0 else n


def matmul(a, b, *, name, ta=False, tb=False, a_layer=None, b_layer=None, out_dtype=F32,
           a_fn=None, epi=None, epi_in=None, out_scale=None, out_into=None, tm=1024, tn=1024, tk=1024):
    a2 = a.shape[1:] if a_layer is not None else a.shape
    b2 = b.shape[1:] if b_layer is not None else b.shape
    (K, M) = a2 if ta else a2[::-1]
    if not ta:
        M, K = a2
    if tb:
        N, Kb = b2
    else:
        Kb, N = b2
    assert K == Kb, (a.shape, b.shape)
    tm, tn, tk = _tile(M, tm), _tile(N, tn), _tile(K, tk)
    nk = K // tk
    grid = (M // tm, N // tn, nk)

    def lead(layer, shape, imap):
        if layer is None:
            return pl.BlockSpec(shape, imap)
        return pl.BlockSpec((None,) + shape, lambda i, j, k: (layer,) + imap(i, j, k))

    a_spec = lead(a_layer, (tk, tm) if ta else (tm, tk), (lambda i, j, k: (k, i)) if ta else (lambda i, j, k: (i, k)))
    b_spec = lead(b_layer, (tn, tk) if tb else (tk, tn), (lambda i, j, k: (j, k)) if tb else (lambda i, j, k: (k, j)))
    in_specs = [a_spec, b_spec]
    operands = [a, b]
    if epi is not None:
        in_specs.append(pl.BlockSpec((tm, tn), lambda i, j, k: (i, j)))
        operands.append(epi_in)
    aliases = {}
    if out_into is not None:
        buf, layer = out_into
        assert buf.shape[1:] == (M, N) and buf.dtype == out_dtype
        out_shape = jax.ShapeDtypeStruct(buf.shape, buf.dtype)
        out_spec = pl.BlockSpec((None, tm, tn), lambda i, j, k: (layer, i, j))
        in_specs.append(pl.BlockSpec(memory_space=pl.ANY))
        operands.append(buf)
        aliases = {len(operands) - 1: 0}
    else:
        out_shape = jax.ShapeDtypeStruct((M, N), out_dtype)
        out_spec = pl.BlockSpec((tm, tn), lambda i, j, k: (i, j))
    dims = (((0 if ta else 1,), (1 if tb else 0,)), ((), ()))
    n_in = len(operands)

    def body(*refs):
        a_ref, b_ref = refs[0], refs[1]
        e_ref = refs[2] if epi is not None else None
        o_ref, acc_ref = refs[n_in], refs[n_in + 1]
        k = pl.program_id(2)

        @pl.when(k == 0)
        def _():
            acc_ref[...] = jnp.zeros_like(acc_ref)

        av = a_ref[...]
        if a_fn is not None:
            av = a_fn(av)
        acc_ref[...] += lax.dot_general(av.astype(BF16), b_ref[...].astype(BF16), dims,
                                        preferred_element_type=F32)

        @pl.when(k == nk - 1)
        def _():
            r = acc_ref[...]
            if epi is not None:
                r = epi(r, e_ref[...])
            if out_scale is not None:
                r = r * out_scale
            o_ref[...] = r.astype(out_dtype)

    return pl.pallas_call(
        body, name=name, grid=grid, in_specs=in_specs, out_specs=out_spec, out_shape=out_shape,
        scratch_shapes=[pltpu.VMEM((tm, tn), F32)], input_output_aliases=aliases,
        compiler_params=pltpu.CompilerParams(dimension_semantics=("parallel", "parallel", "arbitrary")),
    )(*operands)


def _relu2(u):
    r = jnp.maximum(u, 0.0)
    return r * r


def _relu2_grad(acc, u):
    return acc * (2.0 * jnp.maximum(u, 0.0))


def rowwise(fn, rows, vecs, out_rows, n_acc, *, name, tile=ROW_TILE):
    S = rows[0].shape[0]
    tile = _tile(S, tile)
    n_rows, n_vecs, n_out = len(rows), len(vecs), len(out_rows)
    acc_cols = [None] * n_acc

    def body(*refs):
        ins = [r[...] for r in refs[:n_rows + n_vecs]]
        outs = refs[n_rows + n_vecs:]
        ro, ac = fn(*ins)
        assert len(ro) == n_out and len(ac) == n_acc
        for r, o in zip(outs[:n_out], ro):
            r[...] = o.astype(r.dtype)
        i = pl.program_id(0)
        for r, a in zip(outs[n_out:], ac):
            @pl.when(i == 0)
            def _():
                r[...] = jnp.zeros_like(r)
            r[...] += a

    acc_shapes = jax.eval_shape(
        lambda *xs: fn(*xs)[1],
        *[jax.ShapeDtypeStruct((tile, r.shape[1]), r.dtype) for r in rows],
        *[jax.ShapeDtypeStruct(v.shape, v.dtype) for v in vecs])
    in_specs = [pl.BlockSpec((tile, r.shape[1]), lambda i: (i, 0)) for r in rows]
    in_specs += [pl.BlockSpec(v.shape, lambda i: (0, 0)) for v in vecs]
    out_specs = [pl.BlockSpec((tile, c), lambda i: (i, 0)) for c, _ in out_rows]
    out_specs += [pl.BlockSpec(a.shape, lambda i: (0, 0)) for a in acc_shapes]
    out_shape = [jax.ShapeDtypeStruct((S, c), dt) for c, dt in out_rows]
    out_shape += [jax.ShapeDtypeStruct(a.shape, F32) for a in acc_shapes]
    del acc_cols
    return pl.pallas_call(
        body, name=name, grid=(S // tile,), in_specs=in_specs, out_specs=out_specs, out_shape=out_shape,
        compiler_params=pltpu.CompilerParams(dimension_semantics=("arbitrary",)),
    )(*rows, *vecs)


def _rms(x, g):
    r = lax.rsqrt(jnp.mean(x * x, axis=-1, keepdims=True) + EPS)
    return x * r * g


def _rms_bwd(x, g, dy):
    r = lax.rsqrt(jnp.mean(x * x, axis=-1, keepdims=True) + EPS)
    xh = x * r
    dyg = dy * g
    dx = r * (dyg - xh * jnp.mean(dyg * xh, axis=-1, keepdims=True))
    dg = jnp.sum(dy * xh, axis=0, keepdims=True)
    return dx, dg


def norm_only(x, g, dtype, *, name):
    D = x.shape[1]
    return rowwise(lambda xv, gv: ([_rms(xv, gv)], []), [x], [g], [(D, dtype)], 0, name=name)[0]


def residual_norms(x, m, g_post, next_gs, next_dtypes, *, name):
    D = x.shape[1]

    def fn(xv, mv, gp, *gs):
        xn = xv + _rms(mv, gp)
        return [xn] + [_rms(xn, g) for g in gs], []

    return rowwise(fn, [x, m], [g_post] + list(next_gs), [(D, F32)] + [(D, dt) for dt in next_dtypes], 0, name=name)


def residual_loss(x, m, g_post, target, *, name):
    D = x.shape[1]

    def fn(xv, mv, tv, gp):
        e = xv + _rms(mv, gp) - tv
        return [e * (1.0 / D)], [jnp.sum(e * e, axis=0, keepdims=True) * (0.5 / D)]

    return rowwise(fn, [x, m, target], [g_post], [(D, F32)], 1, name=name)


def post_norm_bwd(d, g_post, dy, dtype, *, name):
    D = d.shape[1]

    def fn(dv, dyv, gp):
        dd, dg = _rms_bwd(dv, gp, dyv)
        return [dd], [dg]

    return rowwise(fn, [d, dy], [g_post], [(D, dtype)], 1, name=name)


def mid_bwd(dy, x_mid, dh2, m, g_mlp_pre, g_mix_post, dm_dtype, *, name):
    D = dy.shape[1]

    def fn(dyv, xm, dh, mv, gpre, gpost):
        dx, dg_pre = _rms_bwd(xm, gpre, dh)
        dxm = dyv + dx
        dm, dg_post = _rms_bwd(mv, gpost, dxm)
        return [dxm, dm], [dg_pre, dg_post]

    return rowwise(fn, [dy, x_mid, dh2, m], [g_mlp_pre, g_mix_post], [(D, F32), (D, dm_dtype)], 2, name=name)


def pre_norm_bwd(dxm, x, dhs, gs, *, name):
    D = x.shape[1]
    n = len(dhs)

    def fn(dxv, xv, *rest):
        dh, g = rest[:n], rest[n:]
        out, accs = dxv, []
        for k in range(n):
            dx, dg = _rms_bwd(xv, g[k], dh[k])
            out = out + dx
            accs.append(dg)
        return [out], accs

    return rowwise(fn, [dxm, x] + list(dhs), list(gs), [(D, F32)], n, name=name)


def sum_concat_cast(pairs, dtype, *, name):
    C = pairs[0][0].shape[1]
    n = len(pairs)

    def fn(*xs):
        return [jnp.concatenate([sum(xs[:n]), sum(xs[n:])], axis=1)], []

    return rowwise(fn, [a for a, _ in pairs] + [b for _, b in pairs], [], [(2 * C, dtype)], 0, name=name)[0]


def _window_sum(e, window, total_rows, backward):
    s, k = e, 1
    while k < window:
        s = s + pltpu.roll(s, (total_rows - k) if backward else k, 0)
        k *= 2
    return s


def pool_fwd(h, w, scale, *, name):
    S, D = h.shape
    G = len(POOL_WINDOWS)
    GC = D // G
    tile = _tile(S, ROW_TILE)
    hb = tile // HALO

    def body(hc_ref, hp_ref, w_ref, sc_ref, o_ref):
        i = pl.program_id(0)
        prev = jnp.where(i > 0, hp_ref[...], 0.0)
        ext = jnp.concatenate([prev, hc_ref[...]], axis=0)
        t = i * tile + lax.broadcasted_iota(jnp.int32, (tile, 1), 0)
        outs = []
        for g, window in enumerate(POOL_WINDOWS):
            e = ext[:, g * GC:(g + 1) * GC]
            s = _window_sum(e, window, HALO + tile, False)[HALO:, :]
            cnt = jnp.minimum(t + 1, window).astype(F32)
            y = s / cnt - e[HALO:, :]
            outs.append(jnp.dot(y.astype(BF16), w_ref[g], preferred_element_type=F32))
        o_ref[...] = jnp.concatenate(outs, axis=1) * sc_ref[...]

    return pl.pallas_call(
        body, name=name, grid=(S // tile,),
        in_specs=[pl.BlockSpec((tile, D), lambda i: (i, 0)),
                  pl.BlockSpec((HALO, D), lambda i: (jnp.maximum(i * hb - 1, 0), 0)),
                  pl.BlockSpec((G, GC, GC), lambda i: (0, 0, 0)),
                  pl.BlockSpec((1, D), lambda i: (0, 0))],
        out_specs=pl.BlockSpec((tile, D), lambda i: (i, 0)),
        out_shape=jax.ShapeDtypeStruct((S, D), F32),
        compiler_params=pltpu.CompilerParams(dimension_semantics=("parallel",)),
    )(h, h, w, scale)


def pool_bwd(h, dm, w, scale, *, name):
    S, D = h.shape
    G = len(POOL_WINDOWS)
    GC = D // G
    tile = _tile(S, ROW_TILE)
    hb = tile // HALO
    n_tiles = S // tile
    last_halo = S // HALO - 1

    def body(hc_ref, hp_ref, dmc_ref, dmn_ref, w_ref, sc_ref, dh_ref, dw_ref, dsc_ref):
        i = pl.program_id(0)

        @pl.when(i == 0)
        def _():
            dw_ref[...] = jnp.zeros_like(dw_ref)
            dsc_ref[...] = jnp.zeros_like(dsc_ref)

        prev = jnp.where(i > 0, hp_ref[...], 0.0)
        ext = jnp.concatenate([prev, hc_ref[...]], axis=0)
        nxt = jnp.where(i < n_tiles - 1, dmn_ref[...], 0.0)
        dmc = dmc_ref[...]
        dm_ext = jnp.concatenate([dmc, nxt], axis=0)
        t = i * tile + lax.broadcasted_iota(jnp.int32, (tile, 1), 0)
        t_ext = i * tile + lax.broadcasted_iota(jnp.int32, (tile + HALO, 1), 0)
        dhs, dscs = [], []
        for g, window in enumerate(POOL_WINDOWS):
            cols = slice(g * GC, (g + 1) * GC)
            e = ext[:, cols]
            s = _window_sum(e, window, HALO + tile, False)[HALO:, :]
            y = (s / jnp.minimum(t + 1, window).astype(F32) - e[HALO:, :]).astype(BF16)
            wg = w_ref[g]
            ypre = jnp.dot(y, wg, preferred_element_type=F32)
            dscs.append(jnp.sum(dmc[:, cols] * ypre, axis=0, keepdims=True))
            dyp = (dm_ext[:, cols] * sc_ref[:, cols]).astype(BF16)
            dw_ref[g] += lax.dot_general(y, dyp[:tile, :], (((0,), (0,)), ((), ())), preferred_element_type=F32)
            dy = lax.dot_general(dyp, wg, (((1,), (1,)), ((), ())), preferred_element_type=F32)
            r = dy / jnp.minimum(t_ext + 1, window).astype(F32)
            sr = _window_sum(r, window, tile + HALO, True)
            dhs.append(sr[:tile, :] - dy[:tile, :])
        dh_ref[...] = jnp.concatenate(dhs, axis=1)
        dsc_ref[...] += jnp.concatenate(dscs, axis=1)

    return pl.pallas_call(
        body, name=name, grid=(n_tiles,),
        in_specs=[pl.BlockSpec((tile, D), lambda i: (i, 0)),
                  pl.BlockSpec((HALO, D), lambda i: (jnp.maximum(i * hb - 1, 0), 0)),
                  pl.BlockSpec((tile, D), lambda i: (i, 0)),
                  pl.BlockSpec((HALO, D), lambda i: (jnp.minimum((i + 1) * hb, last_halo), 0)),
                  pl.BlockSpec((G, GC, GC), lambda i: (0, 0, 0)),
                  pl.BlockSpec((1, D), lambda i: (0, 0))],
        out_specs=[pl.BlockSpec((tile, D), lambda i: (i, 0)),
                   pl.BlockSpec((G, GC, GC), lambda i: (0, 0, 0)),
                   pl.BlockSpec((1, D), lambda i: (0, 0))],
        out_shape=[jax.ShapeDtypeStruct((S, D), F32), jax.ShapeDtypeStruct((G, GC, GC), F32),
                   jax.ShapeDtypeStruct((1, D), F32)],
        compiler_params=pltpu.CompilerParams(dimension_semantics=("arbitrary",)),
    )(h, h, dm, dm, w, scale)


ATT_LANES = 256
N_PAIR = ATT_LANES // HEAD_DIM


def _softplus_parts(z, mask):
    sp = jnp.maximum(z, 0.0) + jnp.log(1.0 + jnp.exp(-jnp.abs(z)))
    logb = z - sp
    if mask is not None:
        sp = jnp.where(mask, sp, 0.0)
    hi = sp.astype(BF16)
    lo = (sp - hi.astype(F32)).astype(BF16)
    return logb, jnp.concatenate([hi, lo], axis=1), jnp.sum(sp, axis=1, keepdims=True)


def _weights(logb, later, c, mask):
    a = jnp.exp(logb - (later + c))
    return a if mask is None else jnp.where(mask, a, 0.0)


def _tile_weights(qs, ks, cs, u2_later, masks):
    zs = [lax.dot_general(q, k, (((1,), (1,)), ((), ())), preferred_element_type=F32) for q, k in zip(qs, ks)]
    parts = [_softplus_parts(z, m) for z, m in zip(zs, masks)]
    laters = [jnp.dot(hilo, u2_later, preferred_element_type=F32) for _, hilo, _ in parts]
    aa = [_weights(p[0], later, c, m) for p, later, c, m in zip(parts, laters, cs, masks)]
    return [p[0] for p in parts], aa, [p[2] for p in parts]


def _tri(T, later):
    rows = lax.broadcasted_iota(jnp.int32, (T, T), 0)
    cols = lax.broadcasted_iota(jnp.int32, (T, T), 1)
    return jnp.where((rows > cols) if later else (rows < cols), 1.0, 0.0).astype(BF16)


def _head_masks(x2, axis=None):
    lane = lax.broadcasted_iota(jnp.int32, (1, ATT_LANES), 1)
    parts = [jnp.where((lane // HEAD_DIM) == hh, x2, jnp.zeros_like(x2)) for hh in range(N_PAIR)]
    return parts if axis is None else jnp.concatenate(parts, axis=axis)


def _cat_bf16(parts, axis):
    return jnp.concatenate([p.astype(BF16) for p in parts], axis=axis)


def attn_fwd(q, kv, *, name):
    S, D = q.shape
    P = D // ATT_LANES
    T = _tile(S, ATT_TILE)

    def body(q_ref, k_ref, v_ref, o_ref):
        i = pl.program_id(1)
        u_later = _tri(T, True)
        u2_later = jnp.concatenate([u_later, u_later], axis=0)
        qhs = _head_masks(q_ref[...])
        diag = lax.broadcasted_iota(jnp.int32, (T, T), 1) < lax.broadcasted_iota(jnp.int32, (T, T), 0)

        def tile(j, carry, mask):
            cs, acc = carry
            off = pl.multiple_of(j * T, T)
            kj = k_ref[pl.ds(off, T), :]
            vcat = _head_masks(v_ref[pl.ds(off, T), :], axis=0)
            _, aa, rss = _tile_weights(qhs, [kj] * N_PAIR, cs, u2_later, [mask] * N_PAIR)
            acc = acc + jnp.dot(_cat_bf16(aa, 1), vcat, preferred_element_type=F32)
            return tuple(c + rs for c, rs in zip(cs, rss)), acc

        carry = tile(i, ((jnp.zeros((T, 1), F32),) * N_PAIR, jnp.zeros((T, ATT_LANES), F32)), diag)
        _, acc = lax.fori_loop(1, i + 1, lambda n, cr: tile(i - n, cr, None), carry)
        o_ref[...] = acc.astype(o_ref.dtype)

    return pl.pallas_call(
        body, name=name, grid=(P, S // T),
        in_specs=[pl.BlockSpec((T, ATT_LANES), lambda p, i: (i, p)),
                  pl.BlockSpec((S, ATT_LANES), lambda p, i: (0, p)),
                  pl.BlockSpec((S, ATT_LANES), lambda p, i: (0, P + p))],
        out_specs=pl.BlockSpec((T, ATT_LANES), lambda p, i: (i, p)),
        out_shape=jax.ShapeDtypeStruct((S, D), BF16),
        compiler_params=pltpu.CompilerParams(dimension_semantics=("parallel", "arbitrary")),
    )(q, kv, kv)


def attn_bwd(q, kv, do, *, name):
    S, D = q.shape
    P = D // ATT_LANES
    T = _tile(S, ATT_TILE)
    nb = S // T

    def body(q_ref, k_ref, v_ref, do_ref, dq_ref, dk_ref, dv_ref, g_scr, s_scr):
        i = pl.program_id(1)

        @pl.when(i == 0)
        def _():
            dk_ref[...] = jnp.zeros_like(dk_ref)
            dv_ref[...] = jnp.zeros_like(dv_ref)

        u_later = _tri(T, True)
        u2_later = jnp.concatenate([u_later, u_later], axis=0)
        u_earlier = _tri(T, False)
        qhs = _head_masks(q_ref[...])
        dohs = _head_masks(do_ref[...])
        qcat = jnp.concatenate(qhs, axis=0)
        docat = jnp.concatenate(dohs, axis=0)
        diag = lax.broadcasted_iota(jnp.int32, (T, T), 1) < lax.broadcasted_iota(jnp.int32, (T, T), 0)
        tdot = (((0,), (0,)), ((), ()))

        def tile1(j, cs, mask):
            off = pl.multiple_of(j * T, T)
            kj = k_ref[pl.ds(off, T), :]
            vj = v_ref[pl.ds(off, T), :]
            das = [lax.dot_general(d, vj, (((1,), (1,)), ((), ())), preferred_element_type=F32) for d in dohs]
            logbs, aa, rss = _tile_weights(qhs, [kj] * N_PAIR, cs, u2_later, [mask] * N_PAIR)
            for hh in range(N_PAIR):
                g_scr[hh, j] = (das[hh] * aa[hh]).astype(BF16)
                sg = jnp.exp(logbs[hh])
                if mask is not None:
                    sg = jnp.where(mask, sg, 0.0)
                s_scr[hh, j] = sg.astype(BF16)
            dv_ref[pl.ds(off, T), :] += lax.dot_general(_cat_bf16(aa, 0), docat, tdot, preferred_element_type=F32)
            return tuple(c + rs for c, rs in zip(cs, rss))

        cs = tile1(i, (jnp.zeros((T, 1), F32),) * N_PAIR, diag)
        lax.fori_loop(1, i + 1, lambda n, c: tile1(i - n, c, None), cs)

        def tile2(j, carry):
            off = pl.multiple_of(j * T, T)
            kj = k_ref[pl.ds(off, T), :]
            cs, acc = carry
            gs = [g_scr[hh, j] for hh in range(N_PAIR)]
            cums = [jnp.dot(g, u_earlier, preferred_element_type=F32) for g in gs]
            kcat = _head_masks(kj, axis=0)
            dzs, new = [], []
            for hh in range(N_PAIR):
                gf = gs[hh].astype(F32)
                sg = s_scr[hh, j].astype(F32)
                dzs.append((gf - sg * (gf + (cums[hh] + cs[hh]))).astype(BF16))
                new.append(cs[hh] + jnp.sum(gf, axis=1, keepdims=True))
            acc = acc + jnp.dot(jnp.concatenate(dzs, axis=1), kcat, preferred_element_type=F32)
            dk_ref[pl.ds(off, T), :] += lax.dot_general(jnp.concatenate(dzs, axis=0), qcat, tdot,
                                                       preferred_element_type=F32)
            return tuple(new), acc

        _, dq = lax.fori_loop(0, i + 1, tile2,
                              ((jnp.zeros((T, 1), F32),) * N_PAIR, jnp.zeros((T, ATT_LANES), F32)))
        dq_ref[...] = (dq * (HEAD_DIM ** -0.5)).astype(dq_ref.dtype)

    return pl.pallas_call(
        body, name=name, grid=(P, nb),
        in_specs=[pl.BlockSpec((T, ATT_LANES), lambda p, i: (i, p)),
                  pl.BlockSpec((S, ATT_LANES), lambda p, i: (0, p)),
                  pl.BlockSpec((S, ATT_LANES), lambda p, i: (0, P + p)),
                  pl.BlockSpec((T, ATT_LANES), lambda p, i: (i, p))],
        out_specs=[pl.BlockSpec((T, ATT_LANES), lambda p, i: (i, p)),
                   pl.BlockSpec((S, ATT_LANES), lambda p, i: (0, p)),
                   pl.BlockSpec((S, ATT_LANES), lambda p, i: (0, p))],
        out_shape=[jax.ShapeDtypeStruct((S, D), BF16), jax.ShapeDtypeStruct((S, D), F32),
                   jax.ShapeDtypeStruct((S, D), F32)],
        scratch_shapes=[pltpu.VMEM((N_PAIR, nb, T, T), BF16), pltpu.VMEM((N_PAIR, nb, T, T), BF16)],
        compiler_params=pltpu.CompilerParams(dimension_semantics=("parallel", "arbitrary")),
    )(q, kv, kv, do)


def _window(ref, axis, dev, n):
    return ref.at[(slice(None),) * axis + (pl.ds(dev * n, n),)]


def all_gather(ops, *, name):
    n_ops = len(ops)
    out_shape = []
    for a, ax in ops:
        shp = list(a.shape)
        shp[ax] *= N_DEV
        out_shape.append(jax.ShapeDtypeStruct(tuple(shp), a.dtype))

    def body(*refs):
        ins, outs = refs[:n_ops], refs[n_ops:2 * n_ops]
        send_sems, recv_sems, local_sems = refs[2 * n_ops:]
        x, y, c = (lax.axis_index(n) for n in MESH_AXES)
        me, sibling = (x, y, c), (x, y, 1 - c)
        chips = [(1 - x, y), (x, 1 - y), (1 - x, 1 - y)]

        def rows(o, dev):
            px, py, pc = dev
            ax = ops[o][1]
            return _window(outs[o], ax, 4 * px + 2 * py + pc, ops[o][0].shape[ax])

        def copy(o, k, block, to, src=None):
            return pltpu.make_async_remote_copy(
                src_ref=rows(o, block) if src is None else src, dst_ref=rows(o, block),
                send_sem=send_sems.at[o, k], recv_sem=recv_sems.at[o, k],
                device_id=to, device_id_type=pl.DeviceIdType.MESH)

        mine, first, passed = [], [], []
        for o in range(n_ops):
            cp = pltpu.make_async_copy(ins[o], rows(o, me), local_sems.at[o])
            cp.start()
            mine.append(cp)
            first.append(copy(o, 0, me, sibling, src=ins[o]))
            first += [copy(o, 1 + j, me, (*chip, c), src=ins[o]) for j, chip in enumerate(chips)]
        for cp in first:
            cp.start()
        for j, chip in enumerate(chips):
            for o in range(n_ops):
                copy(o, 1 + j, (*chip, c), me).wait_recv()
                cp = copy(o, 4 + j, (*chip, c), sibling)
                cp.start()
                passed.append(cp)
        for o in range(n_ops):
            copy(o, 0, sibling, me).wait_recv()
            for j, chip in enumerate(chips):
                copy(o, 4 + j, (*chip, 1 - c), me).wait_recv()
        for cp in first + passed:
            cp.wait_send()
        for cp in mine:
            cp.wait()

    any_spec = pl.BlockSpec(memory_space=pl.ANY)
    return pl.pallas_call(
        body, name=name, in_specs=[any_spec] * n_ops, out_specs=[any_spec] * n_ops, out_shape=out_shape,
        scratch_shapes=[pltpu.SemaphoreType.DMA((n_ops, 7)), pltpu.SemaphoreType.DMA((n_ops, 7)),
                        pltpu.SemaphoreType.DMA((n_ops,))],
        compiler_params=pltpu.CompilerParams(has_side_effects=True),
    )(*[a for a, _ in ops])


def scatter_blocks(ops, *, name):
    n_ops = len(ops)
    blocks = []
    for a, ax in ops:
        shp = list(a.shape)
        shp[ax] //= N_DEV
        blocks.append(tuple(shp))
    out_shape = [jax.ShapeDtypeStruct((N_DEV,) + b, a.dtype) for b, (a, _) in zip(blocks, ops)]

    def body(*refs):
        ins, outs = refs[:n_ops], refs[n_ops:2 * n_ops]
        send_sems, recv_sems, local_sems = refs[2 * n_ops:]
        x, y, c = (lax.axis_index(n) for n in MESH_AXES)
        me = 4 * x + 2 * y + c

        def block(o, dev):
            ax = ops[o][1]
            return _window(ins[o], ax, dev, blocks[o][ax])

        def peer_of(r):
            px = 1 - x if r & 4 else x
            py = 1 - y if r & 2 else y
            pc = 1 - c if r & 1 else c
            return (px, py, pc), 4 * px + 2 * py + pc

        sends, local = [], []
        for o in range(n_ops):
            cp = pltpu.make_async_copy(block(o, me), outs[o].at[me], local_sems.at[o])
            cp.start()
            local.append(cp)
            for r in range(1, N_DEV):
                peer, peer_id = peer_of(r)
                cp = pltpu.make_async_remote_copy(
                    src_ref=block(o, peer_id), dst_ref=outs[o].at[me],
                    send_sem=send_sems.at[o, r - 1], recv_sem=recv_sems.at[o, r - 1],
                    device_id=peer, device_id_type=pl.DeviceIdType.MESH)
                cp.start()
                sends.append(cp)
        for o in range(n_ops):
            for r in range(1, N_DEV):
                peer, peer_id = peer_of(r)
                pltpu.make_async_remote_copy(
                    src_ref=block(o, me), dst_ref=outs[o].at[peer_id],
                    send_sem=send_sems.at[o, r - 1], recv_sem=recv_sems.at[o, r - 1],
                    device_id=peer, device_id_type=pl.DeviceIdType.MESH).wait_recv()
        for cp in sends:
            cp.wait_send()
        for cp in local:
            cp.wait()

    any_spec = pl.BlockSpec(memory_space=pl.ANY)
    return pl.pallas_call(
        body, name=name, in_specs=[any_spec] * n_ops, out_specs=[any_spec] * n_ops, out_shape=out_shape,
        scratch_shapes=[pltpu.SemaphoreType.DMA((n_ops, 7)), pltpu.SemaphoreType.DMA((n_ops, 7)),
                        pltpu.SemaphoreType.DMA((n_ops,))],
        compiler_params=pltpu.CompilerParams(has_side_effects=True),
    )(*[a for a, _ in ops])


def _adamw(w, g, m, v):
    m = ADAM_B1 * m + (1.0 - ADAM_B1) * g
    v = ADAM_B2 * v + (1.0 - ADAM_B2) * (g * g)
    m_hat = m / (1.0 - ADAM_B1 ** ADAM_STEP)
    v_hat = v / (1.0 - ADAM_B2 ** ADAM_STEP)
    delta = -ADAM_LR * (m_hat / (jnp.sqrt(v_hat) + ADAM_EPS) + ADAM_WD * w)
    return delta, m, v


def reduce_adamw(parts, w, m, v, *, name):
    shape = w.shape
    C = shape[-1]
    R = w.size // C
    tile = _tile(R, 256)

    def body(p_ref, w_ref, m_ref, v_ref, g_ref, d_ref, nm_ref, nv_ref):
        g = p_ref[0].astype(F32)
        for s in range(1, N_DEV):
            g = g + p_ref[s].astype(F32)
        d, nm, nv = _adamw(w_ref[...], g, m_ref[...], v_ref[...])
        g_ref[...] = g
        d_ref[...] = d
        nm_ref[...] = nm
        nv_ref[...] = nv

    row = pl.BlockSpec((tile, C), lambda i: (i, 0))
    outs = pl.pallas_call(
        body, name=name, grid=(R // tile,),
        in_specs=[pl.BlockSpec((N_DEV, tile, C), lambda i: (0, i, 0)), row, row, row],
        out_specs=[row] * 4, out_shape=[jax.ShapeDtypeStruct((R, C), F32)] * 4,
        compiler_params=pltpu.CompilerParams(dimension_semantics=("parallel",)),
    )(parts.reshape(N_DEV, R, C), w.reshape(R, C), m.reshape(R, C), v.reshape(R, C))
    return [o.reshape(shape) for o in outs]


def gains_adamw(parts, groups, *, name):
    n = len(groups)

    def body(*refs):
        p_ref = refs[0]
        ins, outs = refs[1:1 + 3 * n], refs[1 + 3 * n:]
        for k in range(n):
            w_ref, m_ref, v_ref = ins[3 * k:3 * k + 3]
            L = w_ref.shape[0]
            g = p_ref[pl.ds(8 * k, L), :]
            for s in range(1, N_DEV):
                g = g + p_ref[pl.ds(s * GAIN_ROWS + 8 * k, L), :]
            d, nm, nv = _adamw(w_ref[...], g, m_ref[...], v_ref[...])
            for r, val in zip(outs[4 * k:4 * k + 4], (g, d, nm, nv)):
                r[...] = val

    flat = [a for grp in groups for a in grp]
    out_shape = [jax.ShapeDtypeStruct(grp[0].shape, F32) for grp in groups for _ in range(4)]
    outs = pl.pallas_call(body, name=name, out_shape=out_shape)(parts, *flat)
    return [outs[4 * k:4 * k + 4] for k in range(n)]


def kernel(x, pool_w, pool_scale, w_q, w_kv, kv_norm_g, w_o, w_up, w_down, mix_pre_g, mix_post_g, mlp_pre_g, mlp_post_g, loss_target, m_pool_w, m_pool_scale, m_w_q, m_w_kv, m_kv_norm_g, m_w_o, m_w_up, m_w_down, m_mix_pre_g, m_mix_post_g, m_mlp_pre_g, m_mlp_post_g, v_pool_w, v_pool_scale, v_w_q, v_w_kv, v_kv_norm_g, v_w_o, v_w_up, v_w_down, v_mix_pre_g, v_mix_post_g, v_mlp_pre_g, v_mlp_post_g):
    _, S, D = x.shape
    x0 = x.reshape(S, D)
    target = loss_target.reshape(S, D)
    depth = w_up.shape[0]
    n_pool = pool_w.shape[0]
    F = w_up.shape[2] * N_DEV
    G = pool_w.shape[1]
    GC = D // G

    def vec(a, l):
        return a[l].reshape(1, D)

    gathered = all_gather(
        [(w_q.astype(BF16), 1), (w_kv.astype(BF16), 1), (w_o.astype(BF16), 1), (w_up.astype(BF16), 2),
         (w_down.astype(BF16), 1), (pool_w.astype(BF16), 2), (pool_scale, 1)], name="gather_weights")
    wq, wkv, wo, wup, wdn, pw, psc = gathered

    saved = []
    xs = x0
    h1 = norm_only(xs, vec(mix_pre_g, 0), F32, name="norm_in")
    kv = hk = None
    dy = loss_rows = None
    for l in range(depth):
        is_pool = l < n_pool
        st = {"x": xs, "h1": h1}
        if is_pool:
            m = pool_fwd(h1, pw[l], psc[l].reshape(1, D), name=f"pool_fwd{l}")
        else:
            j = l - n_pool
            q = matmul(h1, wq, b_layer=j, out_dtype=BF16, out_scale=HEAD_DIM ** -0.5, name=f"q_proj{j}")
            o = attn_fwd(q, kv, name=f"attn_fwd{j}")
            m = matmul(o, wo, b_layer=j, name=f"o_proj{j}")
            st.update(q=q, o=o)
        x_mid, h2 = residual_norms(xs, m, vec(mix_post_g, l), [vec(mlp_pre_g, l)], [BF16], name=f"mix_out{l}")
        u = matmul(h2, wup, b_layer=l, name=f"mlp_up{l}")
        d = matmul(u, wdn, b_layer=l, a_fn=_relu2, tk=512, name=f"mlp_down{l}")
        st.update(m=m, x_mid=x_mid, h2=h2, u=u, d=d)
        saved.append(st)
        if l == depth - 1:
            dy, loss_rows = residual_loss(x_mid, d, vec(mlp_post_g, l), target, name="loss")
        elif l == n_pool - 1:
            xs, h1, hk = residual_norms(x_mid, d, vec(mlp_post_g, l), [vec(mix_pre_g, l + 1), kv_norm_g.reshape(1, D)],
                                        [BF16, BF16], name=f"mlp_out{l}")
            kv = matmul(hk, wkv, out_dtype=BF16, name="kv_proj")
        else:
            nxt_dt = F32 if l + 1 < n_pool else BF16
            xs, h1 = residual_norms(x_mid, d, vec(mlp_post_g, l), [vec(mix_pre_g, l + 1)], [nxt_dt], name=f"mlp_out{l}")
    loss = lax.psum(jnp.sum(loss_rows), MESH_AXES)

    g_wq = lax.empty((depth - n_pool, D, D), BF16)
    g_wo = lax.empty((depth - n_pool, D, D), BF16)
    g_wup = lax.empty((depth, D, F), BF16)
    g_wdn = lax.empty((depth, F, D), BF16)
    g_pw, g_psc = [None] * n_pool, [None] * n_pool
    gains = {k: [None] * depth for k in ("mix_pre", "mix_post", "mlp_pre", "mlp_post")}
    dkvs = []
    g_wkv = g_kvn = None
    for l in reversed(range(depth)):
        st = saved[l]
        is_pool = l < n_pool
        dd, gains["mlp_post"][l] = post_norm_bwd(st["d"], vec(mlp_post_g, l), dy, BF16, name=f"mlp_out_bwd{l}")
        du = matmul(dd, wdn, b_layer=l, tb=True, out_dtype=BF16, epi=_relu2_grad, epi_in=st["u"], name=f"mlp_du{l}")
        g_wdn = matmul(st["u"], dd, ta=True, a_fn=_relu2, tk=512, out_dtype=BF16, out_into=(g_wdn, l),
                       name=f"mlp_dwdn{l}")
        g_wup = matmul(st["h2"], du, ta=True, out_dtype=BF16, out_into=(g_wup, l), name=f"mlp_dwup{l}")
        dh2 = matmul(du, wup, b_layer=l, tb=True, name=f"mlp_dh{l}")
        dxm, dm, gains["mlp_pre"][l], gains["mix_post"][l] = mid_bwd(
            dy, st["x_mid"], dh2, st["m"], vec(mlp_pre_g, l), vec(mix_post_g, l), F32 if is_pool else BF16,
            name=f"mid_bwd{l}")
        if is_pool:
            dh1, g_pw[l], g_psc[l] = pool_bwd(st["h1"], dm, pw[l], psc[l].reshape(1, D), name=f"pool_bwd{l}")
        else:
            j = l - n_pool
            do = matmul(dm, wo, b_layer=j, tb=True, out_dtype=BF16, name=f"o_proj_dx{j}")
            g_wo = matmul(st["o"], dm, ta=True, out_dtype=BF16, out_into=(g_wo, j), name=f"o_proj_dw{j}")
            dq, dk, dv = attn_bwd(st["q"], kv, do, name=f"attn_bwd{j}")
            dkvs.append((dk, dv))
            g_wq = matmul(st["h1"], dq, ta=True, out_dtype=BF16, out_into=(g_wq, j), name=f"q_proj_dw{j}")
            dh1 = matmul(dq, wq, b_layer=j, tb=True, name=f"q_proj_dx{j}")
        if l == n_pool:
            dkv = sum_concat_cast(dkvs, BF16, name="dkv_pack")
            g_wkv = matmul(hk, dkv, ta=True, out_dtype=BF16, name="kv_proj_dw")
            dhk = matmul(dkv, wkv, tb=True, name="kv_proj_dx")
            dy, gains["mix_pre"][l], g_kvn = pre_norm_bwd(
                dxm, st["x"], [dh1, dhk], [vec(mix_pre_g, l), kv_norm_g.reshape(1, D)], name=f"mix_in_bwd{l}")
        else:
            dy, gains["mix_pre"][l] = pre_norm_bwd(dxm, st["x"], [dh1], [vec(mix_pre_g, l)], name=f"mix_in_bwd{l}")
    grad_x = dy.reshape(x.shape)

    g_pool_w = jnp.stack(g_pw)
    g_pool_scale = jnp.concatenate(g_psc, axis=0)
    zero_rows = jnp.zeros((8 - depth, D), F32)
    gain_rows = []
    for k in ("mix_pre", "mix_post", "mlp_pre", "mlp_post"):
        gain_rows += gains[k] + [zero_rows]
    gain_rows += [g_kvn, jnp.zeros((7, D), F32)]
    gain_pack = jnp.concatenate(gain_rows, axis=0)
    gain_parts = all_gather([(gain_pack, 0)], name="gather_gain_grads")[0]
    parts = scatter_blocks(
        [(g_wq, 1), (g_wkv, 1), (g_wo, 1), (g_wup, 2), (g_wdn, 1), (g_pool_w, 2), (g_pool_scale, 1)],
        name="exchange_grads")
    big = [(w_q, m_w_q, v_w_q), (w_kv, m_w_kv, v_w_kv), (w_o, m_w_o, v_w_o), (w_up, m_w_up, v_w_up),
           (w_down, m_w_down, v_w_down), (pool_w, m_pool_w, v_pool_w), (pool_scale, m_pool_scale, v_pool_scale)]
    names = ["w_q", "w_kv", "w_o", "w_up", "w_down", "pool_w", "pool_scale"]
    res = {n: reduce_adamw(p, *wmv, name=f"adamw_{n}") for n, p, wmv in zip(names, parts, big)}
    gain_groups = [(mix_pre_g, m_mix_pre_g, v_mix_pre_g), (mix_post_g, m_mix_post_g, v_mix_post_g),
                   (mlp_pre_g, m_mlp_pre_g, v_mlp_pre_g), (mlp_post_g, m_mlp_post_g, v_mlp_post_g),
                   (kv_norm_g.reshape(1, D), m_kv_norm_g.reshape(1, D), v_kv_norm_g.reshape(1, D))]
    gres = gains_adamw(gain_parts, gain_groups, name="adamw_gains")
    for n, r in zip(["mix_pre_g", "mix_post_g", "mlp_pre_g", "mlp_post_g"], gres[:4]):
        res[n] = r
    res["kv_norm_g"] = [a.reshape(D) for a in gres[4]]

    order = ["pool_w", "pool_scale", "w_q", "w_kv", "kv_norm_g", "w_o", "w_up", "w_down",
             "mix_pre_g", "mix_post_g", "mlp_pre_g", "mlp_post_g"]
    out = [loss, grad_x]
    for k in range(4):
        out += [res[n][k] for n in order]
    return tuple(out)
```

```python
import functools

import jax
import jax.numpy as jnp
from jax import lax
from jax.experimental import pallas as pl
from jax.experimental.pallas import tpu as pltpu

F32 = jnp.float32
BF16 = jnp.bfloat16

EPS = 1e-6
HEAD_DIM = 64
LANES = 128
POOL_WINDOWS = (2, 4, 8, 16)
HALO = 16
N_DEV = 8
MESH_AXES = ("x", "y", "c")

ADAM_LR = 0.001
ADAM_B1 = 0.9
ADAM_B2 = 0.999
ADAM_EPS = 1e-08
ADAM_WD = 0.01
ADAM_STEP = 10

ROW_TILE = 256
ATT_TILE = 256
ROW_CHUNK = 32
GAIN_ROWS = 40


def _tile(n, want):
    return want if n % want == 0 else n


def matmul(a, b, *, name, ta=False, tb=False, a_layer=None, b_layer=None, out_dtype=F32,
           a_fn=None, epi=None, epi_in=None, out_scale=None, tm=1024, tn=1024, tk=1024):
    a2 = a.shape[1:] if a_layer is not None else a.shape
    b2 = b.shape[1:] if b_layer is not None else b.shape
    (K, M) = a2 if ta else a2[::-1]
    if not ta:
        M, K = a2
    if tb:
        N, Kb = b2
    else:
        Kb, N = b2
    assert K == Kb, (a.shape, b.shape)
    tm, tn, tk = _tile(M, tm), _tile(N, tn), _tile(K, tk)
    nk = K // tk
    grid = (M // tm, N // tn, nk)

    def lead(layer, shape, imap):
        if layer is None:
            return pl.BlockSpec(shape, imap)
        return pl.BlockSpec((None,) + shape, lambda i, j, k: (layer,) + imap(i, j, k))

    a_spec = lead(a_layer, (tk, tm) if ta else (tm, tk), (lambda i, j, k: (k, i)) if ta else (lambda i, j, k: (i, k)))
    b_spec = lead(b_layer, (tn, tk) if tb else (tk, tn), (lambda i, j, k: (j, k)) if tb else (lambda i, j, k: (k, j)))
    in_specs = [a_spec, b_spec]
    operands = [a, b]
    if epi is not None:
        in_specs.append(pl.BlockSpec((tm, tn), lambda i, j, k: (i, j)))
        operands.append(epi_in)
    out_shape = jax.ShapeDtypeStruct((M, N), out_dtype)
    out_spec = pl.BlockSpec((tm, tn), lambda i, j, k: (i, j))
    dims = (((0 if ta else 1,), (1 if tb else 0,)), ((), ()))
    n_in = len(operands)

    def body(*refs):
        a_ref, b_ref = refs[0], refs[1]
        e_ref = refs[2] if epi is not None else None
        o_ref = refs[n_in]

        def product():
            av = a_ref[...]
            if a_fn is not None:
                av = a_fn(av)
            return lax.dot_general(av.astype(BF16), b_ref[...].astype(BF16), dims, preferred_element_type=F32)

        def finish(r):
            if epi is not None:
                r = epi(r, e_ref[...])
            if out_scale is not None:
                r = r * out_scale
            o_ref[...] = r.astype(out_dtype)

        if nk == 1:
            finish(product())
            return
        acc_ref = refs[n_in + 1]
        k = pl.program_id(2)

        @pl.when(k == 0)
        def _():
            acc_ref[...] = product()

        @pl.when(k > 0)
        def _():
            acc_ref[...] += product()

        @pl.when(k == nk - 1)
        def _():
            finish(acc_ref[...])

    return pl.pallas_call(
        body, name=name, grid=grid, in_specs=in_specs, out_specs=out_spec, out_shape=out_shape,
        scratch_shapes=[pltpu.VMEM((tm, tn), F32)] if nk > 1 else [],
        compiler_params=pltpu.CompilerParams(dimension_semantics=("parallel", "parallel", "arbitrary")),
    )(*operands)


def _relu2(u):
    r = jnp.maximum(u, 0.0)
    return r * r


def _relu2_grad(acc, u):
    return acc * (2.0 * jnp.maximum(u, 0.0))


def rowwise(fn, rows, vecs, out_rows, n_acc, *, name, tile=ROW_TILE):
    S = rows[0].shape[0]
    tile = _tile(S, tile)
    n_rows, n_vecs, n_out = len(rows), len(vecs), len(out_rows)
    acc_cols = [None] * n_acc

    def body(*refs):
        ins = [r[...] for r in refs[:n_rows + n_vecs]]
        outs = refs[n_rows + n_vecs:]
        ro, ac = fn(*ins)
        assert len(ro) == n_out and len(ac) == n_acc
        for r, o in zip(outs[:n_out], ro):
            r[...] = o.astype(r.dtype)
        i = pl.program_id(0)
        for r, a in zip(outs[n_out:], ac):
            @pl.when(i == 0)
            def _():
                r[...] = jnp.zeros_like(r)
            r[...] += a

    acc_shapes = jax.eval_shape(
        lambda *xs: fn(*xs)[1],
        *[jax.ShapeDtypeStruct((tile, r.shape[1]), r.dtype) for r in rows],
        *[jax.ShapeDtypeStruct(v.shape, v.dtype) for v in vecs])
    in_specs = [pl.BlockSpec((tile, r.shape[1]), lambda i: (i, 0)) for r in rows]
    in_specs += [pl.BlockSpec(v.shape, lambda i: (0, 0)) for v in vecs]
    out_specs = [pl.BlockSpec((tile, c), lambda i: (i, 0)) for c, _ in out_rows]
    out_specs += [pl.BlockSpec(a.shape, lambda i: (0, 0)) for a in acc_shapes]
    out_shape = [jax.ShapeDtypeStruct((S, c), dt) for c, dt in out_rows]
    out_shape += [jax.ShapeDtypeStruct(a.shape, F32) for a in acc_shapes]
    del acc_cols
    return pl.pallas_call(
        body, name=name, grid=(S // tile,), in_specs=in_specs, out_specs=out_specs, out_shape=out_shape,
        compiler_params=pltpu.CompilerParams(dimension_semantics=("arbitrary",)),
    )(*rows, *vecs)


def _rms(x, g):
    r = lax.rsqrt(jnp.mean(x * x, axis=-1, keepdims=True) + EPS)
    return x * r * g


def _rms_bwd(x, g, dy):
    r = lax.rsqrt(jnp.mean(x * x, axis=-1, keepdims=True) + EPS)
    xh = x * r
    dyg = dy * g
    dx = r * (dyg - xh * jnp.mean(dyg * xh, axis=-1, keepdims=True))
    dg = jnp.sum(dy * xh, axis=0, keepdims=True)
    return dx, dg


def norm_only(x, g, dtype, *, name):
    D = x.shape[1]
    return rowwise(lambda xv, gv: ([_rms(xv, gv)], []), [x], [g], [(D, dtype)], 0, name=name)[0]


def residual_norms(x, m, g_post, next_gs, next_dtypes, *, name):
    D = x.shape[1]

    def fn(xv, mv, gp, *gs):
        xn = xv + _rms(mv, gp)
        return [xn] + [_rms(xn, g) for g in gs], []

    return rowwise(fn, [x, m], [g_post] + list(next_gs), [(D, F32)] + [(D, dt) for dt in next_dtypes], 0, name=name)


def residual_loss(x, m, g_post, target, *, name):
    D = x.shape[1]

    def fn(xv, mv, tv, gp):
        e = xv + _rms(mv, gp) - tv
        return [e * (1.0 / D)], [jnp.sum(e * e, axis=0, keepdims=True) * (0.5 / D)]

    return rowwise(fn, [x, m, target], [g_post], [(D, F32)], 1, name=name)


def post_norm_bwd(d, g_post, dy, dtype, *, name):
    D = d.shape[1]

    def fn(dv, dyv, gp):
        dd, dg = _rms_bwd(dv, gp, dyv)
        return [dd], [dg]

    return rowwise(fn, [d, dy], [g_post], [(D, dtype)], 1, name=name)


def mid_bwd(dy, x_mid, dh2, m, g_mlp_pre, g_mix_post, dm_dtype, *, name):
    D = dy.shape[1]

    def fn(dyv, xm, dh, mv, gpre, gpost):
        dx, dg_pre = _rms_bwd(xm, gpre, dh)
        dxm = dyv + dx
        dm, dg_post = _rms_bwd(mv, gpost, dxm)
        return [dxm, dm], [dg_pre, dg_post]

    return rowwise(fn, [dy, x_mid, dh2, m], [g_mlp_pre, g_mix_post], [(D, F32), (D, dm_dtype)], 2, name=name)


def pre_norm_bwd(dxm, x, dhs, gs, *, name):
    D = x.shape[1]
    n = len(dhs)

    def fn(dxv, xv, *rest):
        dh, g = rest[:n], rest[n:]
        out, accs = dxv, []
        for k in range(n):
            dx, dg = _rms_bwd(xv, g[k], dh[k])
            out = out + dx
            accs.append(dg)
        return [out], accs

    return rowwise(fn, [dxm, x] + list(dhs), list(gs), [(D, F32)], n, name=name)


def sum_concat_cast(pairs, dtype, *, name):
    C = pairs[0][0].shape[1]
    n = len(pairs)

    def fn(*xs):
        return [jnp.concatenate([sum(xs[:n]), sum(xs[n:])], axis=1)], []

    return rowwise(fn, [a for a, _ in pairs] + [b for _, b in pairs], [], [(2 * C, dtype)], 0, name=name)[0]


def _window_sum(e, window, total_rows, backward):
    s, k = e, 1
    while k < window:
        s = s + pltpu.roll(s, (total_rows - k) if backward else k, 0)
        k *= 2
    return s


def pool_fwd(h, w, scale, *, name):
    S, D = h.shape
    G = len(POOL_WINDOWS)
    GC = D // G
    tile = _tile(S, ROW_TILE)
    hb = tile // HALO

    def body(hc_ref, hp_ref, w_ref, sc_ref, o_ref):
        i = pl.program_id(0)
        prev = jnp.where(i > 0, hp_ref[...], 0.0)
        ext = jnp.concatenate([prev, hc_ref[...]], axis=0)
        t = i * tile + lax.broadcasted_iota(jnp.int32, (tile, 1), 0)
        outs = []
        for g, window in enumerate(POOL_WINDOWS):
            e = ext[:, g * GC:(g + 1) * GC]
            s = _window_sum(e, window, HALO + tile, False)[HALO:, :]
            cnt = jnp.minimum(t + 1, window).astype(F32)
            y = s / cnt - e[HALO:, :]
            outs.append(jnp.dot(y.astype(BF16), w_ref[g], preferred_element_type=F32))
        o_ref[...] = jnp.concatenate(outs, axis=1) * sc_ref[...]

    return pl.pallas_call(
        body, name=name, grid=(S // tile,),
        in_specs=[pl.BlockSpec((tile, D), lambda i: (i, 0)),
                  pl.BlockSpec((HALO, D), lambda i: (jnp.maximum(i * hb - 1, 0), 0)),
                  pl.BlockSpec((G, GC, GC), lambda i: (0, 0, 0)),
                  pl.BlockSpec((1, D), lambda i: (0, 0))],
        out_specs=pl.BlockSpec((tile, D), lambda i: (i, 0)),
        out_shape=jax.ShapeDtypeStruct((S, D), F32),
        compiler_params=pltpu.CompilerParams(dimension_semantics=("parallel",)),
    )(h, h, w, scale)


def pool_bwd(h, dm, w, scale, *, name):
    S, D = h.shape
    G = len(POOL_WINDOWS)
    GC = D // G
    tile = _tile(S, ROW_TILE)
    hb = tile // HALO
    n_tiles = S // tile
    last_halo = S // HALO - 1

    def body(hc_ref, hp_ref, dmc_ref, dmn_ref, w_ref, sc_ref, dh_ref, dw_ref, dsc_ref):
        i = pl.program_id(0)

        @pl.when(i == 0)
        def _():
            dw_ref[...] = jnp.zeros_like(dw_ref)
            dsc_ref[...] = jnp.zeros_like(dsc_ref)

        prev = jnp.where(i > 0, hp_ref[...], 0.0)
        ext = jnp.concatenate([prev, hc_ref[...]], axis=0)
        nxt = jnp.where(i < n_tiles - 1, dmn_ref[...], 0.0)
        dmc = dmc_ref[...]
        dm_ext = jnp.concatenate([dmc, nxt], axis=0)
        t = i * tile + lax.broadcasted_iota(jnp.int32, (tile, 1), 0)
        t_ext = i * tile + lax.broadcasted_iota(jnp.int32, (tile + HALO, 1), 0)
        dhs, dscs = [], []
        for g, window in enumerate(POOL_WINDOWS):
            cols = slice(g * GC, (g + 1) * GC)
            e = ext[:, cols]
            s = _window_sum(e, window, HALO + tile, False)[HALO:, :]
            y = (s / jnp.minimum(t + 1, window).astype(F32) - e[HALO:, :]).astype(BF16)
            wg = w_ref[g]
            ypre = jnp.dot(y, wg, preferred_element_type=F32)
            dscs.append(jnp.sum(dmc[:, cols] * ypre, axis=0, keepdims=True))
            dyp = (dm_ext[:, cols] * sc_ref[:, cols]).astype(BF16)
            dw_ref[g] += lax.dot_general(y, dyp[:tile, :], (((0,), (0,)), ((), ())), preferred_element_type=F32)
            dy = lax.dot_general(dyp, wg, (((1,), (1,)), ((), ())), preferred_element_type=F32)
            r = dy / jnp.minimum(t_ext + 1, window).astype(F32)
            sr = _window_sum(r, window, tile + HALO, True)
            dhs.append(sr[:tile, :] - dy[:tile, :])
        dh_ref[...] = jnp.concatenate(dhs, axis=1)
        dsc_ref[...] += jnp.concatenate(dscs, axis=1)

    return pl.pallas_call(
        body, name=name, grid=(n_tiles,),
        in_specs=[pl.BlockSpec((tile, D), lambda i: (i, 0)),
                  pl.BlockSpec((HALO, D), lambda i: (jnp.maximum(i * hb - 1, 0), 0)),
                  pl.BlockSpec((tile, D), lambda i: (i, 0)),
                  pl.BlockSpec((HALO, D), lambda i: (jnp.minimum((i + 1) * hb, last_halo), 0)),
                  pl.BlockSpec((G, GC, GC), lambda i: (0, 0, 0)),
                  pl.BlockSpec((1, D), lambda i: (0, 0))],
        out_specs=[pl.BlockSpec((tile, D), lambda i: (i, 0)),
                   pl.BlockSpec((G, GC, GC), lambda i: (0, 0, 0)),
                   pl.BlockSpec((1, D), lambda i: (0, 0))],
        out_shape=[jax.ShapeDtypeStruct((S, D), F32), jax.ShapeDtypeStruct((G, GC, GC), F32),
                   jax.ShapeDtypeStruct((1, D), F32)],
        compiler_params=pltpu.CompilerParams(dimension_semantics=("arbitrary",)),
    )(h, h, dm, dm, w, scale)


ATT_LANES = 256
N_PAIR = ATT_LANES // HEAD_DIM


def _softplus_parts(z, mask):
    sp = jnp.maximum(z, 0.0) + jnp.log(1.0 + jnp.exp(-jnp.abs(z)))
    logb = z - sp
    if mask is not None:
        sp = jnp.where(mask, sp, 0.0)
    hi = sp.astype(BF16)
    lo = (sp - hi.astype(F32)).astype(BF16)
    return logb, jnp.concatenate([hi, lo], axis=1), jnp.sum(sp, axis=1, keepdims=True)


def _weights(logb, later, c, mask):
    a = jnp.exp(logb - (later + c))
    return a if mask is None else jnp.where(mask, a, 0.0)


def _tile_weights(qs, ks, cs, u2_later, masks):
    zs = [lax.dot_general(q, k, (((1,), (1,)), ((), ())), preferred_element_type=F32) for q, k in zip(qs, ks)]
    parts = [_softplus_parts(z, m) for z, m in zip(zs, masks)]
    laters = [jnp.dot(hilo, u2_later, preferred_element_type=F32) for _, hilo, _ in parts]
    aa = [_weights(p[0], later, c, m) for p, later, c, m in zip(parts, laters, cs, masks)]
    return [p[0] for p in parts], aa, [p[2] for p in parts]


def _tri(T, later):
    rows = lax.broadcasted_iota(jnp.int32, (T, T), 0)
    cols = lax.broadcasted_iota(jnp.int32, (T, T), 1)
    return jnp.where((rows > cols) if later else (rows < cols), 1.0, 0.0).astype(BF16)


def _head_masks(x2, axis=None):
    lane = lax.broadcasted_iota(jnp.int32, (1, ATT_LANES), 1)
    parts = [jnp.where((lane // HEAD_DIM) == hh, x2, jnp.zeros_like(x2)) for hh in range(N_PAIR)]
    return parts if axis is None else jnp.concatenate(parts, axis=axis)


def _cat_bf16(parts, axis):
    return jnp.concatenate([p.astype(BF16) for p in parts], axis=axis)


def _side_split(side, refs, n_in, n_out):
    if side is None:
        return refs, None
    n_src, n_buf = len(side.srcs), len(side.bufs)
    ins, rest = refs[:n_in], refs[n_in:]
    src_refs, rest = rest[:n_src], rest[n_src + n_buf:]
    outs, rest = rest[:n_out], rest[n_out:]
    buf_refs, rest = rest[:n_buf], rest[n_buf:]
    own_scratch, sems = rest[:len(rest) - 3], rest[len(rest) - 3:]
    return tuple(ins) + tuple(outs) + tuple(own_scratch), (src_refs, buf_refs, sems)


def _side_phase(side, side_refs, phase, when):
    if side is None:
        return

    @pl.when(when)
    def _():
        side.copies(phase, *side_refs)


def attn_fwd(q, kv, *, name, side=None):
    S, D = q.shape
    P = D // ATT_LANES
    T = _tile(S, ATT_TILE)
    nq = S // T

    def body(*refs):
        (q_ref, k_ref, v_ref, o_ref), side_refs = _side_split(side, refs, 3, 1)
        i = pl.program_id(1)
        p = pl.program_id(0)
        _side_phase(side, side_refs, 0, (p == 0) & (i == 0))
        u_later = _tri(T, True)
        u2_later = jnp.concatenate([u_later, u_later], axis=0)
        qhs = _head_masks(q_ref[...])
        diag = lax.broadcasted_iota(jnp.int32, (T, T), 1) < lax.broadcasted_iota(jnp.int32, (T, T), 0)

        def tile(j, carry, mask):
            cs, acc = carry
            off = pl.multiple_of(j * T, T)
            kj = k_ref[pl.ds(off, T), :]
            vcat = _head_masks(v_ref[pl.ds(off, T), :], axis=0)
            _, aa, rss = _tile_weights(qhs, [kj] * N_PAIR, cs, u2_later, [mask] * N_PAIR)
            acc = acc + jnp.dot(_cat_bf16(aa, 1), vcat, preferred_element_type=F32)
            return tuple(c + rs for c, rs in zip(cs, rss)), acc

        carry = tile(i, ((jnp.zeros((T, 1), F32),) * N_PAIR, jnp.zeros((T, ATT_LANES), F32)), diag)
        _, acc = lax.fori_loop(1, i + 1, lambda n, cr: tile(i - n, cr, None), carry)
        o_ref[...] = acc.astype(o_ref.dtype)
        _side_phase(side, side_refs, 1, (p == P - 1) & (i == nq - 1))

    sd = side
    outs = pl.pallas_call(
        body, name=name, grid=(P, nq),
        in_specs=[pl.BlockSpec((T, ATT_LANES), lambda p, i: (i, p)),
                  pl.BlockSpec((S, ATT_LANES), lambda p, i: (0, p)),
                  pl.BlockSpec((S, ATT_LANES), lambda p, i: (0, P + p))] + (sd.specs() if sd else []),
        out_specs=[pl.BlockSpec((T, ATT_LANES), lambda p, i: (i, p))] + (sd.out_specs() if sd else []),
        out_shape=[jax.ShapeDtypeStruct((S, D), BF16)] + (sd.out_shape() if sd else []),
        scratch_shapes=sd.scratch() if sd else [],
        input_output_aliases=sd.aliases(3, 1) if sd else {},
        compiler_params=pltpu.CompilerParams(dimension_semantics=("arbitrary", "arbitrary"),
                                             has_side_effects=sd is not None),
    )(q, kv, kv, *(sd.operands() if sd else []))
    return outs[0] if sd is None else (outs[0], sd.result(outs[1:]))


def attn_bwd(q, kv, do, *, name, side=None):
    S, D = q.shape
    P = D // ATT_LANES
    T = _tile(S, ATT_TILE)
    nb = S // T

    def body(*refs):
        (q_ref, k_ref, v_ref, do_ref, dq_ref, dk_ref, dv_ref, g_scr, s_scr), side_refs = _side_split(side, refs, 4, 3)
        i = pl.program_id(1)
        p = pl.program_id(0)
        _side_phase(side, side_refs, 0, (p == 0) & (i == 0))

        @pl.when(i == 0)
        def _():
            dk_ref[...] = jnp.zeros_like(dk_ref)
            dv_ref[...] = jnp.zeros_like(dv_ref)

        u_later = _tri(T, True)
        u2_later = jnp.concatenate([u_later, u_later], axis=0)
        u_earlier = _tri(T, False)
        qhs = _head_masks(q_ref[...])
        dohs = _head_masks(do_ref[...])
        qcat = jnp.concatenate(qhs, axis=0)
        docat = jnp.concatenate(dohs, axis=0)
        diag = lax.broadcasted_iota(jnp.int32, (T, T), 1) < lax.broadcasted_iota(jnp.int32, (T, T), 0)
        tdot = (((0,), (0,)), ((), ()))

        def tile1(j, cs, mask):
            off = pl.multiple_of(j * T, T)
            kj = k_ref[pl.ds(off, T), :]
            vj = v_ref[pl.ds(off, T), :]
            das = [lax.dot_general(d, vj, (((1,), (1,)), ((), ())), preferred_element_type=F32) for d in dohs]
            logbs, aa, rss = _tile_weights(qhs, [kj] * N_PAIR, cs, u2_later, [mask] * N_PAIR)
            for hh in range(N_PAIR):
                g_scr[hh, j] = (das[hh] * aa[hh]).astype(BF16)
                sg = jnp.exp(logbs[hh])
                if mask is not None:
                    sg = jnp.where(mask, sg, 0.0)
                s_scr[hh, j] = sg.astype(BF16)
            dv_ref[pl.ds(off, T), :] += lax.dot_general(_cat_bf16(aa, 0), docat, tdot, preferred_element_type=F32)
            return tuple(c + rs for c, rs in zip(cs, rss))

        cs = tile1(i, (jnp.zeros((T, 1), F32),) * N_PAIR, diag)
        lax.fori_loop(1, i + 1, lambda n, c: tile1(i - n, c, None), cs)

        def tile2(j, carry):
            off = pl.multiple_of(j * T, T)
            kj = k_ref[pl.ds(off, T), :]
            cs, acc = carry
            gs = [g_scr[hh, j] for hh in range(N_PAIR)]
            cums = [jnp.dot(g, u_earlier, preferred_element_type=F32) for g in gs]
            kcat = _head_masks(kj, axis=0)
            dzs, new = [], []
            for hh in range(N_PAIR):
                gf = gs[hh].astype(F32)
                sg = s_scr[hh, j].astype(F32)
                dzs.append((gf - sg * (gf + (cums[hh] + cs[hh]))).astype(BF16))
                new.append(cs[hh] + jnp.sum(gf, axis=1, keepdims=True))
            acc = acc + jnp.dot(jnp.concatenate(dzs, axis=1), kcat, preferred_element_type=F32)
            dk_ref[pl.ds(off, T), :] += lax.dot_general(jnp.concatenate(dzs, axis=0), qcat, tdot,
                                                       preferred_element_type=F32)
            return tuple(new), acc

        _, dq = lax.fori_loop(0, i + 1, tile2,
                              ((jnp.zeros((T, 1), F32),) * N_PAIR, jnp.zeros((T, ATT_LANES), F32)))
        dq_ref[...] = (dq * (HEAD_DIM ** -0.5)).astype(dq_ref.dtype)
        _side_phase(side, side_refs, 1, (p == P - 1) & (i == nb - 1))

    sd = side
    outs = pl.pallas_call(
        body, name=name, grid=(P, nb),
        in_specs=[pl.BlockSpec((T, ATT_LANES), lambda p, i: (i, p)),
                  pl.BlockSpec((S, ATT_LANES), lambda p, i: (0, p)),
                  pl.BlockSpec((S, ATT_LANES), lambda p, i: (0, P + p)),
                  pl.BlockSpec((T, ATT_LANES), lambda p, i: (i, p))] + (sd.specs() if sd else []),
        out_specs=[pl.BlockSpec((T, ATT_LANES), lambda p, i: (i, p)),
                   pl.BlockSpec((S, ATT_LANES), lambda p, i: (0, p)),
                   pl.BlockSpec((S, ATT_LANES), lambda p, i: (0, p))] + (sd.out_specs() if sd else []),
        out_shape=[jax.ShapeDtypeStruct((S, D), BF16), jax.ShapeDtypeStruct((S, D), F32),
                   jax.ShapeDtypeStruct((S, D), F32)] + (sd.out_shape() if sd else []),
        scratch_shapes=[pltpu.VMEM((N_PAIR, nb, T, T), BF16), pltpu.VMEM((N_PAIR, nb, T, T), BF16)]
        + (sd.scratch() if sd else []),
        input_output_aliases=sd.aliases(4, 3) if sd else {},
        compiler_params=pltpu.CompilerParams(dimension_semantics=("arbitrary", "arbitrary"),
                                             has_side_effects=sd is not None),
    )(q, kv, kv, do, *(sd.operands() if sd else []))
    return tuple(outs[:3]) if sd is None else (*outs[:3], sd.result(outs[3:]))


def _window(ref, axis, dev, n):
    return ref.at[(slice(None),) * axis + (pl.ds(dev * n, n),)]


def all_gather(ops, *, name):
    n_ops = len(ops)
    out_shape = []
    for a, ax in ops:
        shp = list(a.shape)
        shp[ax] *= N_DEV
        out_shape.append(jax.ShapeDtypeStruct(tuple(shp), a.dtype))

    def body(*refs):
        ins, outs = refs[:n_ops], refs[n_ops:2 * n_ops]
        send_sems, recv_sems, local_sems = refs[2 * n_ops:]
        x, y, c = (lax.axis_index(n) for n in MESH_AXES)
        me, sibling = (x, y, c), (x, y, 1 - c)
        chips = [(1 - x, y), (x, 1 - y), (1 - x, 1 - y)]

        def rows(o, dev):
            px, py, pc = dev
            ax = ops[o][1]
            return _window(outs[o], ax, 4 * px + 2 * py + pc, ops[o][0].shape[ax])

        def copy(o, k, block, to, src=None):
            return pltpu.make_async_remote_copy(
                src_ref=rows(o, block) if src is None else src, dst_ref=rows(o, block),
                send_sem=send_sems.at[o, k], recv_sem=recv_sems.at[o, k],
                device_id=to, device_id_type=pl.DeviceIdType.MESH)

        mine, first, passed = [], [], []
        for o in range(n_ops):
            cp = pltpu.make_async_copy(ins[o], rows(o, me), local_sems.at[o])
            cp.start()
            mine.append(cp)
            first.append(copy(o, 0, me, sibling, src=ins[o]))
            first += [copy(o, 1 + j, me, (*chip, c), src=ins[o]) for j, chip in enumerate(chips)]
        for cp in first:
            cp.start()
        for j, chip in enumerate(chips):
            for o in range(n_ops):
                copy(o, 1 + j, (*chip, c), me).wait_recv()
                cp = copy(o, 4 + j, (*chip, c), sibling)
                cp.start()
                passed.append(cp)
        for o in range(n_ops):
            copy(o, 0, sibling, me).wait_recv()
            for j, chip in enumerate(chips):
                copy(o, 4 + j, (*chip, 1 - c), me).wait_recv()
        for cp in first + passed:
            cp.wait_send()
        for cp in mine:
            cp.wait()

    any_spec = pl.BlockSpec(memory_space=pl.ANY)
    return pl.pallas_call(
        body, name=name, in_specs=[any_spec] * n_ops, out_specs=[any_spec] * n_ops, out_shape=out_shape,
        scratch_shapes=[pltpu.SemaphoreType.DMA((n_ops, 7)), pltpu.SemaphoreType.DMA((n_ops, 7)),
                        pltpu.SemaphoreType.DMA((n_ops,))],
        compiler_params=pltpu.CompilerParams(has_side_effects=True),
    )(*[a for a, _ in ops])


class Side:
    def __init__(self, kind, items, bufs):
        self.kind, self.items = kind, items
        self.names = list(bufs)
        self.bufs = [bufs[n] for n in self.names]
        self.srcs = [it[0] for it in items]

    def operands(self):
        return self.srcs + self.bufs

    def specs(self):
        return [pl.BlockSpec(memory_space=pl.ANY)] * (len(self.srcs) + len(self.bufs))

    def out_specs(self):
        return [pl.BlockSpec(memory_space=pl.ANY)] * len(self.bufs)

    def out_shape(self):
        return [jax.ShapeDtypeStruct(b.shape, b.dtype) for b in self.bufs]

    def aliases(self, first_in, first_out):
        return {first_in + len(self.srcs) + k: first_out + k for k in range(len(self.bufs))}

    def scratch(self):
        n = len(self.items)
        return [pltpu.SemaphoreType.DMA((n, N_DEV - 1)), pltpu.SemaphoreType.DMA((n, N_DEV - 1)),
                pltpu.SemaphoreType.DMA((n,))]

    def result(self, outs):
        return dict(zip(self.names, outs))

    def copies(self, phase, src_refs, buf_refs, sems):
        send_sems, recv_sems, local_sems = sems
        pos = tuple(lax.axis_index(n) for n in MESH_AXES)
        me = 4 * pos[0] + 2 * pos[1] + pos[2]
        for o, (_, ax, name, layer) in enumerate(self.items):
            src, buf = src_refs[o], buf_refs[self.names.index(name)]
            if self.kind == "gather":
                whole = buf if layer is None else buf.at[layer]
                n = src.shape[ax]
                sent = lambda dev, src=src: src
                lands = lambda dev, whole=whole, ax=ax, n=n: _window(whole, ax, dev, n)
            else:
                n = src.shape[ax] // N_DEV
                sent = lambda dev, src=src, ax=ax, n=n: _window(src, ax, dev, n)
                lands = lambda dev, buf=buf, layer=layer: buf.at[dev] if layer is None else buf.at[dev, layer]
            local = pltpu.make_async_copy(sent(me), lands(me), local_sems.at[o])
            if phase == 0:
                local.start()
            else:
                local.wait()
            for r in range(1, N_DEV):
                peer = tuple(1 - p if r & bit else p for p, bit in zip(pos, (4, 2, 1)))
                pid = 4 * peer[0] + 2 * peer[1] + peer[2]
                cp = pltpu.make_async_remote_copy(
                    src_ref=sent(pid) if phase == 0 else sent(me), dst_ref=lands(me) if phase == 0 else lands(pid),
                    send_sem=send_sems.at[o, r - 1], recv_sem=recv_sems.at[o, r - 1],
                    device_id=peer, device_id_type=pl.DeviceIdType.MESH)
                if phase == 0:
                    cp.start()
                else:
                    cp.wait_recv()
                    cp.wait_send()


def run_side(side, *, name):
    n_src, n_buf = len(side.srcs), len(side.bufs)

    def body(*refs):
        src_refs, buf_refs = refs[:n_src], refs[n_src + n_buf:n_src + 2 * n_buf]
        sems = refs[n_src + 2 * n_buf:]
        side.copies(0, src_refs, buf_refs, sems)
        side.copies(1, src_refs, buf_refs, sems)

    outs = pl.pallas_call(
        body, name=name, in_specs=side.specs(), out_specs=side.out_specs(), out_shape=side.out_shape(),
        scratch_shapes=side.scratch(), input_output_aliases=side.aliases(0, 0),
        compiler_params=pltpu.CompilerParams(has_side_effects=True),
    )(*side.operands())
    return side.result(outs)


def _adamw(w, g, m, v):
    m = ADAM_B1 * m + (1.0 - ADAM_B1) * g
    v = ADAM_B2 * v + (1.0 - ADAM_B2) * (g * g)
    m_hat = m / (1.0 - ADAM_B1 ** ADAM_STEP)
    v_hat = v / (1.0 - ADAM_B2 ** ADAM_STEP)
    delta = -ADAM_LR * (m_hat / (jnp.sqrt(v_hat) + ADAM_EPS) + ADAM_WD * w)
    return delta, m, v


def reduce_adamw(parts, w, m, v, *, name):
    shape = w.shape
    C = shape[-1]
    R = w.size // C
    tile = _tile(R, 256)

    def body(p_ref, w_ref, m_ref, v_ref, g_ref, d_ref, nm_ref, nv_ref):
        g = p_ref[0].astype(F32)
        for s in range(1, N_DEV):
            g = g + p_ref[s].astype(F32)
        d, nm, nv = _adamw(w_ref[...], g, m_ref[...], v_ref[...])
        g_ref[...] = g
        d_ref[...] = d
        nm_ref[...] = nm
        nv_ref[...] = nv

    row = pl.BlockSpec((tile, C), lambda i: (i, 0))
    outs = pl.pallas_call(
        body, name=name, grid=(R // tile,),
        in_specs=[pl.BlockSpec((N_DEV, tile, C), lambda i: (0, i, 0)), row, row, row],
        out_specs=[row] * 4, out_shape=[jax.ShapeDtypeStruct((R, C), F32)] * 4,
        compiler_params=pltpu.CompilerParams(dimension_semantics=("parallel",)),
    )(parts.reshape(N_DEV, R, C), w.reshape(R, C), m.reshape(R, C), v.reshape(R, C))
    return [o.reshape(shape) for o in outs]


def gains_adamw(parts, groups, *, name):
    n = len(groups)

    def body(*refs):
        p_ref = refs[0]
        ins, outs = refs[1:1 + 3 * n], refs[1 + 3 * n:]
        for k in range(n):
            w_ref, m_ref, v_ref = ins[3 * k:3 * k + 3]
            L = w_ref.shape[0]
            g = p_ref[pl.ds(8 * k, L), :]
            for s in range(1, N_DEV):
                g = g + p_ref[pl.ds(s * GAIN_ROWS + 8 * k, L), :]
            d, nm, nv = _adamw(w_ref[...], g, m_ref[...], v_ref[...])
            for r, val in zip(outs[4 * k:4 * k + 4], (g, d, nm, nv)):
                r[...] = val

    flat = [a for grp in groups for a in grp]
    out_shape = [jax.ShapeDtypeStruct(grp[0].shape, F32) for grp in groups for _ in range(4)]
    outs = pl.pallas_call(body, name=name, out_shape=out_shape)(parts, *flat)
    return [outs[4 * k:4 * k + 4] for k in range(n)]


def kernel(x, pool_w, pool_scale, w_q, w_kv, kv_norm_g, w_o, w_up, w_down, mix_pre_g, mix_post_g, mlp_pre_g, mlp_post_g, loss_target, m_pool_w, m_pool_scale, m_w_q, m_w_kv, m_kv_norm_g, m_w_o, m_w_up, m_w_down, m_mix_pre_g, m_mix_post_g, m_mlp_pre_g, m_mlp_post_g, v_pool_w, v_pool_scale, v_w_q, v_w_kv, v_kv_norm_g, v_w_o, v_w_up, v_w_down, v_mix_pre_g, v_mix_post_g, v_mlp_pre_g, v_mlp_post_g):
    _, S, D = x.shape
    x0 = x.reshape(S, D)
    target = loss_target.reshape(S, D)
    depth = w_up.shape[0]
    n_pool = pool_w.shape[0]
    F = w_up.shape[2] * N_DEV
    G = pool_w.shape[1]
    GC = D // G

    def vec(a, l):
        return a[l].reshape(1, D)

    n_att = depth - n_pool
    wq_s, wkv_s, wo_s = w_q.astype(BF16), w_kv.astype(BF16), w_o.astype(BF16)
    wup_s, wdn_s, pw_s = w_up.astype(BF16), w_down.astype(BF16), pool_w.astype(BF16)
    wq0, wkv, wup_a, wdn_a, pw, psc = all_gather(
        [(wq_s[0], 0), (wkv_s, 1), (wup_s[:n_pool], 2), (wdn_s[:n_pool], 1), (pw_s, 2), (pool_scale, 1)],
        name="gather_weights")
    late = Side("gather",
                [(wq_s[1:], 1, "wq", None), (wo_s, 1, "wo", None), (wup_s[n_pool:], 2, "wup", None),
                 (wdn_s[n_pool:], 1, "wdn", None)],
                {"wq": lax.empty((n_att - 1, D, D), BF16), "wo": lax.empty((n_att, D, D), BF16),
                 "wup": lax.empty((n_att, D, F), BF16), "wdn": lax.empty((n_att, F, D), BF16)})
    late_w = None

    def layer_of(early, key, l, n_early):
        return (early, l) if l < n_early else (late_w[key], l - n_early)

    saved = []
    xs = x0
    h1 = norm_only(xs, vec(mix_pre_g, 0), F32, name="norm_in")
    kv = hk = None
    dy = loss_rows = None
    for l in range(depth):
        is_pool = l < n_pool
        st = {"x": xs, "h1": h1}
        if is_pool:
            m = pool_fwd(h1, pw[l], psc[l].reshape(1, D), name=f"pool_fwd{l}")
        else:
            j = l - n_pool
            wq_j, wq_l = (wq0, None) if j == 0 else (late_w["wq"], j - 1)
            q = matmul(h1, wq_j, b_layer=wq_l, out_dtype=BF16, out_scale=HEAD_DIM ** -0.5, name=f"q_proj{j}")
            if j == 0:
                o, late_w = attn_fwd(q, kv, name=f"attn_fwd{j}", side=late)
            else:
                o = attn_fwd(q, kv, name=f"attn_fwd{j}")
            m = matmul(o, late_w["wo"], b_layer=j, name=f"o_proj{j}")
            st.update(q=q, o=o)
        x_mid, h2 = residual_norms(xs, m, vec(mix_post_g, l), [vec(mlp_pre_g, l)], [BF16], name=f"mix_out{l}")
        wup, lu = layer_of(wup_a, "wup", l, n_pool)
        wdn, ld = layer_of(wdn_a, "wdn", l, n_pool)
        u = matmul(h2, wup, b_layer=lu, name=f"mlp_up{l}")
        d = matmul(u, wdn, b_layer=ld, a_fn=_relu2, tk=512, name=f"mlp_down{l}")
        st.update(m=m, x_mid=x_mid, h2=h2, u=u, d=d)
        saved.append(st)
        if l == depth - 1:
            dy, loss_rows = residual_loss(x_mid, d, vec(mlp_post_g, l), target, name="loss")
        elif l == n_pool - 1:
            xs, h1, hk = residual_norms(x_mid, d, vec(mlp_post_g, l), [vec(mix_pre_g, l + 1), kv_norm_g.reshape(1, D)],
                                        [BF16, BF16], name=f"mlp_out{l}")
            kv = matmul(hk, wkv, out_dtype=BF16, name="kv_proj")
        else:
            nxt_dt = F32 if l + 1 < n_pool else BF16
            xs, h1 = residual_norms(x_mid, d, vec(mlp_post_g, l), [vec(mix_pre_g, l + 1)], [nxt_dt], name=f"mlp_out{l}")
    loss = lax.psum(jnp.sum(loss_rows), MESH_AXES)

    recv = {"w_q": (w_q, BF16), "w_kv": (w_kv, BF16), "w_o": (w_o, BF16), "w_up": (w_up, BF16),
            "w_down": (w_down, BF16), "pool_w": (pool_w, F32), "pool_scale": (pool_scale, F32)}
    recv = {n: lax.empty((N_DEV,) + w.shape, dt) for n, (w, dt) in recv.items()}
    pending = []

    def send_pending():
        return Side("scatter", list(pending), {n: recv[n] for n in dict.fromkeys(it[2] for it in pending)})

    g_pw, g_psc = [None] * n_pool, [None] * n_pool
    gains = {k: [None] * depth for k in ("mix_pre", "mix_post", "mlp_pre", "mlp_post")}
    dkvs = []
    g_kvn = None
    for l in reversed(range(depth)):
        st = saved[l]
        is_pool = l < n_pool
        wup, lu = layer_of(wup_a, "wup", l, n_pool)
        wdn, ld = layer_of(wdn_a, "wdn", l, n_pool)
        dd, gains["mlp_post"][l] = post_norm_bwd(st["d"], vec(mlp_post_g, l), dy, BF16, name=f"mlp_out_bwd{l}")
        du = matmul(dd, wdn, b_layer=ld, tb=True, out_dtype=BF16, epi=_relu2_grad, epi_in=st["u"], name=f"mlp_du{l}")
        g_wdn = matmul(st["u"], dd, ta=True, a_fn=_relu2, tk=512, out_dtype=BF16, name=f"mlp_dwdn{l}")
        g_wup = matmul(st["h2"], du, ta=True, out_dtype=BF16, name=f"mlp_dwup{l}")
        pending += [(g_wup, 1, "w_up", l), (g_wdn, 0, "w_down", l)]
        dh2 = matmul(du, wup, b_layer=lu, tb=True, name=f"mlp_dh{l}")
        dxm, dm, gains["mlp_pre"][l], gains["mix_post"][l] = mid_bwd(
            dy, st["x_mid"], dh2, st["m"], vec(mlp_pre_g, l), vec(mix_post_g, l), F32 if is_pool else BF16,
            name=f"mid_bwd{l}")
        if is_pool:
            dh1, g_pw[l], g_psc[l] = pool_bwd(st["h1"], dm, pw[l], psc[l].reshape(1, D), name=f"pool_bwd{l}")
        else:
            j = l - n_pool
            wq_j, wq_l = (wq0, None) if j == 0 else (late_w["wq"], j - 1)
            do = matmul(dm, late_w["wo"], b_layer=j, tb=True, out_dtype=BF16, name=f"o_proj_dx{j}")
            g_wo = matmul(st["o"], dm, ta=True, out_dtype=BF16, name=f"o_proj_dw{j}")
            pending.append((g_wo, 0, "w_o", j))
            dq, dk, dv, got = attn_bwd(st["q"], kv, do, name=f"attn_bwd{j}", side=send_pending())
            recv.update(got)
            pending.clear()
            dkvs.append((dk, dv))
            g_wq = matmul(st["h1"], dq, ta=True, out_dtype=BF16, name=f"q_proj_dw{j}")
            pending.append((g_wq, 0, "w_q", j))
            dh1 = matmul(dq, wq_j, b_layer=wq_l, tb=True, name=f"q_proj_dx{j}")
        if l == n_pool:
            dkv = sum_concat_cast(dkvs, BF16, name="dkv_pack")
            g_wkv = matmul(hk, dkv, ta=True, out_dtype=BF16, name="kv_proj_dw")
            pending.append((g_wkv, 1, "w_kv", None))
            dhk = matmul(dkv, wkv, tb=True, name="kv_proj_dx")
            dy, gains["mix_pre"][l], g_kvn = pre_norm_bwd(
                dxm, st["x"], [dh1, dhk], [vec(mix_pre_g, l), kv_norm_g.reshape(1, D)], name=f"mix_in_bwd{l}")
        else:
            dy, gains["mix_pre"][l] = pre_norm_bwd(dxm, st["x"], [dh1], [vec(mix_pre_g, l)], name=f"mix_in_bwd{l}")
    grad_x = dy.reshape(x.shape)

    g_pool_w = jnp.stack(g_pw)
    g_pool_scale = jnp.concatenate(g_psc, axis=0)
    zero_rows = jnp.zeros((8 - depth, D), F32)
    gain_rows = []
    for k in ("mix_pre", "mix_post", "mlp_pre", "mlp_post"):
        gain_rows += gains[k] + [zero_rows]
    gain_rows += [g_kvn, jnp.zeros((7, D), F32)]
    gain_pack = jnp.concatenate(gain_rows, axis=0)
    gain_parts = all_gather([(gain_pack, 0)], name="gather_gain_grads")[0]
    pending += [(g_pool_w, 2, "pool_w", None), (g_pool_scale, 1, "pool_scale", None)]
    recv.update(run_side(send_pending(), name="exchange_grads"))
    big = {"w_q": (w_q, m_w_q, v_w_q), "w_kv": (w_kv, m_w_kv, v_w_kv), "w_o": (w_o, m_w_o, v_w_o),
           "w_up": (w_up, m_w_up, v_w_up), "w_down": (w_down, m_w_down, v_w_down),
           "pool_w": (pool_w, m_pool_w, v_pool_w), "pool_scale": (pool_scale, m_pool_scale, v_pool_scale)}
    res = {n: reduce_adamw(recv[n], *wmv, name=f"adamw_{n}") for n, wmv in big.items()}
    gain_groups = [(mix_pre_g, m_mix_pre_g, v_mix_pre_g), (mix_post_g, m_mix_post_g, v_mix_post_g),
                   (mlp_pre_g, m_mlp_pre_g, v_mlp_pre_g), (mlp_post_g, m_mlp_post_g, v_mlp_post_g),
                   (kv_norm_g.reshape(1, D), m_kv_norm_g.reshape(1, D), v_kv_norm_g.reshape(1, D))]
    gres = gains_adamw(gain_parts, gain_groups, name="adamw_gains")
    for n, r in zip(["mix_pre_g", "mix_post_g", "mlp_pre_g", "mlp_post_g"], gres[:4]):
        res[n] = r
    res["kv_norm_g"] = [a.reshape(D) for a in gres[4]]

    order = ["pool_w", "pool_scale", "w_q", "w_kv", "kv_norm_g", "w_o", "w_up", "w_down",
             "mix_pre_g", "mix_post_g", "mlp_pre_g", "mlp_post_g"]
    out = [loss, grad_x]
    for k in range(4):
        out += [res[n][k] for n in order]
    return tuple(out)
```

```python
import functools

import jax
import jax.numpy as jnp
from jax import lax
from jax.experimental import pallas as pl
from jax.experimental.pallas import tpu as pltpu

F32 = jnp.float32
BF16 = jnp.bfloat16

EPS = 1e-6
HEAD_DIM = 64
LANES = 128
POOL_WINDOWS = (2, 4, 8, 16)
HALO = 16
N_DEV = 8
MESH_AXES = ("x", "y", "c")

ADAM_LR = 0.001
ADAM_B1 = 0.9
ADAM_B2 = 0.999
ADAM_EPS = 1e-08
ADAM_WD = 0.01
ADAM_STEP = 10

ROW_TILE = 256
ATT_TILE = 256
ROW_CHUNK = 32
GAIN_ROWS = 40


def _tile(n, want):
    return want if n % want == 0 else n


def matmul(a, b, *, name, ta=False, tb=False, a_layer=None, b_layer=None, out_dtype=F32,
           a_fn=None, epi=None, epi_in=None, out_scale=None, tm=1024, tn=1024, tk=1024):
    a2 = a.shape[1:] if a_layer is not None else a.shape
    b2 = b.shape[1:] if b_layer is not None else b.shape
    (K, M) = a2 if ta else a2[::-1]
    if not ta:
        M, K = a2
    if tb:
        N, Kb = b2
    else:
        Kb, N = b2
    assert K == Kb, (a.shape, b.shape)
    tm, tn, tk = _tile(M, tm), _tile(N, tn), _tile(K, tk)
    nk = K // tk
    grid = (M // tm, N // tn, nk)

    def lead(layer, shape, imap):
        if layer is None:
            return pl.BlockSpec(shape, imap)
        return pl.BlockSpec((None,) + shape, lambda i, j, k: (layer,) + imap(i, j, k))

    a_spec = lead(a_layer, (tk, tm) if ta else (tm, tk), (lambda i, j, k: (k, i)) if ta else (lambda i, j, k: (i, k)))
    b_spec = lead(b_layer, (tn, tk) if tb else (tk, tn), (lambda i, j, k: (j, k)) if tb else (lambda i, j, k: (k, j)))
    in_specs = [a_spec, b_spec]
    operands = [a, b]
    if epi is not None:
        in_specs.append(pl.BlockSpec((tm, tn), lambda i, j, k: (i, j)))
        operands.append(epi_in)
    out_shape = jax.ShapeDtypeStruct((M, N), out_dtype)
    out_spec = pl.BlockSpec((tm, tn), lambda i, j, k: (i, j))
    dims = (((0 if ta else 1,), (1 if tb else 0,)), ((), ()))
    n_in = len(operands)

    def body(*refs):
        a_ref, b_ref = refs[0], refs[1]
        e_ref = refs[2] if epi is not None else None
        o_ref = refs[n_in]

        def product():
            av = a_ref[...]
            if a_fn is not None:
                av = a_fn(av)
            return lax.dot_general(av.astype(BF16), b_ref[...].astype(BF16), dims, preferred_element_type=F32)

        def finish(r):
            if epi is not None:
                r = epi(r, e_ref[...])
            if out_scale is not None:
                r = r * out_scale
            o_ref[...] = r.astype(out_dtype)

        if nk == 1:
            finish(product())
            return
        acc_ref = refs[n_in + 1]
        k = pl.program_id(2)

        @pl.when(k == 0)
        def _():
            acc_ref[...] = product()

        @pl.when(k > 0)
        def _():
            acc_ref[...] += product()

        @pl.when(k == nk - 1)
        def _():
            finish(acc_ref[...])

    return pl.pallas_call(
        body, name=name, grid=grid, in_specs=in_specs, out_specs=out_spec, out_shape=out_shape,
        scratch_shapes=[pltpu.VMEM((tm, tn), F32)] if nk > 1 else [],
        compiler_params=pltpu.CompilerParams(dimension_semantics=("parallel", "parallel", "arbitrary")),
    )(*operands)


def _relu2(u):
    r = jnp.maximum(u, 0.0)
    return r * r


def _relu2_grad(acc, u):
    return acc * (2.0 * jnp.maximum(u, 0.0))


def rowwise(fn, rows, vecs, out_rows, n_acc, *, name, tile=ROW_TILE):
    S = rows[0].shape[0]
    tile = _tile(S, tile)
    n_rows, n_vecs, n_out = len(rows), len(vecs), len(out_rows)
    acc_cols = [None] * n_acc

    def body(*refs):
        ins = [r[...] for r in refs[:n_rows + n_vecs]]
        outs = refs[n_rows + n_vecs:]
        ro, ac = fn(*ins)
        assert len(ro) == n_out and len(ac) == n_acc
        for r, o in zip(outs[:n_out], ro):
            r[...] = o.astype(r.dtype)
        i = pl.program_id(0)
        for r, a in zip(outs[n_out:], ac):
            @pl.when(i == 0)
            def _():
                r[...] = jnp.zeros_like(r)
            r[...] += a

    acc_shapes = jax.eval_shape(
        lambda *xs: fn(*xs)[1],
        *[jax.ShapeDtypeStruct((tile, r.shape[1]), r.dtype) for r in rows],
        *[jax.ShapeDtypeStruct(v.shape, v.dtype) for v in vecs])
    in_specs = [pl.BlockSpec((tile, r.shape[1]), lambda i: (i, 0)) for r in rows]
    in_specs += [pl.BlockSpec(v.shape, lambda i: (0, 0)) for v in vecs]
    out_specs = [pl.BlockSpec((tile, c), lambda i: (i, 0)) for c, _ in out_rows]
    out_specs += [pl.BlockSpec(a.shape, lambda i: (0, 0)) for a in acc_shapes]
    out_shape = [jax.ShapeDtypeStruct((S, c), dt) for c, dt in out_rows]
    out_shape += [jax.ShapeDtypeStruct(a.shape, F32) for a in acc_shapes]
    del acc_cols
    return pl.pallas_call(
        body, name=name, grid=(S // tile,), in_specs=in_specs, out_specs=out_specs, out_shape=out_shape,
        compiler_params=pltpu.CompilerParams(dimension_semantics=("arbitrary",)),
    )(*rows, *vecs)


def _rms(x, g):
    r = lax.rsqrt(jnp.mean(x * x, axis=-1, keepdims=True) + EPS)
    return x * r * g


def _rms_bwd(x, g, dy):
    r = lax.rsqrt(jnp.mean(x * x, axis=-1, keepdims=True) + EPS)
    xh = x * r
    dyg = dy * g
    dx = r * (dyg - xh * jnp.mean(dyg * xh, axis=-1, keepdims=True))
    dg = jnp.sum(dy * xh, axis=0, keepdims=True)
    return dx, dg


def norm_only(x, g, dtype, *, name):
    D = x.shape[1]
    return rowwise(lambda xv, gv: ([_rms(xv, gv)], []), [x], [g], [(D, dtype)], 0, name=name)[0]


def residual_norms(x, m, g_post, next_gs, next_dtypes, *, name):
    D = x.shape[1]

    def fn(xv, mv, gp, *gs):
        xn = xv + _rms(mv, gp)
        return [xn] + [_rms(xn, g) for g in gs], []

    return rowwise(fn, [x, m], [g_post] + list(next_gs), [(D, F32)] + [(D, dt) for dt in next_dtypes], 0, name=name)


def residual_loss(x, m, g_post, target, *, name):
    D = x.shape[1]

    def fn(xv, mv, tv, gp):
        e = xv + _rms(mv, gp) - tv
        return [e * (1.0 / D)], [jnp.sum(e * e, axis=0, keepdims=True) * (0.5 / D)]

    return rowwise(fn, [x, m, target], [g_post], [(D, F32)], 1, name=name)


def post_norm_bwd(d, g_post, dy, dtype, *, name):
    D = d.shape[1]

    def fn(dv, dyv, gp):
        dd, dg = _rms_bwd(dv, gp, dyv)
        return [dd], [dg]

    return rowwise(fn, [d, dy], [g_post], [(D, dtype)], 1, name=name)


def mid_bwd(dy, x_mid, dh2, m, g_mlp_pre, g_mix_post, dm_dtype, *, name):
    D = dy.shape[1]

    def fn(dyv, xm, dh, mv, gpre, gpost):
        dx, dg_pre = _rms_bwd(xm, gpre, dh)
        dxm = dyv + dx
        dm, dg_post = _rms_bwd(mv, gpost, dxm)
        return [dxm, dm], [dg_pre, dg_post]

    return rowwise(fn, [dy, x_mid, dh2, m], [g_mlp_pre, g_mix_post], [(D, F32), (D, dm_dtype)], 2, name=name)


def pre_norm_bwd(dxm, x, dhs, gs, *, name):
    D = x.shape[1]
    n = len(dhs)

    def fn(dxv, xv, *rest):
        dh, g = rest[:n], rest[n:]
        out, accs = dxv, []
        for k in range(n):
            dx, dg = _rms_bwd(xv, g[k], dh[k])
            out = out + dx
            accs.append(dg)
        return [out], accs

    return rowwise(fn, [dxm, x] + list(dhs), list(gs), [(D, F32)], n, name=name)


def sum_concat_cast(pairs, dtype, *, name):
    C = pairs[0][0].shape[1]
    n = len(pairs)

    def fn(*xs):
        return [jnp.concatenate([sum(xs[:n]), sum(xs[n:])], axis=1)], []

    return rowwise(fn, [a for a, _ in pairs] + [b for _, b in pairs], [], [(2 * C, dtype)], 0, name=name)[0]


def _window_sum(e, window, total_rows, backward):
    s, k = e, 1
    while k < window:
        s = s + pltpu.roll(s, (total_rows - k) if backward else k, 0)
        k *= 2
    return s


def pool_fwd(h, w, scale, *, name):
    S, D = h.shape
    G = len(POOL_WINDOWS)
    GC = D // G
    tile = _tile(S, ROW_TILE)
    hb = tile // HALO

    def body(hc_ref, hp_ref, w_ref, sc_ref, o_ref):
        i = pl.program_id(0)
        prev = jnp.where(i > 0, hp_ref[...], 0.0)
        ext = jnp.concatenate([prev, hc_ref[...]], axis=0)
        t = i * tile + lax.broadcasted_iota(jnp.int32, (tile, 1), 0)
        outs = []
        for g, window in enumerate(POOL_WINDOWS):
            e = ext[:, g * GC:(g + 1) * GC]
            s = _window_sum(e, window, HALO + tile, False)[HALO:, :]
            cnt = jnp.minimum(t + 1, window).astype(F32)
            y = s / cnt - e[HALO:, :]
            outs.append(jnp.dot(y.astype(BF16), w_ref[g], preferred_element_type=F32))
        o_ref[...] = jnp.concatenate(outs, axis=1) * sc_ref[...]

    return pl.pallas_call(
        body, name=name, grid=(S // tile,),
        in_specs=[pl.BlockSpec((tile, D), lambda i: (i, 0)),
                  pl.BlockSpec((HALO, D), lambda i: (jnp.maximum(i * hb - 1, 0), 0)),
                  pl.BlockSpec((G, GC, GC), lambda i: (0, 0, 0)),
                  pl.BlockSpec((1, D), lambda i: (0, 0))],
        out_specs=pl.BlockSpec((tile, D), lambda i: (i, 0)),
        out_shape=jax.ShapeDtypeStruct((S, D), F32),
        compiler_params=pltpu.CompilerParams(dimension_semantics=("parallel",)),
    )(h, h, w, scale)


def pool_bwd(h, dm, w, scale, *, name):
    S, D = h.shape
    G = len(POOL_WINDOWS)
    GC = D // G
    tile = _tile(S, ROW_TILE)
    hb = tile // HALO
    n_tiles = S // tile
    last_halo = S // HALO - 1

    def body(hc_ref, hp_ref, dmc_ref, dmn_ref, w_ref, sc_ref, dh_ref, dw_ref, dsc_ref):
        i = pl.program_id(0)

        @pl.when(i == 0)
        def _():
            dw_ref[...] = jnp.zeros_like(dw_ref)
            dsc_ref[...] = jnp.zeros_like(dsc_ref)

        prev = jnp.where(i > 0, hp_ref[...], 0.0)
        ext = jnp.concatenate([prev, hc_ref[...]], axis=0)
        nxt = jnp.where(i < n_tiles - 1, dmn_ref[...], 0.0)
        dmc = dmc_ref[...]
        dm_ext = jnp.concatenate([dmc, nxt], axis=0)
        t = i * tile + lax.broadcasted_iota(jnp.int32, (tile, 1), 0)
        t_ext = i * tile + lax.broadcasted_iota(jnp.int32, (tile + HALO, 1), 0)
        dhs, dscs = [], []
        for g, window in enumerate(POOL_WINDOWS):
            cols = slice(g * GC, (g + 1) * GC)
            e = ext[:, cols]
            s = _window_sum(e, window, HALO + tile, False)[HALO:, :]
            y = (s / jnp.minimum(t + 1, window).astype(F32) - e[HALO:, :]).astype(BF16)
            wg = w_ref[g]
            ypre = jnp.dot(y, wg, preferred_element_type=F32)
            dscs.append(jnp.sum(dmc[:, cols] * ypre, axis=0, keepdims=True))
            dyp = (dm_ext[:, cols] * sc_ref[:, cols]).astype(BF16)
            dw_ref[g] += lax.dot_general(y, dyp[:tile, :], (((0,), (0,)), ((), ())), preferred_element_type=F32)
            dy = lax.dot_general(dyp, wg, (((1,), (1,)), ((), ())), preferred_element_type=F32)
            r = dy / jnp.minimum(t_ext + 1, window).astype(F32)
            sr = _window_sum(r, window, tile + HALO, True)
            dhs.append(sr[:tile, :] - dy[:tile, :])
        dh_ref[...] = jnp.concatenate(dhs, axis=1)
        dsc_ref[...] += jnp.concatenate(dscs, axis=1)

    return pl.pallas_call(
        body, name=name, grid=(n_tiles,),
        in_specs=[pl.BlockSpec((tile, D), lambda i: (i, 0)),
                  pl.BlockSpec((HALO, D), lambda i: (jnp.maximum(i * hb - 1, 0), 0)),
                  pl.BlockSpec((tile, D), lambda i: (i, 0)),
                  pl.BlockSpec((HALO, D), lambda i: (jnp.minimum((i + 1) * hb, last_halo), 0)),
                  pl.BlockSpec((G, GC, GC), lambda i: (0, 0, 0)),
                  pl.BlockSpec((1, D), lambda i: (0, 0))],
        out_specs=[pl.BlockSpec((tile, D), lambda i: (i, 0)),
                   pl.BlockSpec((G, GC, GC), lambda i: (0, 0, 0)),
                   pl.BlockSpec((1, D), lambda i: (0, 0))],
        out_shape=[jax.ShapeDtypeStruct((S, D), F32), jax.ShapeDtypeStruct((G, GC, GC), F32),
                   jax.ShapeDtypeStruct((1, D), F32)],
        compiler_params=pltpu.CompilerParams(dimension_semantics=("arbitrary",)),
    )(h, h, dm, dm, w, scale)


ATT_LANES = 256
N_PAIR = ATT_LANES // HEAD_DIM


def _scores(xs, kj):
    return [lax.dot_general(x, kj, (((1,), (1,)), ((), ())), preferred_element_type=F32) for x in xs]


def _softplus_parts(z, mask):
    sp = jnp.maximum(z, 0.0) + jnp.log(1.0 + jnp.exp(-jnp.abs(z)))
    logb = z - sp
    if mask is not None:
        sp = jnp.where(mask, sp, 0.0)
    return logb, sp.astype(BF16), jnp.sum(sp, axis=1, keepdims=True)


def _weights(logb, later, c, mask):
    a = jnp.exp(logb - (later + c))
    return a if mask is None else jnp.where(mask, a, 0.0)


def _tile_weights(zss, cs, u_later, mask):
    partss = [[_softplus_parts(z, mask) for z in zs] for zs in zss]
    laterss = [[jnp.dot(sp, u_later, preferred_element_type=F32) for _, sp, _ in parts] for parts in partss]
    out = []
    for parts, laters in zip(partss, laterss):
        out.append(([p[0] for p in parts], [_weights(p[0], later, c, mask) for p, later, c in zip(parts, laters, cs)]))
        cs = [c + p[2] for c, p in zip(cs, parts)]
    return out, tuple(cs)


def _tri(T, later):
    rows = lax.broadcasted_iota(jnp.int32, (T, T), 0)
    cols = lax.broadcasted_iota(jnp.int32, (T, T), 1)
    return jnp.where((rows > cols) if later else (rows < cols), 1.0, 0.0).astype(BF16)


def _head_masks(x2, axis=None):
    lane = lax.broadcasted_iota(jnp.int32, (1, ATT_LANES), 1)
    parts = [jnp.where((lane // HEAD_DIM) == hh, x2, jnp.zeros_like(x2)) for hh in range(N_PAIR)]
    return parts if axis is None else jnp.concatenate(parts, axis=axis)


def _cat_bf16(parts, axis):
    return jnp.concatenate([p.astype(BF16) for p in parts], axis=axis)


def _side_split(side, refs, n_in, n_out):
    if side is None:
        return refs, None
    n_src, n_buf = len(side.srcs), len(side.bufs)
    ins, rest = refs[:n_in], refs[n_in:]
    src_refs, rest = rest[:n_src], rest[n_src + n_buf:]
    outs, rest = rest[:n_out], rest[n_out:]
    buf_refs, rest = rest[:n_buf], rest[n_buf:]
    own_scratch, sems = rest[:len(rest) - 3], rest[len(rest) - 3:]
    return tuple(ins) + tuple(outs) + tuple(own_scratch), (src_refs, buf_refs, sems)


def _side_phase(side, side_refs, phase, when):
    if side is None:
        return

    @pl.when(when)
    def _():
        side.copies(phase, *side_refs)


def attn_fwd(q, kv, *, name, side=None):
    S, D = q.shape
    P = D // ATT_LANES
    T = _tile(S, ATT_TILE)
    nq = S // T

    def body(*refs):
        (q_ref, k_ref, v_ref, o_ref), side_refs = _side_split(side, refs, 3, 1)
        i = pl.program_id(1)
        p = pl.program_id(0)
        _side_phase(side, side_refs, 0, (p == 0) & (i == 0))
        u_later = _tri(T, True)
        qhs = _head_masks(q_ref[...])
        diag = lax.broadcasted_iota(jnp.int32, (T, T), 1) < lax.broadcasted_iota(jnp.int32, (T, T), 0)

        def rows_of(ref, j):
            return ref[pl.ds(pl.multiple_of(j * T, T), T), :]

        def step(js, carry, mask):
            cs, acc = carry
            zss = [_scores(qhs, rows_of(k_ref, j)) for j in js]
            vcat = jnp.concatenate([_head_masks(rows_of(v_ref, j), axis=0) for j in js], axis=0)
            per_tile, cs = _tile_weights(zss, cs, u_later, mask)
            acat = _cat_bf16([a for _, aa in per_tile for a in aa], 1)
            return cs, acc + jnp.dot(acat, vcat, preferred_element_type=F32)

        carry = step([i], ((jnp.zeros((T, 1), F32),) * N_PAIR, jnp.zeros((T, ATT_LANES), F32)), diag)
        carry = lax.fori_loop(0, i % 2, lambda n, cr: step([i - 1], cr, None), carry)
        first = i - 1 - i % 2
        _, acc = lax.fori_loop(0, i // 2, lambda n, cr: step([first - 2 * n, first - 2 * n - 1], cr, None), carry)
        o_ref[...] = acc.astype(o_ref.dtype)
        _side_phase(side, side_refs, 1, (p == P - 1) & (i == nq - 1))

    sd = side
    outs = pl.pallas_call(
        body, name=name, grid=(P, nq),
        in_specs=[pl.BlockSpec((T, ATT_LANES), lambda p, i: (i, p)),
                  pl.BlockSpec((S, ATT_LANES), lambda p, i: (0, p)),
                  pl.BlockSpec((S, ATT_LANES), lambda p, i: (0, P + p))] + (sd.specs() if sd else []),
        out_specs=[pl.BlockSpec((T, ATT_LANES), lambda p, i: (i, p))] + (sd.out_specs() if sd else []),
        out_shape=[jax.ShapeDtypeStruct((S, D), BF16)] + (sd.out_shape() if sd else []),
        scratch_shapes=sd.scratch() if sd else [],
        input_output_aliases=sd.aliases(3, 1) if sd else {},
        compiler_params=pltpu.CompilerParams(dimension_semantics=("arbitrary", "arbitrary"),
                                             has_side_effects=sd is not None),
    )(q, kv, kv, *(sd.operands() if sd else []))
    return outs[0] if sd is None else (outs[0], sd.result(outs[1:]))


def attn_bwd(q, kv, do, *, name, side=None):
    S, D = q.shape
    P = D // ATT_LANES
    T = _tile(S, ATT_TILE)
    nb = S // T

    def body(*refs):
        (q_ref, k_ref, v_ref, do_ref, dq_ref, dk_ref, dv_ref, g_scr, s_scr), side_refs = _side_split(side, refs, 4, 3)
        i = pl.program_id(1)
        p = pl.program_id(0)
        _side_phase(side, side_refs, 0, (p == 0) & (i == 0))

        @pl.when(i == 0)
        def _():
            dk_ref[...] = jnp.zeros_like(dk_ref)
            dv_ref[...] = jnp.zeros_like(dv_ref)

        u_later = _tri(T, True)
        u_earlier = _tri(T, False)
        qhs = _head_masks(q_ref[...])
        dohs = _head_masks(do_ref[...])
        qcat = jnp.concatenate(qhs, axis=0)
        docat = jnp.concatenate(dohs, axis=0)
        diag = lax.broadcasted_iota(jnp.int32, (T, T), 1) < lax.broadcasted_iota(jnp.int32, (T, T), 0)
        tdot = (((0,), (0,)), ((), ()))

        def rows_of(ref, j):
            return ref.at[pl.ds(pl.multiple_of(j * T, T), T), :]

        def step1(js, cs, mask):
            dass = [_scores(dohs, rows_of(v_ref, j)[...]) for j in js]
            zss = [_scores(qhs, rows_of(k_ref, j)[...]) for j in js]
            per_tile, cs = _tile_weights(zss, cs, u_later, mask)
            for j, das, (logbs, aa) in zip(js, dass, per_tile):
                for hh in range(N_PAIR):
                    g_scr[hh, j] = (das[hh] * aa[hh]).astype(BF16)
                    sg = jnp.exp(logbs[hh])
                    if mask is not None:
                        sg = jnp.where(mask, sg, 0.0)
                    s_scr[hh, j] = sg.astype(BF16)
            for j, (_, aa) in zip(js, per_tile):
                rows_of(dv_ref, j)[...] += lax.dot_general(_cat_bf16(aa, 0), docat, tdot,
                                                          preferred_element_type=F32)
            return cs

        cs = step1([i], (jnp.zeros((T, 1), F32),) * N_PAIR, diag)
        cs = lax.fori_loop(0, i % 2, lambda n, c: step1([i - 1], c, None), cs)
        first = i - 1 - i % 2
        lax.fori_loop(0, i // 2, lambda n, c: step1([first - 2 * n, first - 2 * n - 1], c, None), cs)

        def step2(js, carry):
            cs, acc = carry
            cumss = [[jnp.dot(g_scr[hh, j], u_earlier, preferred_element_type=F32) for hh in range(N_PAIR)]
                     for j in js]
            kcat = jnp.concatenate([_head_masks(rows_of(k_ref, j)[...], axis=0) for j in js], axis=0)
            dzss = []
            for j, cums in zip(js, cumss):
                dzs, new = [], []
                for hh in range(N_PAIR):
                    gf = g_scr[hh, j].astype(F32)
                    sg = s_scr[hh, j].astype(F32)
                    dzs.append((gf - sg * (gf + (cums[hh] + cs[hh]))).astype(BF16))
                    new.append(cs[hh] + jnp.sum(gf, axis=1, keepdims=True))
                cs = tuple(new)
                dzss.append(dzs)
            acc = acc + jnp.dot(jnp.concatenate([dz for dzs in dzss for dz in dzs], axis=1), kcat,
                                preferred_element_type=F32)
            for j, dzs in zip(js, dzss):
                rows_of(dk_ref, j)[...] += lax.dot_general(jnp.concatenate(dzs, axis=0), qcat, tdot,
                                                          preferred_element_type=F32)
            return cs, acc

        carry = ((jnp.zeros((T, 1), F32),) * N_PAIR, jnp.zeros((T, ATT_LANES), F32))
        carry = lax.fori_loop(0, (i + 1) // 2, lambda n, cr: step2([2 * n, 2 * n + 1], cr), carry)
        _, dq = lax.fori_loop(0, (i + 1) % 2, lambda n, cr: step2([i], cr), carry)
        dq_ref[...] = (dq * (HEAD_DIM ** -0.5)).astype(dq_ref.dtype)
        _side_phase(side, side_refs, 1, (p == P - 1) & (i == nb - 1))

    sd = side
    outs = pl.pallas_call(
        body, name=name, grid=(P, nb),
        in_specs=[pl.BlockSpec((T, ATT_LANES), lambda p, i: (i, p)),
                  pl.BlockSpec((S, ATT_LANES), lambda p, i: (0, p)),
                  pl.BlockSpec((S, ATT_LANES), lambda p, i: (0, P + p)),
                  pl.BlockSpec((T, ATT_LANES), lambda p, i: (i, p))] + (sd.specs() if sd else []),
        out_specs=[pl.BlockSpec((T, ATT_LANES), lambda p, i: (i, p)),
                   pl.BlockSpec((S, ATT_LANES), lambda p, i: (0, p)),
                   pl.BlockSpec((S, ATT_LANES), lambda p, i: (0, p))] + (sd.out_specs() if sd else []),
        out_shape=[jax.ShapeDtypeStruct((S, D), BF16), jax.ShapeDtypeStruct((S, D), F32),
                   jax.ShapeDtypeStruct((S, D), F32)] + (sd.out_shape() if sd else []),
        scratch_shapes=[pltpu.VMEM((N_PAIR, nb, T, T), BF16), pltpu.VMEM((N_PAIR, nb, T, T), BF16)]
        + (sd.scratch() if sd else []),
        input_output_aliases=sd.aliases(4, 3) if sd else {},
        compiler_params=pltpu.CompilerParams(dimension_semantics=("arbitrary", "arbitrary"),
                                             has_side_effects=sd is not None),
    )(q, kv, kv, do, *(sd.operands() if sd else []))
    return tuple(outs[:3]) if sd is None else (*outs[:3], sd.result(outs[3:]))


def _window(ref, axis, dev, n):
    return ref.at[(slice(None),) * axis + (pl.ds(dev * n, n),)]


def all_gather(ops, *, name):
    n_ops = len(ops)
    out_shape = []
    for a, ax in ops:
        shp = list(a.shape)
        shp[ax] *= N_DEV
        out_shape.append(jax.ShapeDtypeStruct(tuple(shp), a.dtype))

    def body(*refs):
        ins, outs = refs[:n_ops], refs[n_ops:2 * n_ops]
        send_sems, recv_sems, local_sems = refs[2 * n_ops:]
        x, y, c = (lax.axis_index(n) for n in MESH_AXES)
        me, sibling = (x, y, c), (x, y, 1 - c)
        chips = [(1 - x, y), (x, 1 - y), (1 - x, 1 - y)]

        def rows(o, dev):
            px, py, pc = dev
            ax = ops[o][1]
            return _window(outs[o], ax, 4 * px + 2 * py + pc, ops[o][0].shape[ax])

        def copy(o, k, block, to, src=None):
            return pltpu.make_async_remote_copy(
                src_ref=rows(o, block) if src is None else src, dst_ref=rows(o, block),
                send_sem=send_sems.at[o, k], recv_sem=recv_sems.at[o, k],
                device_id=to, device_id_type=pl.DeviceIdType.MESH)

        mine, first, passed = [], [], []
        for o in range(n_ops):
            cp = pltpu.make_async_copy(ins[o], rows(o, me), local_sems.at[o])
            cp.start()
            mine.append(cp)
            first.append(copy(o, 0, me, sibling, src=ins[o]))
            first += [copy(o, 1 + j, me, (*chip, c), src=ins[o]) for j, chip in enumerate(chips)]
        for cp in first:
            cp.start()
        for j, chip in enumerate(chips):
            for o in range(n_ops):
                copy(o, 1 + j, (*chip, c), me).wait_recv()
                cp = copy(o, 4 + j, (*chip, c), sibling)
                cp.start()
                passed.append(cp)
        for o in range(n_ops):
            copy(o, 0, sibling, me).wait_recv()
            for j, chip in enumerate(chips):
                copy(o, 4 + j, (*chip, 1 - c), me).wait_recv()
        for cp in first + passed:
            cp.wait_send()
        for cp in mine:
            cp.wait()

    any_spec = pl.BlockSpec(memory_space=pl.ANY)
    return pl.pallas_call(
        body, name=name, in_specs=[any_spec] * n_ops, out_specs=[any_spec] * n_ops, out_shape=out_shape,
        scratch_shapes=[pltpu.SemaphoreType.DMA((n_ops, 7)), pltpu.SemaphoreType.DMA((n_ops, 7)),
                        pltpu.SemaphoreType.DMA((n_ops,))],
        compiler_params=pltpu.CompilerParams(has_side_effects=True),
    )(*[a for a, _ in ops])


class Side:
    def __init__(self, kind, items, bufs):
        self.kind, self.items = kind, items
        self.names = list(bufs)
        self.bufs = [bufs[n] for n in self.names]
        self.srcs = [it[0] for it in items]

    def operands(self):
        return self.srcs + self.bufs

    def specs(self):
        return [pl.BlockSpec(memory_space=pl.ANY)] * (len(self.srcs) + len(self.bufs))

    def out_specs(self):
        return [pl.BlockSpec(memory_space=pl.ANY)] * len(self.bufs)

    def out_shape(self):
        return [jax.ShapeDtypeStruct(b.shape, b.dtype) for b in self.bufs]

    def aliases(self, first_in, first_out):
        return {first_in + len(self.srcs) + k: first_out + k for k in range(len(self.bufs))}

    def scratch(self):
        n = len(self.items)
        return [pltpu.SemaphoreType.DMA((n, N_DEV - 1)), pltpu.SemaphoreType.DMA((n, N_DEV - 1)),
                pltpu.SemaphoreType.DMA((n,))]

    def result(self, outs):
        return dict(zip(self.names, outs))

    def copies(self, phase, src_refs, buf_refs, sems):
        send_sems, recv_sems, local_sems = sems
        pos = tuple(lax.axis_index(n) for n in MESH_AXES)
        me = 4 * pos[0] + 2 * pos[1] + pos[2]
        for o, (_, ax, name, layer) in enumerate(self.items):
            src, buf = src_refs[o], buf_refs[self.names.index(name)]
            if self.kind == "gather":
                whole = buf if layer is None else buf.at[layer]
                n = src.shape[ax]
                sent = lambda dev, src=src: src
                lands = lambda dev, whole=whole, ax=ax, n=n: _window(whole, ax, dev, n)
            else:
                n = src.shape[ax] // N_DEV
                sent = lambda dev, src=src, ax=ax, n=n: _window(src, ax, dev, n)
                lands = lambda dev, buf=buf, layer=layer: buf.at[dev] if layer is None else buf.at[dev, layer]
            local = pltpu.make_async_copy(sent(me), lands(me), local_sems.at[o])
            if phase == 0:
                local.start()
            else:
                local.wait()
            for r in range(1, N_DEV):
                peer = tuple(1 - p if r & bit else p for p, bit in zip(pos, (4, 2, 1)))
                pid = 4 * peer[0] + 2 * peer[1] + peer[2]
                cp = pltpu.make_async_remote_copy(
                    src_ref=sent(pid) if phase == 0 else sent(me), dst_ref=lands(me) if phase == 0 else lands(pid),
                    send_sem=send_sems.at[o, r - 1], recv_sem=recv_sems.at[o, r - 1],
                    device_id=peer, device_id_type=pl.DeviceIdType.MESH)
                if phase == 0:
                    cp.start()
                else:
                    cp.wait_recv()
                    cp.wait_send()


def run_side(side, *, name):
    n_src, n_buf = len(side.srcs), len(side.bufs)

    def body(*refs):
        src_refs, buf_refs = refs[:n_src], refs[n_src + n_buf:n_src + 2 * n_buf]
        sems = refs[n_src + 2 * n_buf:]
        side.copies(0, src_refs, buf_refs, sems)
        side.copies(1, src_refs, buf_refs, sems)

    outs = pl.pallas_call(
        body, name=name, in_specs=side.specs(), out_specs=side.out_specs(), out_shape=side.out_shape(),
        scratch_shapes=side.scratch(), input_output_aliases=side.aliases(0, 0),
        compiler_params=pltpu.CompilerParams(has_side_effects=True),
    )(*side.operands())
    return side.result(outs)


def _adamw(w, g, m, v):
    m = ADAM_B1 * m + (1.0 - ADAM_B1) * g
    v = ADAM_B2 * v + (1.0 - ADAM_B2) * (g * g)
    m_hat = m / (1.0 - ADAM_B1 ** ADAM_STEP)
    v_hat = v / (1.0 - ADAM_B2 ** ADAM_STEP)
    delta = -ADAM_LR * (m_hat / (jnp.sqrt(v_hat) + ADAM_EPS) + ADAM_WD * w)
    return delta, m, v


def reduce_adamw(parts, w, m, v, *, name):
    shape = w.shape
    C = shape[-1]
    R = w.size // C
    tile = _tile(R, 256)

    def body(p_ref, w_ref, m_ref, v_ref, g_ref, d_ref, nm_ref, nv_ref):
        g = p_ref[0].astype(F32)
        for s in range(1, N_DEV):
            g = g + p_ref[s].astype(F32)
        d, nm, nv = _adamw(w_ref[...], g, m_ref[...], v_ref[...])
        g_ref[...] = g
        d_ref[...] = d
        nm_ref[...] = nm
        nv_ref[...] = nv

    row = pl.BlockSpec((tile, C), lambda i: (i, 0))
    outs = pl.pallas_call(
        body, name=name, grid=(R // tile,),
        in_specs=[pl.BlockSpec((N_DEV, tile, C), lambda i: (0, i, 0)), row, row, row],
        out_specs=[row] * 4, out_shape=[jax.ShapeDtypeStruct((R, C), F32)] * 4,
        compiler_params=pltpu.CompilerParams(dimension_semantics=("parallel",)),
    )(parts.reshape(N_DEV, R, C), w.reshape(R, C), m.reshape(R, C), v.reshape(R, C))
    return [o.reshape(shape) for o in outs]


def gains_adamw(parts, groups, *, name):
    n = len(groups)

    def body(*refs):
        p_ref = refs[0]
        ins, outs = refs[1:1 + 3 * n], refs[1 + 3 * n:]
        for k in range(n):
            w_ref, m_ref, v_ref = ins[3 * k:3 * k + 3]
            L = w_ref.shape[0]
            g = p_ref[pl.ds(8 * k, L), :]
            for s in range(1, N_DEV):
                g = g + p_ref[pl.ds(s * GAIN_ROWS + 8 * k, L), :]
            d, nm, nv = _adamw(w_ref[...], g, m_ref[...], v_ref[...])
            for r, val in zip(outs[4 * k:4 * k + 4], (g, d, nm, nv)):
                r[...] = val

    flat = [a for grp in groups for a in grp]
    out_shape = [jax.ShapeDtypeStruct(grp[0].shape, F32) for grp in groups for _ in range(4)]
    outs = pl.pallas_call(body, name=name, out_shape=out_shape)(parts, *flat)
    return [outs[4 * k:4 * k + 4] for k in range(n)]


def kernel(x, pool_w, pool_scale, w_q, w_kv, kv_norm_g, w_o, w_up, w_down, mix_pre_g, mix_post_g, mlp_pre_g, mlp_post_g, loss_target, m_pool_w, m_pool_scale, m_w_q, m_w_kv, m_kv_norm_g, m_w_o, m_w_up, m_w_down, m_mix_pre_g, m_mix_post_g, m_mlp_pre_g, m_mlp_post_g, v_pool_w, v_pool_scale, v_w_q, v_w_kv, v_kv_norm_g, v_w_o, v_w_up, v_w_down, v_mix_pre_g, v_mix_post_g, v_mlp_pre_g, v_mlp_post_g):
    _, S, D = x.shape
    x0 = x.reshape(S, D)
    target = loss_target.reshape(S, D)
    depth = w_up.shape[0]
    n_pool = pool_w.shape[0]
    F = w_up.shape[2] * N_DEV
    G = pool_w.shape[1]
    GC = D // G

    def vec(a, l):
        return a[l].reshape(1, D)

    n_att = depth - n_pool
    wq_s, wkv_s, wo_s = w_q.astype(BF16), w_kv.astype(BF16), w_o.astype(BF16)
    wup_s, wdn_s, pw_s = w_up.astype(BF16), w_down.astype(BF16), pool_w.astype(BF16)
    wq0, wkv, wup_a, wdn_a, pw, psc = all_gather(
        [(wq_s[0], 0), (wkv_s, 1), (wup_s[:n_pool], 2), (wdn_s[:n_pool], 1), (pw_s, 2), (pool_scale, 1)],
        name="gather_weights")
    late = Side("gather",
                [(wq_s[1:], 1, "wq", None), (wo_s, 1, "wo", None), (wup_s[n_pool:], 2, "wup", None),
                 (wdn_s[n_pool:], 1, "wdn", None)],
                {"wq": lax.empty((n_att - 1, D, D), BF16), "wo": lax.empty((n_att, D, D), BF16),
                 "wup": lax.empty((n_att, D, F), BF16), "wdn": lax.empty((n_att, F, D), BF16)})
    late_w = None

    def layer_of(early, key, l, n_early):
        return (early, l) if l < n_early else (late_w[key], l - n_early)

    saved = []
    xs = x0
    h1 = norm_only(xs, vec(mix_pre_g, 0), F32, name="norm_in")
    kv = hk = None
    dy = loss_rows = None
    for l in range(depth):
        is_pool = l < n_pool
        st = {"x": xs, "h1": h1}
        if is_pool:
            m = pool_fwd(h1, pw[l], psc[l].reshape(1, D), name=f"pool_fwd{l}")
        else:
            j = l - n_pool
            wq_j, wq_l = (wq0, None) if j == 0 else (late_w["wq"], j - 1)
            q = matmul(h1, wq_j, b_layer=wq_l, out_dtype=BF16, out_scale=HEAD_DIM ** -0.5, name=f"q_proj{j}")
            if j == 0:
                o, late_w = attn_fwd(q, kv, name=f"attn_fwd{j}", side=late)
            else:
                o = attn_fwd(q, kv, name=f"attn_fwd{j}")
            m = matmul(o, late_w["wo"], b_layer=j, name=f"o_proj{j}")
            st.update(q=q, o=o)
        x_mid, h2 = residual_norms(xs, m, vec(mix_post_g, l), [vec(mlp_pre_g, l)], [BF16], name=f"mix_out{l}")
        wup, lu = layer_of(wup_a, "wup", l, n_pool)
        wdn, ld = layer_of(wdn_a, "wdn", l, n_pool)
        u = matmul(h2, wup, b_layer=lu, name=f"mlp_up{l}")
        d = matmul(u, wdn, b_layer=ld, a_fn=_relu2, tk=512, name=f"mlp_down{l}")
        st.update(m=m, x_mid=x_mid, h2=h2, u=u, d=d)
        saved.append(st)
        if l == depth - 1:
            dy, loss_rows = residual_loss(x_mid, d, vec(mlp_post_g, l), target, name="loss")
        elif l == n_pool - 1:
            xs, h1, hk = residual_norms(x_mid, d, vec(mlp_post_g, l), [vec(mix_pre_g, l + 1), kv_norm_g.reshape(1, D)],
                                        [BF16, BF16], name=f"mlp_out{l}")
            kv = matmul(hk, wkv, out_dtype=BF16, name="kv_proj")
        else:
            nxt_dt = F32 if l + 1 < n_pool else BF16
            xs, h1 = residual_norms(x_mid, d, vec(mlp_post_g, l), [vec(mix_pre_g, l + 1)], [nxt_dt], name=f"mlp_out{l}")
    loss = lax.psum(jnp.sum(loss_rows), MESH_AXES)

    recv = {"w_q": (w_q, BF16), "w_kv": (w_kv, BF16), "w_o": (w_o, BF16), "w_up": (w_up, BF16),
            "w_down": (w_down, BF16), "pool_w": (pool_w, F32), "pool_scale": (pool_scale, F32)}
    recv = {n: lax.empty((N_DEV,) + w.shape, dt) for n, (w, dt) in recv.items()}
    pending = []

    def send_pending():
        return Side("scatter", list(pending), {n: recv[n] for n in dict.fromkeys(it[2] for it in pending)})

    g_pw, g_psc = [None] * n_pool, [None] * n_pool
    gains = {k: [None] * depth for k in ("mix_pre", "mix_post", "mlp_pre", "mlp_post")}
    dkvs = []
    g_kvn = None
    for l in reversed(range(depth)):
        st = saved[l]
        is_pool = l < n_pool
        wup, lu = layer_of(wup_a, "wup", l, n_pool)
        wdn, ld = layer_of(wdn_a, "wdn", l, n_pool)
        dd, gains["mlp_post"][l] = post_norm_bwd(st["d"], vec(mlp_post_g, l), dy, BF16, name=f"mlp_out_bwd{l}")
        du = matmul(dd, wdn, b_layer=ld, tb=True, out_dtype=BF16, epi=_relu2_grad, epi_in=st["u"], name=f"mlp_du{l}")
        g_wdn = matmul(st["u"], dd, ta=True, a_fn=_relu2, tk=512, out_dtype=BF16, name=f"mlp_dwdn{l}")
        g_wup = matmul(st["h2"], du, ta=True, out_dtype=BF16, name=f"mlp_dwup{l}")
        pending += [(g_wup, 1, "w_up", l), (g_wdn, 0, "w_down", l)]
        dh2 = matmul(du, wup, b_layer=lu, tb=True, name=f"mlp_dh{l}")
        dxm, dm, gains["mlp_pre"][l], gains["mix_post"][l] = mid_bwd(
            dy, st["x_mid"], dh2, st["m"], vec(mlp_pre_g, l), vec(mix_post_g, l), F32 if is_pool else BF16,
            name=f"mid_bwd{l}")
        if is_pool:
            dh1, g_pw[l], g_psc[l] = pool_bwd(st["h1"], dm, pw[l], psc[l].reshape(1, D), name=f"pool_bwd{l}")
        else:
            j = l - n_pool
            wq_j, wq_l = (wq0, None) if j == 0 else (late_w["wq"], j - 1)
            do = matmul(dm, late_w["wo"], b_layer=j, tb=True, out_dtype=BF16, name=f"o_proj_dx{j}")
            g_wo = matmul(st["o"], dm, ta=True, out_dtype=BF16, name=f"o_proj_dw{j}")
            pending.append((g_wo, 0, "w_o", j))
            dq, dk, dv, got = attn_bwd(st["q"], kv, do, name=f"attn_bwd{j}", side=send_pending())
            recv.update(got)
            pending.clear()
            dkvs.append((dk, dv))
            g_wq = matmul(st["h1"], dq, ta=True, out_dtype=BF16, name=f"q_proj_dw{j}")
            pending.append((g_wq, 0, "w_q", j))
            dh1 = matmul(dq, wq_j, b_layer=wq_l, tb=True, name=f"q_proj_dx{j}")
        if l == n_pool:
            dkv = sum_concat_cast(dkvs, BF16, name="dkv_pack")
            g_wkv = matmul(hk, dkv, ta=True, out_dtype=BF16, name="kv_proj_dw")
            pending.append((g_wkv, 1, "w_kv", None))
            dhk = matmul(dkv, wkv, tb=True, name="kv_proj_dx")
            dy, gains["mix_pre"][l], g_kvn = pre_norm_bwd(
                dxm, st["x"], [dh1, dhk], [vec(mix_pre_g, l), kv_norm_g.reshape(1, D)], name=f"mix_in_bwd{l}")
        else:
            dy, gains["mix_pre"][l] = pre_norm_bwd(dxm, st["x"], [dh1], [vec(mix_pre_g, l)], name=f"mix_in_bwd{l}")
    grad_x = dy.reshape(x.shape)

    g_pool_w = jnp.stack(g_pw)
    g_pool_scale = jnp.concatenate(g_psc, axis=0)
    zero_rows = jnp.zeros((8 - depth, D), F32)
    gain_rows = []
    for k in ("mix_pre", "mix_post", "mlp_pre", "mlp_post"):
        gain_rows += gains[k] + [zero_rows]
    gain_rows += [g_kvn, jnp.zeros((7, D), F32)]
    gain_pack = jnp.concatenate(gain_rows, axis=0)
    gain_parts = all_gather([(gain_pack, 0)], name="gather_gain_grads")[0]
    pending += [(g_pool_w, 2, "pool_w", None), (g_pool_scale, 1, "pool_scale", None)]
    recv.update(run_side(send_pending(), name="exchange_grads"))
    big = {"w_q": (w_q, m_w_q, v_w_q), "w_kv": (w_kv, m_w_kv, v_w_kv), "w_o": (w_o, m_w_o, v_w_o),
           "w_up": (w_up, m_w_up, v_w_up), "w_down": (w_down, m_w_down, v_w_down),
           "pool_w": (pool_w, m_pool_w, v_pool_w), "pool_scale": (pool_scale, m_pool_scale, v_pool_scale)}
    res = {n: reduce_adamw(recv[n], *wmv, name=f"adamw_{n}") for n, wmv in big.items()}
    gain_groups = [(mix_pre_g, m_mix_pre_g, v_mix_pre_g), (mix_post_g, m_mix_post_g, v_mix_post_g),
                   (mlp_pre_g, m_mlp_pre_g, v_mlp_pre_g), (mlp_post_g, m_mlp_post_g, v_mlp_post_g),
                   (kv_norm_g.reshape(1, D), m_kv_norm_g.reshape(1, D), v_kv_norm_g.reshape(1, D))]
    gres = gains_adamw(gain_parts, gain_groups, name="adamw_gains")
    for n, r in zip(["mix_pre_g", "mix_post_g", "mlp_pre_g", "mlp_post_g"], gres[:4]):
        res[n] = r
    res["kv_norm_g"] = [a.reshape(D) for a in gres[4]]

    order = ["pool_w", "pool_scale", "w_q", "w_kv", "kv_norm_g", "w_o", "w_up", "w_down",
             "mix_pre_g", "mix_post_g", "mlp_pre_g", "mlp_post_g"]
    out = [loss, grad_x]
    for k in range(4):
        out += [res[n][k] for n in order]
    return tuple(out)
```

```python
import functools

import jax
import jax.numpy as jnp
from jax import lax
from jax.experimental import pallas as pl
from jax.experimental.pallas import tpu as pltpu

F32 = jnp.float32
BF16 = jnp.bfloat16

EPS = 1e-6
HEAD_DIM = 64
LANES = 128
POOL_WINDOWS = (2, 4, 8, 16)
HALO = 16
N_DEV = 8
MESH_AXES = ("x", "y", "c")

ADAM_LR = 0.001
ADAM_B1 = 0.9
ADAM_B2 = 0.999
ADAM_EPS = 1e-08
ADAM_WD = 0.01
ADAM_STEP = 10

ROW_TILE = 256
ATT_TILE = 256
ROW_CHUNK = 32
GAIN_ROWS = 40


def _tile(n, want):
    return want if n % want == 0 else n


def matmul(a, b, *, name, ta=False, tb=False, a_layer=None, b_layer=None, out_dtype=F32,
           a_fn=None, epi=None, epi_in=None, out_scale=None, tm=1024, tn=1024, tk=1024):
    a2 = a.shape[1:] if a_layer is not None else a.shape
    b2 = b.shape[1:] if b_layer is not None else b.shape
    (K, M) = a2 if ta else a2[::-1]
    if not ta:
        M, K = a2
    if tb:
        N, Kb = b2
    else:
        Kb, N = b2
    assert K == Kb, (a.shape, b.shape)
    tm, tn, tk = _tile(M, tm), _tile(N, tn), _tile(K, tk)
    nk = K // tk
    grid = (M // tm, N // tn, nk)

    def lead(layer, shape, imap):
        if layer is None:
            return pl.BlockSpec(shape, imap)
        return pl.BlockSpec((None,) + shape, lambda i, j, k: (layer,) + imap(i, j, k))

    a_spec = lead(a_layer, (tk, tm) if ta else (tm, tk), (lambda i, j, k: (k, i)) if ta else (lambda i, j, k: (i, k)))
    b_spec = lead(b_layer, (tn, tk) if tb else (tk, tn), (lambda i, j, k: (j, k)) if tb else (lambda i, j, k: (k, j)))
    in_specs = [a_spec, b_spec]
    operands = [a, b]
    if epi is not None:
        in_specs.append(pl.BlockSpec((tm, tn), lambda i, j, k: (i, j)))
        operands.append(epi_in)
    out_shape = jax.ShapeDtypeStruct((M, N), out_dtype)
    out_spec = pl.BlockSpec((tm, tn), lambda i, j, k: (i, j))
    dims = (((0 if ta else 1,), (1 if tb else 0,)), ((), ()))
    n_in = len(operands)

    def body(*refs):
        a_ref, b_ref = refs[0], refs[1]
        e_ref = refs[2] if epi is not None else None
        o_ref = refs[n_in]

        def product():
            av = a_ref[...]
            if a_fn is not None:
                av = a_fn(av)
            return lax.dot_general(av.astype(BF16), b_ref[...].astype(BF16), dims, preferred_element_type=F32)

        def finish(r):
            if epi is not None:
                r = epi(r, e_ref[...])
            if out_scale is not None:
                r = r * out_scale
            o_ref[...] = r.astype(out_dtype)

        if nk == 1:
            finish(product())
            return
        acc_ref = refs[n_in + 1]
        k = pl.program_id(2)

        @pl.when(k == 0)
        def _():
            acc_ref[...] = product()

        @pl.when(k > 0)
        def _():
            acc_ref[...] += product()

        @pl.when(k == nk - 1)
        def _():
            finish(acc_ref[...])

    return pl.pallas_call(
        body, name=name, grid=grid, in_specs=in_specs, out_specs=out_spec, out_shape=out_shape,
        scratch_shapes=[pltpu.VMEM((tm, tn), F32)] if nk > 1 else [],
        compiler_params=pltpu.CompilerParams(dimension_semantics=("parallel", "parallel", "arbitrary")),
    )(*operands)


def _relu2(u):
    r = jnp.maximum(u, 0.0)
    return r * r


def _relu2_grad(acc, u):
    return acc * (2.0 * jnp.maximum(u, 0.0))


def rowwise(fn, rows, vecs, out_rows, n_acc, *, name, tile=ROW_TILE):
    S = rows[0].shape[0]
    tile = _tile(S, tile)
    n_rows, n_vecs, n_out = len(rows), len(vecs), len(out_rows)
    acc_cols = [None] * n_acc

    def body(*refs):
        ins = [r[...] for r in refs[:n_rows + n_vecs]]
        outs = refs[n_rows + n_vecs:]
        ro, ac = fn(*ins)
        assert len(ro) == n_out and len(ac) == n_acc
        for r, o in zip(outs[:n_out], ro):
            r[...] = o.astype(r.dtype)
        i = pl.program_id(0)
        for r, a in zip(outs[n_out:], ac):
            @pl.when(i == 0)
            def _():
                r[...] = jnp.zeros_like(r)
            r[...] += a

    acc_shapes = jax.eval_shape(
        lambda *xs: fn(*xs)[1],
        *[jax.ShapeDtypeStruct((tile, r.shape[1]), r.dtype) for r in rows],
        *[jax.ShapeDtypeStruct(v.shape, v.dtype) for v in vecs])
    in_specs = [pl.BlockSpec((tile, r.shape[1]), lambda i: (i, 0)) for r in rows]
    in_specs += [pl.BlockSpec(v.shape, lambda i: (0, 0)) for v in vecs]
    out_specs = [pl.BlockSpec((tile, c), lambda i: (i, 0)) for c, _ in out_rows]
    out_specs += [pl.BlockSpec(a.shape, lambda i: (0, 0)) for a in acc_shapes]
    out_shape = [jax.ShapeDtypeStruct((S, c), dt) for c, dt in out_rows]
    out_shape += [jax.ShapeDtypeStruct(a.shape, F32) for a in acc_shapes]
    del acc_cols
    return pl.pallas_call(
        body, name=name, grid=(S // tile,), in_specs=in_specs, out_specs=out_specs, out_shape=out_shape,
        compiler_params=pltpu.CompilerParams(dimension_semantics=("arbitrary",)),
    )(*rows, *vecs)


def _rms(x, g):
    r = lax.rsqrt(jnp.mean(x * x, axis=-1, keepdims=True) + EPS)
    return x * r * g


def _rms_bwd(x, g, dy):
    r = lax.rsqrt(jnp.mean(x * x, axis=-1, keepdims=True) + EPS)
    xh = x * r
    dyg = dy * g
    dx = r * (dyg - xh * jnp.mean(dyg * xh, axis=-1, keepdims=True))
    dg = jnp.sum(dy * xh, axis=0, keepdims=True)
    return dx, dg


def norm_only(x, g, dtype, *, name):
    D = x.shape[1]
    return rowwise(lambda xv, gv: ([_rms(xv, gv)], []), [x], [g], [(D, dtype)], 0, name=name)[0]


def residual_norms(x, m, g_post, next_gs, next_dtypes, *, name):
    D = x.shape[1]

    def fn(xv, mv, gp, *gs):
        xn = xv + _rms(mv, gp)
        return [xn] + [_rms(xn, g) for g in gs], []

    return rowwise(fn, [x, m], [g_post] + list(next_gs), [(D, F32)] + [(D, dt) for dt in next_dtypes], 0, name=name)


def residual_loss(x, m, g_post, target, *, name):
    D = x.shape[1]

    def fn(xv, mv, tv, gp):
        e = xv + _rms(mv, gp) - tv
        return [e * (1.0 / D)], [jnp.sum(e * e, axis=0, keepdims=True) * (0.5 / D)]

    return rowwise(fn, [x, m, target], [g_post], [(D, F32)], 1, name=name)


def post_norm_bwd(d, g_post, dy, dtype, *, name):
    D = d.shape[1]

    def fn(dv, dyv, gp):
        dd, dg = _rms_bwd(dv, gp, dyv)
        return [dd], [dg]

    return rowwise(fn, [d, dy], [g_post], [(D, dtype)], 1, name=name)


def mid_bwd(dy, x_mid, dh2, m, g_mlp_pre, g_mix_post, dm_dtype, *, name):
    D = dy.shape[1]

    def fn(dyv, xm, dh, mv, gpre, gpost):
        dx, dg_pre = _rms_bwd(xm, gpre, dh)
        dxm = dyv + dx
        dm, dg_post = _rms_bwd(mv, gpost, dxm)
        return [dxm, dm], [dg_pre, dg_post]

    return rowwise(fn, [dy, x_mid, dh2, m], [g_mlp_pre, g_mix_post], [(D, F32), (D, dm_dtype)], 2, name=name)


def pre_norm_bwd(dxm, x, dhs, gs, *, name):
    D = x.shape[1]
    n = len(dhs)

    def fn(dxv, xv, *rest):
        dh, g = rest[:n], rest[n:]
        out, accs = dxv, []
        for k in range(n):
            dx, dg = _rms_bwd(xv, g[k], dh[k])
            out = out + dx
            accs.append(dg)
        return [out], accs

    return rowwise(fn, [dxm, x] + list(dhs), list(gs), [(D, F32)], n, name=name)


def sum_concat_cast(pairs, dtype, *, name):
    C = pairs[0][0].shape[1]
    n = len(pairs)

    def fn(*xs):
        return [jnp.concatenate([sum(xs[:n]), sum(xs[n:])], axis=1)], []

    return rowwise(fn, [a for a, _ in pairs] + [b for _, b in pairs], [], [(2 * C, dtype)], 0, name=name)[0]


def _window_sum(e, window, total_rows, backward):
    s, k = e, 1
    while k < window:
        s = s + pltpu.roll(s, (total_rows - k) if backward else k, 0)
        k *= 2
    return s


def pool_fwd(h, w, scale, *, name):
    S, D = h.shape
    G = len(POOL_WINDOWS)
    GC = D // G
    tile = _tile(S, ROW_TILE)
    hb = tile // HALO

    def body(hc_ref, hp_ref, w_ref, sc_ref, o_ref):
        i = pl.program_id(0)
        prev = jnp.where(i > 0, hp_ref[...], 0.0)
        ext = jnp.concatenate([prev, hc_ref[...]], axis=0)
        t = i * tile + lax.broadcasted_iota(jnp.int32, (tile, 1), 0)
        outs = []
        for g, window in enumerate(POOL_WINDOWS):
            e = ext[:, g * GC:(g + 1) * GC]
            s = _window_sum(e, window, HALO + tile, False)[HALO:, :]
            cnt = jnp.minimum(t + 1, window).astype(F32)
            y = s / cnt - e[HALO:, :]
            outs.append(jnp.dot(y.astype(BF16), w_ref[g], preferred_element_type=F32))
        o_ref[...] = jnp.concatenate(outs, axis=1) * sc_ref[...]

    return pl.pallas_call(
        body, name=name, grid=(S // tile,),
        in_specs=[pl.BlockSpec((tile, D), lambda i: (i, 0)),
                  pl.BlockSpec((HALO, D), lambda i: (jnp.maximum(i * hb - 1, 0), 0)),
                  pl.BlockSpec((G, GC, GC), lambda i: (0, 0, 0)),
                  pl.BlockSpec((1, D), lambda i: (0, 0))],
        out_specs=pl.BlockSpec((tile, D), lambda i: (i, 0)),
        out_shape=jax.ShapeDtypeStruct((S, D), F32),
        compiler_params=pltpu.CompilerParams(dimension_semantics=("parallel",)),
    )(h, h, w, scale)


def pool_bwd(h, dm, w, scale, *, name):
    S, D = h.shape
    G = len(POOL_WINDOWS)
    GC = D // G
    tile = _tile(S, ROW_TILE)
    hb = tile // HALO
    n_tiles = S // tile
    last_halo = S // HALO - 1

    def body(hc_ref, hp_ref, dmc_ref, dmn_ref, w_ref, sc_ref, dh_ref, dw_ref, dsc_ref):
        i = pl.program_id(0)

        @pl.when(i == 0)
        def _():
            dw_ref[...] = jnp.zeros_like(dw_ref)
            dsc_ref[...] = jnp.zeros_like(dsc_ref)

        prev = jnp.where(i > 0, hp_ref[...], 0.0)
        ext = jnp.concatenate([prev, hc_ref[...]], axis=0)
        nxt = jnp.where(i < n_tiles - 1, dmn_ref[...], 0.0)
        dmc = dmc_ref[...]
        dm_ext = jnp.concatenate([dmc, nxt], axis=0)
        t = i * tile + lax.broadcasted_iota(jnp.int32, (tile, 1), 0)
        t_ext = i * tile + lax.broadcasted_iota(jnp.int32, (tile + HALO, 1), 0)
        dhs, dscs = [], []
        for g, window in enumerate(POOL_WINDOWS):
            cols = slice(g * GC, (g + 1) * GC)
            e = ext[:, cols]
            s = _window_sum(e, window, HALO + tile, False)[HALO:, :]
            y = (s / jnp.minimum(t + 1, window).astype(F32) - e[HALO:, :]).astype(BF16)
            wg = w_ref[g]
            ypre = jnp.dot(y, wg, preferred_element_type=F32)
            dscs.append(jnp.sum(dmc[:, cols] * ypre, axis=0, keepdims=True))
            dyp = (dm_ext[:, cols] * sc_ref[:, cols]).astype(BF16)
            dw_ref[g] += lax.dot_general(y, dyp[:tile, :], (((0,), (0,)), ((), ())), preferred_element_type=F32)
            dy = lax.dot_general(dyp, wg, (((1,), (1,)), ((), ())), preferred_element_type=F32)
            r = dy / jnp.minimum(t_ext + 1, window).astype(F32)
            sr = _window_sum(r, window, tile + HALO, True)
            dhs.append(sr[:tile, :] - dy[:tile, :])
        dh_ref[...] = jnp.concatenate(dhs, axis=1)
        dsc_ref[...] += jnp.concatenate(dscs, axis=1)

    return pl.pallas_call(
        body, name=name, grid=(n_tiles,),
        in_specs=[pl.BlockSpec((tile, D), lambda i: (i, 0)),
                  pl.BlockSpec((HALO, D), lambda i: (jnp.maximum(i * hb - 1, 0), 0)),
                  pl.BlockSpec((tile, D), lambda i: (i, 0)),
                  pl.BlockSpec((HALO, D), lambda i: (jnp.minimum((i + 1) * hb, last_halo), 0)),
                  pl.BlockSpec((G, GC, GC), lambda i: (0, 0, 0)),
                  pl.BlockSpec((1, D), lambda i: (0, 0))],
        out_specs=[pl.BlockSpec((tile, D), lambda i: (i, 0)),
                   pl.BlockSpec((G, GC, GC), lambda i: (0, 0, 0)),
                   pl.BlockSpec((1, D), lambda i: (0, 0))],
        out_shape=[jax.ShapeDtypeStruct((S, D), F32), jax.ShapeDtypeStruct((G, GC, GC), F32),
                   jax.ShapeDtypeStruct((1, D), F32)],
        compiler_params=pltpu.CompilerParams(dimension_semantics=("arbitrary",)),
    )(h, h, dm, dm, w, scale)


ATT_LANES = 256
N_PAIR = ATT_LANES // HEAD_DIM


def _scores(xs, kj):
    return [lax.dot_general(x, kj, (((1,), (1,)), ((), ())), preferred_element_type=F32) for x in xs]


def _softplus_parts(z, mask):
    sp = jnp.maximum(z, 0.0) + jnp.log(1.0 + jnp.exp(-jnp.abs(z)))
    logb = z - sp
    if mask is not None:
        sp = jnp.where(mask, sp, 0.0)
    return logb, sp.astype(BF16), jnp.sum(sp, axis=1, keepdims=True)


def _weights(logb, later, c, mask):
    a = jnp.exp(logb - (later + c))
    return a if mask is None else jnp.where(mask, a, 0.0)


def _tile_weights(zss, cs, u_later, mask):
    partss = [[_softplus_parts(z, mask) for z in zs] for zs in zss]
    laterss = [[jnp.dot(sp, u_later, preferred_element_type=F32) for _, sp, _ in parts] for parts in partss]
    out = []
    for parts, laters in zip(partss, laterss):
        out.append(([p[0] for p in parts], [_weights(p[0], later, c, mask) for p, later, c in zip(parts, laters, cs)]))
        cs = [c + p[2] for c, p in zip(cs, parts)]
    return out, tuple(cs)


def _tri(T, later):
    rows = lax.broadcasted_iota(jnp.int32, (T, T), 0)
    cols = lax.broadcasted_iota(jnp.int32, (T, T), 1)
    return jnp.where((rows > cols) if later else (rows < cols), 1.0, 0.0).astype(BF16)


def _head_masks(x2, axis=None):
    lane = lax.broadcasted_iota(jnp.int32, (1, ATT_LANES), 1)
    parts = [jnp.where((lane // HEAD_DIM) == hh, x2, jnp.zeros_like(x2)) for hh in range(N_PAIR)]
    return parts if axis is None else jnp.concatenate(parts, axis=axis)


def _cat_bf16(parts, axis):
    return jnp.concatenate([p.astype(BF16) for p in parts], axis=axis)


def _side_split(side, refs, n_in, n_out):
    if side is None:
        return refs, None
    n_src, n_buf = len(side.srcs), len(side.bufs)
    ins, rest = refs[:n_in], refs[n_in:]
    src_refs, rest = rest[:n_src], rest[n_src + n_buf:]
    outs, rest = rest[:n_out], rest[n_out:]
    buf_refs, rest = rest[:n_buf], rest[n_buf:]
    own_scratch, sems = rest[:len(rest) - 3], rest[len(rest) - 3:]
    return tuple(ins) + tuple(outs) + tuple(own_scratch), (src_refs, buf_refs, sems)


def _side_phase(side, side_refs, phase, when):
    if side is None:
        return

    @pl.when(when)
    def _():
        side.copies(phase, *side_refs)


def attn_fwd(q, kv, *, name, side=None):
    S, D = q.shape
    P = D // ATT_LANES
    T = _tile(S, ATT_TILE)
    nq = S // T

    def body(*refs):
        (q_ref, k_ref, v_ref, o_ref), side_refs = _side_split(side, refs, 3, 1)
        i = pl.program_id(1)
        p = pl.program_id(0)
        _side_phase(side, side_refs, 0, (p == 0) & (i == 0))
        u_later = _tri(T, True)
        qhs = _head_masks(q_ref[...])
        diag = lax.broadcasted_iota(jnp.int32, (T, T), 1) < lax.broadcasted_iota(jnp.int32, (T, T), 0)

        def rows_of(ref, j):
            return ref[pl.ds(pl.multiple_of(j * T, T), T), :]

        def step(js, carry, mask):
            cs, acc = carry
            zss = [_scores(qhs, rows_of(k_ref, j)) for j in js]
            vcat = jnp.concatenate([_head_masks(rows_of(v_ref, j), axis=0) for j in js], axis=0)
            per_tile, cs = _tile_weights(zss, cs, u_later, mask)
            acat = _cat_bf16([a for _, aa in per_tile for a in aa], 1)
            return cs, acc + jnp.dot(acat, vcat, preferred_element_type=F32)

        carry = step([i], ((jnp.zeros((T, 1), F32),) * N_PAIR, jnp.zeros((T, ATT_LANES), F32)), diag)
        carry = lax.fori_loop(0, i % 2, lambda n, cr: step([i - 1], cr, None), carry)
        first = i - 1 - i % 2
        _, acc = lax.fori_loop(0, i // 2, lambda n, cr: step([first - 2 * n, first - 2 * n - 1], cr, None), carry)
        o_ref[...] = acc.astype(o_ref.dtype)
        _side_phase(side, side_refs, 1, (p == P - 1) & (i == nq - 1))

    sd = side
    outs = pl.pallas_call(
        body, name=name, grid=(P, nq),
        in_specs=[pl.BlockSpec((T, ATT_LANES), lambda p, i: (i, p)),
                  pl.BlockSpec((S, ATT_LANES), lambda p, i: (0, p)),
                  pl.BlockSpec((S, ATT_LANES), lambda p, i: (0, P + p))] + (sd.specs() if sd else []),
        out_specs=[pl.BlockSpec((T, ATT_LANES), lambda p, i: (i, p))] + (sd.out_specs() if sd else []),
        out_shape=[jax.ShapeDtypeStruct((S, D), BF16)] + (sd.out_shape() if sd else []),
        scratch_shapes=sd.scratch() if sd else [],
        input_output_aliases=sd.aliases(3, 1) if sd else {},
        compiler_params=pltpu.CompilerParams(dimension_semantics=("arbitrary", "arbitrary"),
                                             has_side_effects=sd is not None),
    )(q, kv, kv, *(sd.operands() if sd else []))
    return outs[0] if sd is None else (outs[0], sd.result(outs[1:]))


def attn_bwd(q, kv, do, *, name, side=None):
    S, D = q.shape
    P = D // ATT_LANES
    T = _tile(S, ATT_TILE)
    nb = S // T

    def body(*refs):
        (q_ref, k_ref, v_ref, do_ref, dq_ref, dk_ref, dv_ref, g_scr, s_scr), side_refs = _side_split(side, refs, 4, 3)
        i = pl.program_id(1)
        p = pl.program_id(0)
        _side_phase(side, side_refs, 0, (p == 0) & (i == 0))

        @pl.when(i == 0)
        def _():
            dk_ref[...] = jnp.zeros_like(dk_ref)
            dv_ref[...] = jnp.zeros_like(dv_ref)

        u_later = _tri(T, True)
        u_earlier = _tri(T, False)
        qhs = _head_masks(q_ref[...])
        dohs = _head_masks(do_ref[...])
        qcat = jnp.concatenate(qhs, axis=0)
        docat = jnp.concatenate(dohs, axis=0)
        diag = lax.broadcasted_iota(jnp.int32, (T, T), 1) < lax.broadcasted_iota(jnp.int32, (T, T), 0)
        tdot = (((0,), (0,)), ((), ()))

        def rows_of(ref, j):
            return ref.at[pl.ds(pl.multiple_of(j * T, T), T), :]

        def step1(js, cs, mask):
            dass = [_scores(dohs, rows_of(v_ref, j)[...]) for j in js]
            zss = [_scores(qhs, rows_of(k_ref, j)[...]) for j in js]
            per_tile, cs = _tile_weights(zss, cs, u_later, mask)
            for j, das, (logbs, aa) in zip(js, dass, per_tile):
                for hh in range(N_PAIR):
                    g_scr[hh, j] = (das[hh] * aa[hh]).astype(BF16)
                    sg = jnp.exp(logbs[hh])
                    if mask is not None:
                        sg = jnp.where(mask, sg, 0.0)
                    s_scr[hh, j] = sg.astype(BF16)
            for j, (_, aa) in zip(js, per_tile):
                rows_of(dv_ref, j)[...] += lax.dot_general(_cat_bf16(aa, 0), docat, tdot,
                                                          preferred_element_type=F32)
            return cs

        cs = step1([i], (jnp.zeros((T, 1), F32),) * N_PAIR, diag)
        cs = lax.fori_loop(0, i % 2, lambda n, c: step1([i - 1], c, None), cs)
        first = i - 1 - i % 2
        lax.fori_loop(0, i // 2, lambda n, c: step1([first - 2 * n, first - 2 * n - 1], c, None), cs)

        def step2(js, carry):
            cs, acc = carry
            cumss = [[jnp.dot(g_scr[hh, j], u_earlier, preferred_element_type=F32) for hh in range(N_PAIR)]
                     for j in js]
            kcat = jnp.concatenate([_head_masks(rows_of(k_ref, j)[...], axis=0) for j in js], axis=0)
            dzss = []
            for j, cums in zip(js, cumss):
                dzs, new = [], []
                for hh in range(N_PAIR):
                    gf = g_scr[hh, j].astype(F32)
                    sg = s_scr[hh, j].astype(F32)
                    dzs.append((gf - sg * (gf + (cums[hh] + cs[hh]))).astype(BF16))
                    new.append(cs[hh] + jnp.sum(gf, axis=1, keepdims=True))
                cs = tuple(new)
                dzss.append(dzs)
            acc = acc + jnp.dot(jnp.concatenate([dz for dzs in dzss for dz in dzs], axis=1), kcat,
                                preferred_element_type=F32)
            for j, dzs in zip(js, dzss):
                rows_of(dk_ref, j)[...] += lax.dot_general(jnp.concatenate(dzs, axis=0), qcat, tdot,
                                                          preferred_element_type=F32)
            return cs, acc

        carry = ((jnp.zeros((T, 1), F32),) * N_PAIR, jnp.zeros((T, ATT_LANES), F32))
        carry = lax.fori_loop(0, (i + 1) // 2, lambda n, cr: step2([2 * n, 2 * n + 1], cr), carry)
        _, dq = lax.fori_loop(0, (i + 1) % 2, lambda n, cr: step2([i], cr), carry)
        dq_ref[...] = (dq * (HEAD_DIM ** -0.5)).astype(dq_ref.dtype)
        _side_phase(side, side_refs, 1, (p == P - 1) & (i == nb - 1))

    sd = side
    outs = pl.pallas_call(
        body, name=name, grid=(P, nb),
        in_specs=[pl.BlockSpec((T, ATT_LANES), lambda p, i: (i, p)),
                  pl.BlockSpec((S, ATT_LANES), lambda p, i: (0, p)),
                  pl.BlockSpec((S, ATT_LANES), lambda p, i: (0, P + p)),
                  pl.BlockSpec((T, ATT_LANES), lambda p, i: (i, p))] + (sd.specs() if sd else []),
        out_specs=[pl.BlockSpec((T, ATT_LANES), lambda p, i: (i, p)),
                   pl.BlockSpec((S, ATT_LANES), lambda p, i: (0, p)),
                   pl.BlockSpec((S, ATT_LANES), lambda p, i: (0, p))] + (sd.out_specs() if sd else []),
        out_shape=[jax.ShapeDtypeStruct((S, D), BF16), jax.ShapeDtypeStruct((S, D), F32),
                   jax.ShapeDtypeStruct((S, D), F32)] + (sd.out_shape() if sd else []),
        scratch_shapes=[pltpu.VMEM((N_PAIR, nb, T, T), BF16), pltpu.VMEM((N_PAIR, nb, T, T), BF16)]
        + (sd.scratch() if sd else []),
        input_output_aliases=sd.aliases(4, 3) if sd else {},
        compiler_params=pltpu.CompilerParams(dimension_semantics=("arbitrary", "arbitrary"),
                                             has_side_effects=sd is not None),
    )(q, kv, kv, do, *(sd.operands() if sd else []))
    return tuple(outs[:3]) if sd is None else (*outs[:3], sd.result(outs[3:]))


def _window(ref, axis, dev, n):
    return ref.at[(slice(None),) * axis + (pl.ds(dev * n, n),)]


def all_gather(ops, *, name):
    n_ops = len(ops)
    out_shape = []
    for a, ax in ops:
        shp = list(a.shape)
        shp[ax] *= N_DEV
        out_shape.append(jax.ShapeDtypeStruct(tuple(shp), a.dtype))

    def body(*refs):
        ins, outs = refs[:n_ops], refs[n_ops:2 * n_ops]
        send_sems, recv_sems, local_sems = refs[2 * n_ops:]
        x, y, c = (lax.axis_index(n) for n in MESH_AXES)
        me, sibling = (x, y, c), (x, y, 1 - c)
        chips = [(1 - x, y), (x, 1 - y), (1 - x, 1 - y)]

        def rows(o, dev):
            px, py, pc = dev
            ax = ops[o][1]
            return _window(outs[o], ax, 4 * px + 2 * py + pc, ops[o][0].shape[ax])

        def copy(o, k, block, to, src=None):
            return pltpu.make_async_remote_copy(
                src_ref=rows(o, block) if src is None else src, dst_ref=rows(o, block),
                send_sem=send_sems.at[o, k], recv_sem=recv_sems.at[o, k],
                device_id=to, device_id_type=pl.DeviceIdType.MESH)

        mine, first, passed = [], [], []
        for o in range(n_ops):
            cp = pltpu.make_async_copy(ins[o], rows(o, me), local_sems.at[o])
            cp.start()
            mine.append(cp)
            first.append(copy(o, 0, me, sibling, src=ins[o]))
            first += [copy(o, 1 + j, me, (*chip, c), src=ins[o]) for j, chip in enumerate(chips)]
        for cp in first:
            cp.start()
        for j, chip in enumerate(chips):
            for o in range(n_ops):
                copy(o, 1 + j, (*chip, c), me).wait_recv()
                cp = copy(o, 4 + j, (*chip, c), sibling)
                cp.start()
                passed.append(cp)
        for o in range(n_ops):
            copy(o, 0, sibling, me).wait_recv()
            for j, chip in enumerate(chips):
                copy(o, 4 + j, (*chip, 1 - c), me).wait_recv()
        for cp in first + passed:
            cp.wait_send()
        for cp in mine:
            cp.wait()

    any_spec = pl.BlockSpec(memory_space=pl.ANY)
    return pl.pallas_call(
        body, name=name, in_specs=[any_spec] * n_ops, out_specs=[any_spec] * n_ops, out_shape=out_shape,
        scratch_shapes=[pltpu.SemaphoreType.DMA((n_ops, 7)), pltpu.SemaphoreType.DMA((n_ops, 7)),
                        pltpu.SemaphoreType.DMA((n_ops,))],
        compiler_params=pltpu.CompilerParams(has_side_effects=True),
    )(*[a for a, _ in ops])


class Side:
    def __init__(self, kind, items, bufs):
        self.kind, self.items = kind, items
        self.names = list(bufs)
        self.bufs = [bufs[n] for n in self.names]
        self.srcs = [it[0] for it in items]

    def operands(self):
        return self.srcs + self.bufs

    def specs(self):
        return [pl.BlockSpec(memory_space=pl.ANY)] * (len(self.srcs) + len(self.bufs))

    def out_specs(self):
        return [pl.BlockSpec(memory_space=pl.ANY)] * len(self.bufs)

    def out_shape(self):
        return [jax.ShapeDtypeStruct(b.shape, b.dtype) for b in self.bufs]

    def aliases(self, first_in, first_out):
        return {first_in + len(self.srcs) + k: first_out + k for k in range(len(self.bufs))}

    def scratch(self):
        n = len(self.items)
        return [pltpu.SemaphoreType.DMA((n, N_DEV - 1)), pltpu.SemaphoreType.DMA((n, N_DEV - 1)),
                pltpu.SemaphoreType.DMA((n,))]

    def result(self, outs):
        return dict(zip(self.names, outs))

    def copies(self, phase, src_refs, buf_refs, sems):
        send_sems, recv_sems, local_sems = sems
        pos = tuple(lax.axis_index(n) for n in MESH_AXES)
        me = 4 * pos[0] + 2 * pos[1] + pos[2]
        for o, (_, ax, name, layer) in enumerate(self.items):
            src, buf = src_refs[o], buf_refs[self.names.index(name)]
            if self.kind == "gather":
                whole = buf if layer is None else buf.at[layer]
                n = src.shape[ax]
                sent = lambda dev, src=src: src
                lands = lambda dev, whole=whole, ax=ax, n=n: _window(whole, ax, dev, n)
            else:
                n = src.shape[ax] // N_DEV
                sent = lambda dev, src=src, ax=ax, n=n: _window(src, ax, dev, n)
                lands = lambda dev, buf=buf, layer=layer: buf.at[dev] if layer is None else buf.at[dev, layer]
            if local_sems is not None:
                local = pltpu.make_async_copy(sent(me), lands(me), local_sems.at[o])
                if phase == 0:
                    local.start()
                else:
                    local.wait()
            for r in range(1, N_DEV):
                peer = tuple(1 - p if r & bit else p for p, bit in zip(pos, (4, 2, 1)))
                pid = 4 * peer[0] + 2 * peer[1] + peer[2]
                cp = pltpu.make_async_remote_copy(
                    src_ref=sent(pid) if phase == 0 else sent(me), dst_ref=lands(me) if phase == 0 else lands(pid),
                    send_sem=send_sems[o] if isinstance(send_sems, (list, tuple)) else send_sems.at[o, r - 1],
                    recv_sem=recv_sems[o] if isinstance(recv_sems, (list, tuple)) else recv_sems.at[o, r - 1],
                    device_id=peer, device_id_type=pl.DeviceIdType.MESH)
                if phase == 0:
                    cp.start()
                else:
                    cp.wait_recv()
                    cp.wait_send()


def run_side(side, *, name):
    n_src, n_buf = len(side.srcs), len(side.bufs)

    def body(*refs):
        src_refs, buf_refs = refs[:n_src], refs[n_src + n_buf:n_src + 2 * n_buf]
        sems = refs[n_src + 2 * n_buf:]
        side.copies(0, src_refs, buf_refs, sems)
        side.copies(1, src_refs, buf_refs, sems)

    outs = pl.pallas_call(
        body, name=name, in_specs=side.specs(), out_specs=side.out_specs(), out_shape=side.out_shape(),
        scratch_shapes=side.scratch(), input_output_aliases=side.aliases(0, 0),
        compiler_params=pltpu.CompilerParams(has_side_effects=True),
    )(*side.operands())
    return side.result(outs)


def _hbm(a):
    return pltpu.with_memory_space_constraint(a, pltpu.HBM)


def own_blocks(items, bufs):
    me = 4 * lax.axis_index("x") + 2 * lax.axis_index("y") + lax.axis_index("c")
    bufs = {name: bufs[name] for _, _, name, _ in items}
    for src, ax, name, layer in items:
        n = src.shape[ax] // N_DEV
        blk = lax.dynamic_slice_in_dim(src, me * n, n, axis=ax)
        lead = (me,) if layer is None else (me, layer)
        blk = blk.reshape((1,) * len(lead) + blk.shape)
        bufs[name] = lax.dynamic_update_slice(bufs[name], blk, lead + (0,) * src.ndim)
    return bufs


def start_side(side, *, name):
    n_src, n_buf = len(side.srcs), len(side.bufs)
    n = len(side.items)

    def body(*refs):
        src_refs, buf_refs = refs[:n_src], refs[n_src:n_src + n_buf]
        sems = refs[n_src + n_buf:n_src + n_buf + 2 * n]
        token = refs[-1]
        side.copies(0, src_refs, buf_refs, (list(sems[:n]), list(sems[n:]), None))
        token[...] = jnp.zeros_like(token)

    hbm = pl.BlockSpec(memory_space=pltpu.HBM)
    sem = pl.BlockSpec(memory_space=pltpu.SEMAPHORE)
    operands = side.operands()
    outs = pl.pallas_call(
        body, name=name,
        out_shape=(*[pltpu.SemaphoreType.DMA(())] * (2 * n),
                   *[pltpu.HBM(a.shape, a.dtype) for a in operands], jax.ShapeDtypeStruct((8, LANES), F32)),
        in_specs=[hbm] * len(operands),
        out_specs=(*[sem] * (2 * n), *[hbm] * len(operands), pl.BlockSpec(memory_space=pltpu.VMEM)),
        input_output_aliases={k: 2 * n + k for k in range(len(operands))},
        compiler_params=pltpu.CompilerParams(has_side_effects=pltpu.SideEffectType.DATAFLOW_SIDE_EFFECTING),
    )(*[_hbm(a) for a in operands])
    started = (side, list(outs[:n]), list(outs[n:2 * n]), list(outs[2 * n:2 * n + n_src]))
    return started, dict(zip(side.names, outs[2 * n + n_src:2 * n + n_src + n_buf])), outs[-1]


def wait_sides(started, bufs, after, *, name):
    names = list(bufs)
    flat = []
    for side, send_sems, recv_sems, srcs in started:
        flat += [*srcs, *send_sems, *recv_sems]
    n_buf = len(names)

    def body(*refs):
        buf_refs = refs[:n_buf]
        rest = refs[n_buf:]
        for side, _, _, srcs in started:
            n = len(srcs)
            src_refs, send_sems, recv_sems = rest[:n], list(rest[n:2 * n]), list(rest[2 * n:3 * n])
            rest = rest[3 * n:]
            side.copies(1, src_refs, [buf_refs[names.index(nm)] for nm in side.names], (send_sems, recv_sems, None))

    hbm = pl.BlockSpec(memory_space=pltpu.HBM)
    sem = pl.BlockSpec(memory_space=pltpu.SEMAPHORE)
    specs = []
    for _, _, _, srcs in started:
        specs += [hbm] * len(srcs) + [sem] * (2 * len(srcs))
    outs = pl.pallas_call(
        body, name=name,
        out_shape=tuple(pltpu.HBM(bufs[nm].shape, bufs[nm].dtype) for nm in names),
        in_specs=[hbm] * n_buf + specs + [pl.BlockSpec(memory_space=pl.ANY)],
        out_specs=tuple([hbm] * n_buf),
        input_output_aliases={k: k for k in range(n_buf)},
        compiler_params=pltpu.CompilerParams(has_side_effects=pltpu.SideEffectType.DATAFLOW_SIDE_EFFECTING),
    )(*[bufs[nm] for nm in names], *flat, after)
    return dict(zip(names, outs))


def _adamw(w, g, m, v):
    m = ADAM_B1 * m + (1.0 - ADAM_B1) * g
    v = ADAM_B2 * v + (1.0 - ADAM_B2) * (g * g)
    m_hat = m / (1.0 - ADAM_B1 ** ADAM_STEP)
    v_hat = v / (1.0 - ADAM_B2 ** ADAM_STEP)
    delta = -ADAM_LR * (m_hat / (jnp.sqrt(v_hat) + ADAM_EPS) + ADAM_WD * w)
    return delta, m, v


def reduce_adamw(parts, w, m, v, *, name):
    shape = w.shape
    C = shape[-1]
    R = w.size // C
    tile = _tile(R, 256)

    def body(p_ref, w_ref, m_ref, v_ref, g_ref, d_ref, nm_ref, nv_ref):
        g = p_ref[0].astype(F32)
        for s in range(1, N_DEV):
            g = g + p_ref[s].astype(F32)
        d, nm, nv = _adamw(w_ref[...], g, m_ref[...], v_ref[...])
        g_ref[...] = g
        d_ref[...] = d
        nm_ref[...] = nm
        nv_ref[...] = nv

    row = pl.BlockSpec((tile, C), lambda i: (i, 0))
    outs = pl.pallas_call(
        body, name=name, grid=(R // tile,),
        in_specs=[pl.BlockSpec((N_DEV, tile, C), lambda i: (0, i, 0)), row, row, row],
        out_specs=[row] * 4, out_shape=[jax.ShapeDtypeStruct((R, C), F32)] * 4,
        compiler_params=pltpu.CompilerParams(dimension_semantics=("parallel",)),
    )(parts.reshape(N_DEV, R, C), w.reshape(R, C), m.reshape(R, C), v.reshape(R, C))
    return [o.reshape(shape) for o in outs]


def gains_adamw(parts, groups, *, name):
    n = len(groups)

    def body(*refs):
        p_ref = refs[0]
        ins, outs = refs[1:1 + 3 * n], refs[1 + 3 * n:]
        for k in range(n):
            w_ref, m_ref, v_ref = ins[3 * k:3 * k + 3]
            L = w_ref.shape[0]
            g = p_ref[pl.ds(8 * k, L), :]
            for s in range(1, N_DEV):
                g = g + p_ref[pl.ds(s * GAIN_ROWS + 8 * k, L), :]
            d, nm, nv = _adamw(w_ref[...], g, m_ref[...], v_ref[...])
            for r, val in zip(outs[4 * k:4 * k + 4], (g, d, nm, nv)):
                r[...] = val

    flat = [a for grp in groups for a in grp]
    out_shape = [jax.ShapeDtypeStruct(grp[0].shape, F32) for grp in groups for _ in range(4)]
    outs = pl.pallas_call(body, name=name, out_shape=out_shape)(parts, *flat)
    return [outs[4 * k:4 * k + 4] for k in range(n)]


def kernel(x, pool_w, pool_scale, w_q, w_kv, kv_norm_g, w_o, w_up, w_down, mix_pre_g, mix_post_g, mlp_pre_g, mlp_post_g, loss_target, m_pool_w, m_pool_scale, m_w_q, m_w_kv, m_kv_norm_g, m_w_o, m_w_up, m_w_down, m_mix_pre_g, m_mix_post_g, m_mlp_pre_g, m_mlp_post_g, v_pool_w, v_pool_scale, v_w_q, v_w_kv, v_kv_norm_g, v_w_o, v_w_up, v_w_down, v_mix_pre_g, v_mix_post_g, v_mlp_pre_g, v_mlp_post_g):
    _, S, D = x.shape
    x0 = x.reshape(S, D)
    target = loss_target.reshape(S, D)
    depth = w_up.shape[0]
    n_pool = pool_w.shape[0]
    F = w_up.shape[2] * N_DEV
    G = pool_w.shape[1]
    GC = D // G

    def vec(a, l):
        return a[l].reshape(1, D)

    n_att = depth - n_pool
    wq_s, wkv_s, wo_s = w_q.astype(BF16), w_kv.astype(BF16), w_o.astype(BF16)
    wup_s, wdn_s, pw_s = w_up.astype(BF16), w_down.astype(BF16), pool_w.astype(BF16)
    wq0, wkv, wup_a, wdn_a, pw, psc = all_gather(
        [(wq_s[0], 0), (wkv_s, 1), (wup_s[:n_pool], 2), (wdn_s[:n_pool], 1), (pw_s, 2), (pool_scale, 1)],
        name="gather_weights")
    late = Side("gather",
                [(wq_s[1:], 1, "wq", None), (wo_s, 1, "wo", None), (wup_s[n_pool:], 2, "wup", None),
                 (wdn_s[n_pool:], 1, "wdn", None)],
                {"wq": lax.empty((n_att - 1, D, D), BF16), "wo": lax.empty((n_att, D, D), BF16),
                 "wup": lax.empty((n_att, D, F), BF16), "wdn": lax.empty((n_att, F, D), BF16)})
    late_w = None

    def layer_of(early, key, l, n_early):
        return (early, l) if l < n_early else (late_w[key], l - n_early)

    saved = []
    xs = x0
    h1 = norm_only(xs, vec(mix_pre_g, 0), F32, name="norm_in")
    kv = hk = None
    dy = loss_rows = None
    for l in range(depth):
        is_pool = l < n_pool
        st = {"x": xs, "h1": h1}
        if is_pool:
            m = pool_fwd(h1, pw[l], psc[l].reshape(1, D), name=f"pool_fwd{l}")
        else:
            j = l - n_pool
            wq_j, wq_l = (wq0, None) if j == 0 else (late_w["wq"], j - 1)
            q = matmul(h1, wq_j, b_layer=wq_l, out_dtype=BF16, out_scale=HEAD_DIM ** -0.5, name=f"q_proj{j}")
            if j == 0:
                o, late_w = attn_fwd(q, kv, name=f"attn_fwd{j}", side=late)
            else:
                o = attn_fwd(q, kv, name=f"attn_fwd{j}")
            m = matmul(o, late_w["wo"], b_layer=j, name=f"o_proj{j}")
            st.update(q=q, o=o)
        x_mid, h2 = residual_norms(xs, m, vec(mix_post_g, l), [vec(mlp_pre_g, l)], [BF16], name=f"mix_out{l}")
        wup, lu = layer_of(wup_a, "wup", l, n_pool)
        wdn, ld = layer_of(wdn_a, "wdn", l, n_pool)
        u = matmul(h2, wup, b_layer=lu, name=f"mlp_up{l}")
        d = matmul(u, wdn, b_layer=ld, a_fn=_relu2, tk=512, name=f"mlp_down{l}")
        st.update(m=m, x_mid=x_mid, h2=h2, u=u, d=d)
        saved.append(st)
        if l == depth - 1:
            dy, loss_rows = residual_loss(x_mid, d, vec(mlp_post_g, l), target, name="loss")
        elif l == n_pool - 1:
            xs, h1, hk = residual_norms(x_mid, d, vec(mlp_post_g, l), [vec(mix_pre_g, l + 1), kv_norm_g.reshape(1, D)],
                                        [BF16, BF16], name=f"mlp_out{l}")
            kv = matmul(hk, wkv, out_dtype=BF16, name="kv_proj")
        else:
            nxt_dt = F32 if l + 1 < n_pool else BF16
            xs, h1 = residual_norms(x_mid, d, vec(mlp_post_g, l), [vec(mix_pre_g, l + 1)], [nxt_dt], name=f"mlp_out{l}")
    loss = lax.psum(jnp.sum(loss_rows), MESH_AXES)

    recv = {"w_q": (w_q, BF16), "w_kv": (w_kv, BF16), "w_o": (w_o, BF16), "w_up": (w_up, BF16),
            "w_down": (w_down, BF16), "pool_w": (pool_w, F32), "pool_scale": (pool_scale, F32)}
    recv = {n: lax.empty((N_DEV,) + w.shape, dt) for n, (w, dt) in recv.items()}
    pending = []

    def send_pending():
        return Side("scatter", list(pending), {n: recv[n] for n in dict.fromkeys(it[2] for it in pending)})

    started = []

    def start_pending(tag, then):
        recv.update(own_blocks(pending, recv))
        begun, got, token = start_side(send_pending(), name=f"exchange_start_{tag}")
        recv.update(got)
        started.append(begun)
        pending.clear()
        return lax.optimization_barrier((then, token))[0]

    g_pw, g_psc = [None] * n_pool, [None] * n_pool
    gains = {k: [None] * depth for k in ("mix_pre", "mix_post", "mlp_pre", "mlp_post")}
    dkvs = []
    g_kvn = None
    for l in reversed(range(depth)):
        st = saved[l]
        is_pool = l < n_pool
        wup, lu = layer_of(wup_a, "wup", l, n_pool)
        wdn, ld = layer_of(wdn_a, "wdn", l, n_pool)
        dd, gains["mlp_post"][l] = post_norm_bwd(st["d"], vec(mlp_post_g, l), dy, BF16, name=f"mlp_out_bwd{l}")
        du = matmul(dd, wdn, b_layer=ld, tb=True, out_dtype=BF16, epi=_relu2_grad, epi_in=st["u"], name=f"mlp_du{l}")
        g_wdn = matmul(st["u"], dd, ta=True, a_fn=_relu2, tk=512, out_dtype=BF16, name=f"mlp_dwdn{l}")
        g_wup = matmul(st["h2"], du, ta=True, out_dtype=BF16, name=f"mlp_dwup{l}")
        pending += [(g_wup, 1, "w_up", l), (g_wdn, 0, "w_down", l)]
        dh2 = matmul(du, wup, b_layer=lu, tb=True, name=f"mlp_dh{l}")
        if is_pool:
            dh2 = start_pending(f"l{l}", dh2)
        dxm, dm, gains["mlp_pre"][l], gains["mix_post"][l] = mid_bwd(
            dy, st["x_mid"], dh2, st["m"], vec(mlp_pre_g, l), vec(mix_post_g, l), F32 if is_pool else BF16,
            name=f"mid_bwd{l}")
        if is_pool:
            dh1, g_pw[l], g_psc[l] = pool_bwd(st["h1"], dm, pw[l], psc[l].reshape(1, D), name=f"pool_bwd{l}")
        else:
            j = l - n_pool
            wq_j, wq_l = (wq0, None) if j == 0 else (late_w["wq"], j - 1)
            do = matmul(dm, late_w["wo"], b_layer=j, tb=True, out_dtype=BF16, name=f"o_proj_dx{j}")
            g_wo = matmul(st["o"], dm, ta=True, out_dtype=BF16, name=f"o_proj_dw{j}")
            pending.append((g_wo, 0, "w_o", j))
            dq, dk, dv, got = attn_bwd(st["q"], kv, do, name=f"attn_bwd{j}", side=send_pending())
            recv.update(got)
            pending.clear()
            dkvs.append((dk, dv))
            g_wq = matmul(st["h1"], dq, ta=True, out_dtype=BF16, name=f"q_proj_dw{j}")
            pending.append((g_wq, 0, "w_q", j))
            dh1 = matmul(dq, wq_j, b_layer=wq_l, tb=True, name=f"q_proj_dx{j}")
        if l == n_pool:
            dkv = sum_concat_cast(dkvs, BF16, name="dkv_pack")
            g_wkv = matmul(hk, dkv, ta=True, out_dtype=BF16, name="kv_proj_dw")
            pending.append((g_wkv, 1, "w_kv", None))
            dhk = matmul(dkv, wkv, tb=True, name="kv_proj_dx")
            dy, gains["mix_pre"][l], g_kvn = pre_norm_bwd(
                dxm, st["x"], [dh1, dhk], [vec(mix_pre_g, l), kv_norm_g.reshape(1, D)], name=f"mix_in_bwd{l}")
        else:
            dy, gains["mix_pre"][l] = pre_norm_bwd(dxm, st["x"], [dh1], [vec(mix_pre_g, l)], name=f"mix_in_bwd{l}")
    grad_x = dy.reshape(x.shape)

    g_pool_w = jnp.stack(g_pw)
    g_pool_scale = jnp.concatenate(g_psc, axis=0)
    zero_rows = jnp.zeros((8 - depth, D), F32)
    gain_rows = []
    for k in ("mix_pre", "mix_post", "mlp_pre", "mlp_post"):
        gain_rows += gains[k] + [zero_rows]
    gain_rows += [g_kvn, jnp.zeros((7, D), F32)]
    gain_pack = jnp.concatenate(gain_rows, axis=0)
    gain_parts = all_gather([(gain_pack, 0)], name="gather_gain_grads")[0]
    pending += [(g_pool_w, 2, "pool_w", None), (g_pool_scale, 1, "pool_scale", None)]
    dy = start_pending("pool", dy)
    sent = dict.fromkeys(n for side, _, _, _ in started for n in side.names)
    recv.update(wait_sides(started, {n: recv[n] for n in sent}, dy, name="exchange_wait"))
    big = {"w_q": (w_q, m_w_q, v_w_q), "w_kv": (w_kv, m_w_kv, v_w_kv), "w_o": (w_o, m_w_o, v_w_o),
           "w_up": (w_up, m_w_up, v_w_up), "w_down": (w_down, m_w_down, v_w_down),
           "pool_w": (pool_w, m_pool_w, v_pool_w), "pool_scale": (pool_scale, m_pool_scale, v_pool_scale)}
    res = {n: reduce_adamw(recv[n], *wmv, name=f"adamw_{n}") for n, wmv in big.items()}
    gain_groups = [(mix_pre_g, m_mix_pre_g, v_mix_pre_g), (mix_post_g, m_mix_post_g, v_mix_post_g),
                   (mlp_pre_g, m_mlp_pre_g, v_mlp_pre_g), (mlp_post_g, m_mlp_post_g, v_mlp_post_g),
                   (kv_norm_g.reshape(1, D), m_kv_norm_g.reshape(1, D), v_kv_norm_g.reshape(1, D))]
    gres = gains_adamw(gain_parts, gain_groups, name="adamw_gains")
    for n, r in zip(["mix_pre_g", "mix_post_g", "mlp_pre_g", "mlp_post_g"], gres[:4]):
        res[n] = r
    res["kv_norm_g"] = [a.reshape(D) for a in gres[4]]

    order = ["pool_w", "pool_scale", "w_q", "w_kv", "kv_norm_g", "w_o", "w_up", "w_down",
             "mix_pre_g", "mix_post_g", "mlp_pre_g", "mlp_post_g"]
    out = [loss, grad_x]
    for k in range(4):
        out += [res[n][k] for n in order]
    return tuple(out)
```

```python
import functools

import jax
import jax.numpy as jnp
from jax import lax
from jax.experimental import pallas as pl
from jax.experimental.pallas import tpu as pltpu

F32 = jnp.float32
BF16 = jnp.bfloat16

EPS = 1e-6
HEAD_DIM = 64
LANES = 128
POOL_WINDOWS = (2, 4, 8, 16)
HALO = 16
N_DEV = 8
MESH_AXES = ("x", "y", "c")

ADAM_LR = 0.001
ADAM_B1 = 0.9
ADAM_B2 = 0.999
ADAM_EPS = 1e-08
ADAM_WD = 0.01
ADAM_STEP = 10

ROW_TILE = 256
ATT_TILE = 256
ROW_CHUNK = 32
GAIN_ROWS = 40


def _tile(n, want):
    return want if n % want == 0 else n


def matmul(a, b, *, name, ta=False, tb=False, a_layer=None, b_layer=None, out_dtype=F32,
           a_fn=None, epi=None, epi_in=None, out_scale=None, tm=1024, tn=1024, tk=1024):
    a2 = a.shape[1:] if a_layer is not None else a.shape
    b2 = b.shape[1:] if b_layer is not None else b.shape
    (K, M) = a2 if ta else a2[::-1]
    if not ta:
        M, K = a2
    if tb:
        N, Kb = b2
    else:
        Kb, N = b2
    assert K == Kb, (a.shape, b.shape)
    tm, tn, tk = _tile(M, tm), _tile(N, tn), _tile(K, tk)
    nk = K // tk
    grid = (M // tm, N // tn, nk)

    def lead(layer, shape, imap):
        if layer is None:
            return pl.BlockSpec(shape, imap)
        return pl.BlockSpec((None,) + shape, lambda i, j, k: (layer,) + imap(i, j, k))

    a_spec = lead(a_layer, (tk, tm) if ta else (tm, tk), (lambda i, j, k: (k, i)) if ta else (lambda i, j, k: (i, k)))
    b_spec = lead(b_layer, (tn, tk) if tb else (tk, tn), (lambda i, j, k: (j, k)) if tb else (lambda i, j, k: (k, j)))
    in_specs = [a_spec, b_spec]
    operands = [a, b]
    if epi is not None:
        in_specs.append(pl.BlockSpec((tm, tn), lambda i, j, k: (i, j)))
        operands.append(epi_in)
    out_shape = jax.ShapeDtypeStruct((M, N), out_dtype)
    out_spec = pl.BlockSpec((tm, tn), lambda i, j, k: (i, j))
    dims = (((0 if ta else 1,), (1 if tb else 0,)), ((), ()))
    n_in = len(operands)

    def body(*refs):
        a_ref, b_ref = refs[0], refs[1]
        e_ref = refs[2] if epi is not None else None
        o_ref = refs[n_in]

        def product():
            av = a_ref[...]
            if a_fn is not None:
                av = a_fn(av)
            return lax.dot_general(av.astype(BF16), b_ref[...].astype(BF16), dims, preferred_element_type=F32)

        def finish(r):
            if epi is not None:
                r = epi(r, e_ref[...])
            if out_scale is not None:
                r = r * out_scale
            o_ref[...] = r.astype(out_dtype)

        if nk == 1:
            finish(product())
            return
        acc_ref = refs[n_in + 1]
        k = pl.program_id(2)

        @pl.when(k == 0)
        def _():
            acc_ref[...] = product()

        @pl.when(k > 0)
        def _():
            acc_ref[...] += product()

        @pl.when(k == nk - 1)
        def _():
            finish(acc_ref[...])

    return pl.pallas_call(
        body, name=name, grid=grid, in_specs=in_specs, out_specs=out_spec, out_shape=out_shape,
        scratch_shapes=[pltpu.VMEM((tm, tn), F32)] if nk > 1 else [],
        compiler_params=pltpu.CompilerParams(dimension_semantics=("parallel", "parallel", "arbitrary")),
    )(*operands)


def _relu2(u):
    r = jnp.maximum(u.astype(F32), 0.0)
    return r * r


def _relu2_grad(acc, u):
    return acc * (2.0 * jnp.maximum(u.astype(F32), 0.0))


def rowwise(fn, rows, vecs, out_rows, n_acc, *, name, tile=ROW_TILE):
    S = rows[0].shape[0]
    tile = _tile(S, tile)
    n_rows, n_vecs, n_out = len(rows), len(vecs), len(out_rows)
    acc_cols = [None] * n_acc

    def body(*refs):
        ins = [r[...] for r in refs[:n_rows + n_vecs]]
        outs = refs[n_rows + n_vecs:]
        ro, ac = fn(*ins)
        assert len(ro) == n_out and len(ac) == n_acc
        for r, o in zip(outs[:n_out], ro):
            r[...] = o.astype(r.dtype)
        i = pl.program_id(0)
        for r, a in zip(outs[n_out:], ac):
            @pl.when(i == 0)
            def _():
                r[...] = jnp.zeros_like(r)
            r[...] += a

    acc_shapes = jax.eval_shape(
        lambda *xs: fn(*xs)[1],
        *[jax.ShapeDtypeStruct((tile, r.shape[1]), r.dtype) for r in rows],
        *[jax.ShapeDtypeStruct(v.shape, v.dtype) for v in vecs])
    in_specs = [pl.BlockSpec((tile, r.shape[1]), lambda i: (i, 0)) for r in rows]
    in_specs += [pl.BlockSpec(v.shape, lambda i: (0, 0)) for v in vecs]
    out_specs = [pl.BlockSpec((tile, c), lambda i: (i, 0)) for c, _ in out_rows]
    out_specs += [pl.BlockSpec(a.shape, lambda i: (0, 0)) for a in acc_shapes]
    out_shape = [jax.ShapeDtypeStruct((S, c), dt) for c, dt in out_rows]
    out_shape += [jax.ShapeDtypeStruct(a.shape, F32) for a in acc_shapes]
    del acc_cols
    return pl.pallas_call(
        body, name=name, grid=(S // tile,), in_specs=in_specs, out_specs=out_specs, out_shape=out_shape,
        compiler_params=pltpu.CompilerParams(dimension_semantics=("arbitrary",)),
    )(*rows, *vecs)


def _rms(x, g):
    r = lax.rsqrt(jnp.mean(x * x, axis=-1, keepdims=True) + EPS)
    return x * r * g


def _rms_bwd(x, g, dy):
    r = lax.rsqrt(jnp.mean(x * x, axis=-1, keepdims=True) + EPS)
    xh = x * r
    dyg = dy * g
    dx = r * (dyg - xh * jnp.mean(dyg * xh, axis=-1, keepdims=True))
    dg = jnp.sum(dy * xh, axis=0, keepdims=True)
    return dx, dg


def norm_only(x, g, dtype, *, name):
    D = x.shape[1]
    return rowwise(lambda xv, gv: ([_rms(xv, gv)], []), [x], [g], [(D, dtype)], 0, name=name)[0]


def residual_norms(x, m, g_post, next_gs, next_dtypes, *, name):
    D = x.shape[1]

    def fn(xv, mv, gp, *gs):
        xn = xv + _rms(mv, gp)
        return [xn] + [_rms(xn, g) for g in gs], []

    return rowwise(fn, [x, m], [g_post] + list(next_gs), [(D, F32)] + [(D, dt) for dt in next_dtypes], 0, name=name)


def residual_loss(x, d, g_post, target, *, name):
    D = x.shape[1]

    def fn(xv, dv, tv, gp):
        e = xv + _rms(dv, gp) - tv
        dy = e * (1.0 / D)
        dd, dg = _rms_bwd(dv, gp, dy)
        return [dy, dd], [jnp.sum(e * e, axis=0, keepdims=True) * (0.5 / D), dg]

    return rowwise(fn, [x, d, target], [g_post], [(D, F32), (D, BF16)], 2, name=name)


def mid_bwd(dy, x_mid, dh2, m, g_mlp_pre, g_mix_post, dm_dtype, *, name):
    D = dy.shape[1]

    def fn(dyv, xm, dh, mv, gpre, gpost):
        dx, dg_pre = _rms_bwd(xm, gpre, dh)
        dxm = dyv + dx
        dm, dg_post = _rms_bwd(mv, gpost, dxm)
        return [dxm, dm], [dg_pre, dg_post]

    return rowwise(fn, [dy, x_mid, dh2, m], [g_mlp_pre, g_mix_post], [(D, F32), (D, dm_dtype)], 2, name=name)


def pre_norm_bwd(dxm, x, dhs, gs, below=None, *, name):
    D = x.shape[1]
    n = len(dhs)
    rows = [dxm, x] + list(dhs) + ([below[0]] if below else [])
    vecs = list(gs) + ([below[1]] if below else [])

    def fn(*xs):
        dxv, xv, dh = xs[0], xs[1], xs[2:2 + n]
        g = xs[len(rows):]
        out, accs = dxv, []
        for k in range(n):
            dx, dg = _rms_bwd(xv, g[k], dh[k])
            out = out + dx
            accs.append(dg)
        if below is None:
            return [out], accs
        dd, dg = _rms_bwd(xs[2 + n], g[n], out)
        return [out, dd], accs + [dg]

    return rowwise(fn, rows, vecs, [(D, F32)] + ([(D, BF16)] if below else []), n + (1 if below else 0), name=name)


def sum_concat_cast(pairs, dtype, *, name):
    C = pairs[0][0].shape[1]
    n = len(pairs)

    def fn(*xs):
        return [jnp.concatenate([sum(xs[:n]), sum(xs[n:])], axis=1)], []

    return rowwise(fn, [a for a, _ in pairs] + [b for _, b in pairs], [], [(2 * C, dtype)], 0, name=name)[0]


def _window_sum(e, window, total_rows, backward):
    s, k = e, 1
    while k < window:
        s = s + pltpu.roll(s, (total_rows - k) if backward else k, 0)
        k *= 2
    return s


def pool_fwd(h, w, scale, *, name):
    S, D = h.shape
    G = len(POOL_WINDOWS)
    GC = D // G
    tile = _tile(S, ROW_TILE)
    hb = tile // HALO

    def body(hc_ref, hp_ref, w_ref, sc_ref, o_ref):
        i = pl.program_id(0)
        prev = jnp.where(i > 0, hp_ref[...], 0.0)
        ext = jnp.concatenate([prev, hc_ref[...]], axis=0)
        t = i * tile + lax.broadcasted_iota(jnp.int32, (tile, 1), 0)
        outs = []
        for g, window in enumerate(POOL_WINDOWS):
            e = ext[:, g * GC:(g + 1) * GC]
            s = _window_sum(e, window, HALO + tile, False)[HALO:, :]
            cnt = jnp.minimum(t + 1, window).astype(F32)
            y = s / cnt - e[HALO:, :]
            outs.append(jnp.dot(y.astype(BF16), w_ref[g], preferred_element_type=F32))
        o_ref[...] = jnp.concatenate(outs, axis=1) * sc_ref[...]

    return pl.pallas_call(
        body, name=name, grid=(S // tile,),
        in_specs=[pl.BlockSpec((tile, D), lambda i: (i, 0)),
                  pl.BlockSpec((HALO, D), lambda i: (jnp.maximum(i * hb - 1, 0), 0)),
                  pl.BlockSpec((G, GC, GC), lambda i: (0, 0, 0)),
                  pl.BlockSpec((1, D), lambda i: (0, 0))],
        out_specs=pl.BlockSpec((tile, D), lambda i: (i, 0)),
        out_shape=jax.ShapeDtypeStruct((S, D), F32),
        compiler_params=pltpu.CompilerParams(dimension_semantics=("parallel",)),
    )(h, h, w, scale)


def pool_bwd(h, dm, w, scale, *, name):
    S, D = h.shape
    G = len(POOL_WINDOWS)
    GC = D // G
    tile = _tile(S, ROW_TILE)
    hb = tile // HALO
    n_tiles = S // tile
    last_halo = S // HALO - 1

    def body(hc_ref, hp_ref, dmc_ref, dmn_ref, w_ref, sc_ref, dh_ref, dw_ref, dsc_ref):
        i = pl.program_id(0)

        @pl.when(i == 0)
        def _():
            dw_ref[...] = jnp.zeros_like(dw_ref)
            dsc_ref[...] = jnp.zeros_like(dsc_ref)

        prev = jnp.where(i > 0, hp_ref[...], 0.0)
        ext = jnp.concatenate([prev, hc_ref[...]], axis=0)
        nxt = jnp.where(i < n_tiles - 1, dmn_ref[...], 0.0)
        dmc = dmc_ref[...]
        dm_ext = jnp.concatenate([dmc, nxt], axis=0)
        t = i * tile + lax.broadcasted_iota(jnp.int32, (tile, 1), 0)
        t_ext = i * tile + lax.broadcasted_iota(jnp.int32, (tile + HALO, 1), 0)
        dhs, dscs = [], []
        for g, window in enumerate(POOL_WINDOWS):
            cols = slice(g * GC, (g + 1) * GC)
            e = ext[:, cols]
            s = _window_sum(e, window, HALO + tile, False)[HALO:, :]
            y = (s / jnp.minimum(t + 1, window).astype(F32) - e[HALO:, :]).astype(BF16)
            wg = w_ref[g]
            ypre = jnp.dot(y, wg, preferred_element_type=F32)
            dscs.append(jnp.sum(dmc[:, cols] * ypre, axis=0, keepdims=True))
            dyp = (dm_ext[:, cols] * sc_ref[:, cols]).astype(BF16)
            dw_ref[g] += lax.dot_general(y, dyp[:tile, :], (((0,), (0,)), ((), ())), preferred_element_type=F32)
            dy = lax.dot_general(dyp, wg, (((1,), (1,)), ((), ())), preferred_element_type=F32)
            r = dy / jnp.minimum(t_ext + 1, window).astype(F32)
            sr = _window_sum(r, window, tile + HALO, True)
            dhs.append(sr[:tile, :] - dy[:tile, :])
        dh_ref[...] = jnp.concatenate(dhs, axis=1)
        dsc_ref[...] += jnp.concatenate(dscs, axis=1)

    return pl.pallas_call(
        body, name=name, grid=(n_tiles,),
        in_specs=[pl.BlockSpec((tile, D), lambda i: (i, 0)),
                  pl.BlockSpec((HALO, D), lambda i: (jnp.maximum(i * hb - 1, 0), 0)),
                  pl.BlockSpec((tile, D), lambda i: (i, 0)),
                  pl.BlockSpec((HALO, D), lambda i: (jnp.minimum((i + 1) * hb, last_halo), 0)),
                  pl.BlockSpec((G, GC, GC), lambda i: (0, 0, 0)),
                  pl.BlockSpec((1, D), lambda i: (0, 0))],
        out_specs=[pl.BlockSpec((tile, D), lambda i: (i, 0)),
                   pl.BlockSpec((G, GC, GC), lambda i: (0, 0, 0)),
                   pl.BlockSpec((1, D), lambda i: (0, 0))],
        out_shape=[jax.ShapeDtypeStruct((S, D), F32), jax.ShapeDtypeStruct((G, GC, GC), F32),
                   jax.ShapeDtypeStruct((1, D), F32)],
        compiler_params=pltpu.CompilerParams(dimension_semantics=("arbitrary",)),
    )(h, h, dm, dm, w, scale)


ATT_LANES = 256
N_PAIR = ATT_LANES // HEAD_DIM


def _scores(xs, kj):
    return [lax.dot_general(x, kj, (((1,), (1,)), ((), ())), preferred_element_type=F32) for x in xs]


def _softplus_parts(z, mask):
    sp = jnp.maximum(z, 0.0) + jnp.log(1.0 + jnp.exp(-jnp.abs(z)))
    logb = z - sp
    if mask is not None:
        sp = jnp.where(mask, sp, 0.0)
    return logb, sp.astype(BF16), jnp.sum(sp, axis=1, keepdims=True)


def _weights(logb, later, c, mask):
    a = jnp.exp(logb - (later + c))
    return a if mask is None else jnp.where(mask, a, 0.0)


def _tile_weights(zss, cs, u_later, mask):
    partss = [[_softplus_parts(z, mask) for z in zs] for zs in zss]
    laterss = [[jnp.dot(sp, u_later, preferred_element_type=F32) for _, sp, _ in parts] for parts in partss]
    out = []
    for parts, laters in zip(partss, laterss):
        out.append(([p[0] for p in parts], [_weights(p[0], later, c, mask) for p, later, c in zip(parts, laters, cs)]))
        cs = [c + p[2] for c, p in zip(cs, parts)]
    return out, tuple(cs)


def _tri(T, later):
    rows = lax.broadcasted_iota(jnp.int32, (T, T), 0)
    cols = lax.broadcasted_iota(jnp.int32, (T, T), 1)
    return jnp.where((rows > cols) if later else (rows < cols), 1.0, 0.0).astype(BF16)


def _head_masks(x2, axis=None):
    lane = lax.broadcasted_iota(jnp.int32, (1, ATT_LANES), 1)
    parts = [jnp.where((lane // HEAD_DIM) == hh, x2, jnp.zeros_like(x2)) for hh in range(N_PAIR)]
    return parts if axis is None else jnp.concatenate(parts, axis=axis)


def _cat_bf16(parts, axis):
    return jnp.concatenate([p.astype(BF16) for p in parts], axis=axis)


def _side_split(side, refs, n_in, n_out):
    if side is None:
        return refs, None
    n_src, n_buf = len(side.srcs), len(side.bufs)
    ins, rest = refs[:n_in], refs[n_in:]
    src_refs, rest = rest[:n_src], rest[n_src + n_buf:]
    outs, rest = rest[:n_out], rest[n_out:]
    buf_refs, rest = rest[:n_buf], rest[n_buf:]
    own_scratch, sems = rest[:len(rest) - 3], rest[len(rest) - 3:]
    return tuple(ins) + tuple(outs) + tuple(own_scratch), (src_refs, buf_refs, sems)


def _side_phase(side, side_refs, phase, when):
    if side is None:
        return

    @pl.when(when)
    def _():
        side.copies(phase, *side_refs)


def attn_fwd(q, kv, *, name, side=None):
    S, D = q.shape
    P = D // ATT_LANES
    T = _tile(S, ATT_TILE)
    nq = S // T

    def body(*refs):
        (q_ref, k_ref, v_ref, o_ref), side_refs = _side_split(side, refs, 3, 1)
        i = pl.program_id(1)
        p = pl.program_id(0)
        _side_phase(side, side_refs, 0, (p == 0) & (i == 0))
        u_later = _tri(T, True)
        qhs = _head_masks(q_ref[...])
        diag = lax.broadcasted_iota(jnp.int32, (T, T), 1) < lax.broadcasted_iota(jnp.int32, (T, T), 0)

        def rows_of(ref, j):
            return ref[pl.ds(pl.multiple_of(j * T, T), T), :]

        def step(js, carry, mask):
            cs, acc = carry
            zss = [_scores(qhs, rows_of(k_ref, j)) for j in js]
            vcat = jnp.concatenate([_head_masks(rows_of(v_ref, j), axis=0) for j in js], axis=0)
            per_tile, cs = _tile_weights(zss, cs, u_later, mask)
            acat = _cat_bf16([a for _, aa in per_tile for a in aa], 1)
            return cs, acc + jnp.dot(acat, vcat, preferred_element_type=F32)

        carry = step([i], ((jnp.zeros((T, 1), F32),) * N_PAIR, jnp.zeros((T, ATT_LANES), F32)), diag)
        carry = lax.fori_loop(0, i % 2, lambda n, cr: step([i - 1], cr, None), carry)
        first = i - 1 - i % 2
        _, acc = lax.fori_loop(0, i // 2, lambda n, cr: step([first - 2 * n, first - 2 * n - 1], cr, None), carry)
        o_ref[...] = acc.astype(o_ref.dtype)
        _side_phase(side, side_refs, 1, (p == P - 1) & (i == nq - 1))

    sd = side
    outs = pl.pallas_call(
        body, name=name, grid=(P, nq),
        in_specs=[pl.BlockSpec((T, ATT_LANES), lambda p, i: (i, p)),
                  pl.BlockSpec((S, ATT_LANES), lambda p, i: (0, p)),
                  pl.BlockSpec((S, ATT_LANES), lambda p, i: (0, P + p))] + (sd.specs() if sd else []),
        out_specs=[pl.BlockSpec((T, ATT_LANES), lambda p, i: (i, p))] + (sd.out_specs() if sd else []),
        out_shape=[jax.ShapeDtypeStruct((S, D), BF16)] + (sd.out_shape() if sd else []),
        scratch_shapes=sd.scratch() if sd else [],
        input_output_aliases=sd.aliases(3, 1) if sd else {},
        compiler_params=pltpu.CompilerParams(dimension_semantics=("arbitrary", "arbitrary"),
                                             has_side_effects=sd is not None),
    )(q, kv, kv, *(sd.operands() if sd else []))
    return outs[0] if sd is None else (outs[0], sd.result(outs[1:]))


def attn_bwd(q, kv, do, *, name, side=None):
    S, D = q.shape
    P = D // ATT_LANES
    T = _tile(S, ATT_TILE)
    nb = S // T

    def body(*refs):
        (q_ref, k_ref, v_ref, do_ref, dq_ref, dk_ref, dv_ref, g_scr, s_scr), side_refs = _side_split(side, refs, 4, 3)
        i = pl.program_id(1)
        p = pl.program_id(0)
        _side_phase(side, side_refs, 0, (p == 0) & (i == 0))

        @pl.when(i == 0)
        def _():
            dk_ref[...] = jnp.zeros_like(dk_ref)
            dv_ref[...] = jnp.zeros_like(dv_ref)

        u_later = _tri(T, True)
        u_earlier = _tri(T, False)
        qhs = _head_masks(q_ref[...])
        dohs = _head_masks(do_ref[...])
        qcat = jnp.concatenate(qhs, axis=0)
        docat = jnp.concatenate(dohs, axis=0)
        diag = lax.broadcasted_iota(jnp.int32, (T, T), 1) < lax.broadcasted_iota(jnp.int32, (T, T), 0)
        tdot = (((0,), (0,)), ((), ()))

        def rows_of(ref, j):
            return ref.at[pl.ds(pl.multiple_of(j * T, T), T), :]

        def step1(js, cs, mask):
            dass = [_scores(dohs, rows_of(v_ref, j)[...]) for j in js]
            zss = [_scores(qhs, rows_of(k_ref, j)[...]) for j in js]
            per_tile, cs = _tile_weights(zss, cs, u_later, mask)
            for j, das, (logbs, aa) in zip(js, dass, per_tile):
                for hh in range(N_PAIR):
                    g_scr[hh, j] = (das[hh] * aa[hh]).astype(BF16)
                    sg = jnp.exp(logbs[hh])
                    if mask is not None:
                        sg = jnp.where(mask, sg, 0.0)
                    s_scr[hh, j] = sg.astype(BF16)
            for j, (_, aa) in zip(js, per_tile):
                rows_of(dv_ref, j)[...] += lax.dot_general(_cat_bf16(aa, 0), docat, tdot,
                                                          preferred_element_type=F32)
            return cs

        cs = step1([i], (jnp.zeros((T, 1), F32),) * N_PAIR, diag)
        cs = lax.fori_loop(0, i % 2, lambda n, c: step1([i - 1], c, None), cs)
        first = i - 1 - i % 2
        lax.fori_loop(0, i // 2, lambda n, c: step1([first - 2 * n, first - 2 * n - 1], c, None), cs)

        def step2(js, carry):
            cs, acc = carry
            cumss = [[jnp.dot(g_scr[hh, j], u_earlier, preferred_element_type=F32) for hh in range(N_PAIR)]
                     for j in js]
            kcat = jnp.concatenate([_head_masks(rows_of(k_ref, j)[...], axis=0) for j in js], axis=0)
            dzss = []
            for j, cums in zip(js, cumss):
                dzs, new = [], []
                for hh in range(N_PAIR):
                    gf = g_scr[hh, j].astype(F32)
                    sg = s_scr[hh, j].astype(F32)
                    dzs.append((gf - sg * (gf + (cums[hh] + cs[hh]))).astype(BF16))
                    new.append(cs[hh] + jnp.sum(gf, axis=1, keepdims=True))
                cs = tuple(new)
                dzss.append(dzs)
            acc = acc + jnp.dot(jnp.concatenate([dz for dzs in dzss for dz in dzs], axis=1), kcat,
                                preferred_element_type=F32)
            for j, dzs in zip(js, dzss):
                rows_of(dk_ref, j)[...] += lax.dot_general(jnp.concatenate(dzs, axis=0), qcat, tdot,
                                                          preferred_element_type=F32)
            return cs, acc

        carry = ((jnp.zeros((T, 1), F32),) * N_PAIR, jnp.zeros((T, ATT_LANES), F32))
        carry = lax.fori_loop(0, (i + 1) // 2, lambda n, cr: step2([2 * n, 2 * n + 1], cr), carry)
        _, dq = lax.fori_loop(0, (i + 1) % 2, lambda n, cr: step2([i], cr), carry)
        dq_ref[...] = (dq * (HEAD_DIM ** -0.5)).astype(dq_ref.dtype)
        _side_phase(side, side_refs, 1, (p == P - 1) & (i == nb - 1))

    sd = side
    outs = pl.pallas_call(
        body, name=name, grid=(P, nb),
        in_specs=[pl.BlockSpec((T, ATT_LANES), lambda p, i: (i, p)),
                  pl.BlockSpec((S, ATT_LANES), lambda p, i: (0, p)),
                  pl.BlockSpec((S, ATT_LANES), lambda p, i: (0, P + p)),
                  pl.BlockSpec((T, ATT_LANES), lambda p, i: (i, p))] + (sd.specs() if sd else []),
        out_specs=[pl.BlockSpec((T, ATT_LANES), lambda p, i: (i, p)),
                   pl.BlockSpec((S, ATT_LANES), lambda p, i: (0, p)),
                   pl.BlockSpec((S, ATT_LANES), lambda p, i: (0, p))] + (sd.out_specs() if sd else []),
        out_shape=[jax.ShapeDtypeStruct((S, D), BF16), jax.ShapeDtypeStruct((S, D), F32),
                   jax.ShapeDtypeStruct((S, D), F32)] + (sd.out_shape() if sd else []),
        scratch_shapes=[pltpu.VMEM((N_PAIR, nb, T, T), BF16), pltpu.VMEM((N_PAIR, nb, T, T), BF16)]
        + (sd.scratch() if sd else []),
        input_output_aliases=sd.aliases(4, 3) if sd else {},
        compiler_params=pltpu.CompilerParams(dimension_semantics=("arbitrary", "arbitrary"),
                                             has_side_effects=sd is not None),
    )(q, kv, kv, do, *(sd.operands() if sd else []))
    return tuple(outs[:3]) if sd is None else (*outs[:3], sd.result(outs[3:]))


def _window(ref, axis, dev, n):
    return ref.at[(slice(None),) * axis + (pl.ds(dev * n, n),)]


def all_gather(ops, *, name):
    n_ops = len(ops)
    out_shape = []
    for a, ax in ops:
        shp = list(a.shape)
        shp[ax] *= N_DEV
        out_shape.append(jax.ShapeDtypeStruct(tuple(shp), a.dtype))

    def body(*refs):
        ins, outs = refs[:n_ops], refs[n_ops:2 * n_ops]
        send_sems, recv_sems, local_sems = refs[2 * n_ops:]
        x, y, c = (lax.axis_index(n) for n in MESH_AXES)
        me, sibling = (x, y, c), (x, y, 1 - c)
        chips = [(1 - x, y), (x, 1 - y), (1 - x, 1 - y)]

        def rows(o, dev):
            px, py, pc = dev
            ax = ops[o][1]
            return _window(outs[o], ax, 4 * px + 2 * py + pc, ops[o][0].shape[ax])

        def copy(o, k, block, to, src=None):
            return pltpu.make_async_remote_copy(
                src_ref=rows(o, block) if src is None else src, dst_ref=rows(o, block),
                send_sem=send_sems.at[o, k], recv_sem=recv_sems.at[o, k],
                device_id=to, device_id_type=pl.DeviceIdType.MESH)

        mine, first, passed = [], [], []
        for o in range(n_ops):
            cp = pltpu.make_async_copy(ins[o], rows(o, me), local_sems.at[o])
            cp.start()
            mine.append(cp)
            first.append(copy(o, 0, me, sibling, src=ins[o]))
            first += [copy(o, 1 + j, me, (*chip, c), src=ins[o]) for j, chip in enumerate(chips)]
        for cp in first:
            cp.start()
        for j, chip in enumerate(chips):
            for o in range(n_ops):
                copy(o, 1 + j, (*chip, c), me).wait_recv()
                cp = copy(o, 4 + j, (*chip, c), sibling)
                cp.start()
                passed.append(cp)
        for o in range(n_ops):
            copy(o, 0, sibling, me).wait_recv()
            for j, chip in enumerate(chips):
                copy(o, 4 + j, (*chip, 1 - c), me).wait_recv()
        for cp in first + passed:
            cp.wait_send()
        for cp in mine:
            cp.wait()

    any_spec = pl.BlockSpec(memory_space=pl.ANY)
    return pl.pallas_call(
        body, name=name, in_specs=[any_spec] * n_ops, out_specs=[any_spec] * n_ops, out_shape=out_shape,
        scratch_shapes=[pltpu.SemaphoreType.DMA((n_ops, 7)), pltpu.SemaphoreType.DMA((n_ops, 7)),
                        pltpu.SemaphoreType.DMA((n_ops,))],
        compiler_params=pltpu.CompilerParams(has_side_effects=True),
    )(*[a for a, _ in ops])


class Side:
    def __init__(self, kind, items, bufs):
        self.kind, self.items = kind, items
        self.names = list(bufs)
        self.bufs = [bufs[n] for n in self.names]
        self.srcs = [it[0] for it in items]

    def operands(self):
        return self.srcs + self.bufs

    def specs(self):
        return [pl.BlockSpec(memory_space=pl.ANY)] * (len(self.srcs) + len(self.bufs))

    def out_specs(self):
        return [pl.BlockSpec(memory_space=pl.ANY)] * len(self.bufs)

    def out_shape(self):
        return [jax.ShapeDtypeStruct(b.shape, b.dtype) for b in self.bufs]

    def aliases(self, first_in, first_out):
        return {first_in + len(self.srcs) + k: first_out + k for k in range(len(self.bufs))}

    def scratch(self):
        n = len(self.items)
        return [pltpu.SemaphoreType.DMA((n, N_DEV - 1)), pltpu.SemaphoreType.DMA((n, N_DEV - 1)),
                pltpu.SemaphoreType.DMA((n,))]

    def result(self, outs):
        return dict(zip(self.names, outs))

    def copies(self, phase, src_refs, buf_refs, sems):
        send_sems, recv_sems, local_sems = sems
        pos = tuple(lax.axis_index(n) for n in MESH_AXES)
        me = 4 * pos[0] + 2 * pos[1] + pos[2]
        for o, (_, ax, name, layer) in enumerate(self.items):
            src, buf = src_refs[o], buf_refs[self.names.index(name)]
            if self.kind == "gather":
                whole = buf if layer is None else buf.at[layer]
                n = src.shape[ax]
                sent = lambda dev, src=src: src
                lands = lambda dev, whole=whole, ax=ax, n=n: _window(whole, ax, dev, n)
            else:
                n = src.shape[ax] // N_DEV
                sent = lambda dev, src=src, ax=ax, n=n: _window(src, ax, dev, n)
                lands = lambda dev, buf=buf, layer=layer: buf.at[dev] if layer is None else buf.at[dev, layer]
            if local_sems is not None:
                local = pltpu.make_async_copy(sent(me), lands(me), local_sems.at[o])
                if phase == 0:
                    local.start()
                else:
                    local.wait()
            for r in range(1, N_DEV):
                peer = tuple(1 - p if r & bit else p for p, bit in zip(pos, (4, 2, 1)))
                pid = 4 * peer[0] + 2 * peer[1] + peer[2]
                cp = pltpu.make_async_remote_copy(
                    src_ref=sent(pid) if phase == 0 else sent(me), dst_ref=lands(me) if phase == 0 else lands(pid),
                    send_sem=send_sems[o] if isinstance(send_sems, (list, tuple)) else send_sems.at[o, r - 1],
                    recv_sem=recv_sems[o] if isinstance(recv_sems, (list, tuple)) else recv_sems.at[o, r - 1],
                    device_id=peer, device_id_type=pl.DeviceIdType.MESH)
                if phase == 0:
                    cp.start()
                else:
                    cp.wait_recv()
                    cp.wait_send()


def _hbm(a):
    return pltpu.with_memory_space_constraint(a, pltpu.HBM)


def own_blocks(items, bufs):
    me = 4 * lax.axis_index("x") + 2 * lax.axis_index("y") + lax.axis_index("c")
    bufs = {name: bufs[name] for _, _, name, _ in items}
    for src, ax, name, layer in items:
        n = src.shape[ax] // N_DEV
        blk = lax.dynamic_slice_in_dim(src, me * n, n, axis=ax)
        lead = (me,) if layer is None else (me, layer)
        blk = blk.reshape((1,) * len(lead) + blk.shape)
        bufs[name] = lax.dynamic_update_slice(bufs[name], blk, lead + (0,) * src.ndim)
    return bufs


def start_side(side, *, name):
    n_src, n_buf = len(side.srcs), len(side.bufs)
    n = len(side.items)

    def body(*refs):
        src_refs, buf_refs = refs[:n_src], refs[n_src:n_src + n_buf]
        sems = refs[n_src + n_buf:n_src + n_buf + 2 * n]
        token = refs[-1]
        side.copies(0, src_refs, buf_refs, (list(sems[:n]), list(sems[n:]), None))
        token[...] = jnp.zeros_like(token)

    hbm = pl.BlockSpec(memory_space=pltpu.HBM)
    sem = pl.BlockSpec(memory_space=pltpu.SEMAPHORE)
    operands = side.operands()
    outs = pl.pallas_call(
        body, name=name,
        out_shape=(*[pltpu.SemaphoreType.DMA(())] * (2 * n),
                   *[pltpu.HBM(a.shape, a.dtype) for a in operands], jax.ShapeDtypeStruct((8, LANES), F32)),
        in_specs=[hbm] * len(operands),
        out_specs=(*[sem] * (2 * n), *[hbm] * len(operands), pl.BlockSpec(memory_space=pltpu.VMEM)),
        input_output_aliases={k: 2 * n + k for k in range(len(operands))},
        compiler_params=pltpu.CompilerParams(has_side_effects=pltpu.SideEffectType.DATAFLOW_SIDE_EFFECTING),
    )(*[_hbm(a) for a in operands])
    started = (side, list(outs[:n]), list(outs[n:2 * n]), list(outs[2 * n:2 * n + n_src]))
    return started, dict(zip(side.names, outs[2 * n + n_src:2 * n + n_src + n_buf])), outs[-1]


def wait_sides(started, bufs, after, *, name):
    names = list(bufs)
    flat = []
    for side, send_sems, recv_sems, srcs in started:
        flat += [*srcs, *send_sems, *recv_sems]
    n_buf = len(names)

    def body(*refs):
        buf_refs = refs[:n_buf]
        rest = refs[n_buf:]
        for side, _, _, srcs in started:
            n = len(srcs)
            src_refs, send_sems, recv_sems = rest[:n], list(rest[n:2 * n]), list(rest[2 * n:3 * n])
            rest = rest[3 * n:]
            side.copies(1, src_refs, [buf_refs[names.index(nm)] for nm in side.names], (send_sems, recv_sems, None))

    hbm = pl.BlockSpec(memory_space=pltpu.HBM)
    sem = pl.BlockSpec(memory_space=pltpu.SEMAPHORE)
    specs = []
    for _, _, _, srcs in started:
        specs += [hbm] * len(srcs) + [sem] * (2 * len(srcs))
    outs = pl.pallas_call(
        body, name=name,
        out_shape=tuple(pltpu.HBM(bufs[nm].shape, bufs[nm].dtype) for nm in names),
        in_specs=[hbm] * n_buf + specs + [pl.BlockSpec(memory_space=pl.ANY)],
        out_specs=tuple([hbm] * n_buf),
        input_output_aliases={k: k for k in range(n_buf)},
        compiler_params=pltpu.CompilerParams(has_side_effects=pltpu.SideEffectType.DATAFLOW_SIDE_EFFECTING),
    )(*[bufs[nm] for nm in names], *flat, after)
    return dict(zip(names, outs))


def _adamw(w, g, m, v):
    m = ADAM_B1 * m + (1.0 - ADAM_B1) * g
    v = ADAM_B2 * v + (1.0 - ADAM_B2) * (g * g)
    m_hat = m / (1.0 - ADAM_B1 ** ADAM_STEP)
    v_hat = v / (1.0 - ADAM_B2 ** ADAM_STEP)
    delta = -ADAM_LR * (m_hat / (jnp.sqrt(v_hat) + ADAM_EPS) + ADAM_WD * w)
    return delta, m, v


def reduce_adamw(parts, w, m, v, *, name):
    shape = w.shape
    C = shape[-1]
    R = w.size // C
    tile = _tile(R, 256)

    def body(p_ref, w_ref, m_ref, v_ref, g_ref, d_ref, nm_ref, nv_ref):
        g = p_ref[0].astype(F32)
        for s in range(1, N_DEV):
            g = g + p_ref[s].astype(F32)
        d, nm, nv = _adamw(w_ref[...], g, m_ref[...], v_ref[...])
        g_ref[...] = g
        d_ref[...] = d
        nm_ref[...] = nm
        nv_ref[...] = nv

    row = pl.BlockSpec((tile, C), lambda i: (i, 0))
    outs = pl.pallas_call(
        body, name=name, grid=(R // tile,),
        in_specs=[pl.BlockSpec((N_DEV, tile, C), lambda i: (0, i, 0)), row, row, row],
        out_specs=[row] * 4, out_shape=[jax.ShapeDtypeStruct((R, C), F32)] * 4,
        compiler_params=pltpu.CompilerParams(dimension_semantics=("parallel",)),
    )(parts.reshape(N_DEV, R, C), w.reshape(R, C), m.reshape(R, C), v.reshape(R, C))
    return [o.reshape(shape) for o in outs]


def gains_adamw(parts, groups, *, name):
    n = len(groups)

    def body(*refs):
        p_ref = refs[0]
        ins, outs = refs[1:1 + 3 * n], refs[1 + 3 * n:]
        for k in range(n):
            w_ref, m_ref, v_ref = ins[3 * k:3 * k + 3]
            L = w_ref.shape[0]
            g = p_ref[pl.ds(8 * k, L), :]
            for s in range(1, N_DEV):
                g = g + p_ref[pl.ds(s * GAIN_ROWS + 8 * k, L), :]
            d, nm, nv = _adamw(w_ref[...], g, m_ref[...], v_ref[...])
            for r, val in zip(outs[4 * k:4 * k + 4], (g, d, nm, nv)):
                r[...] = val

    flat = [a for grp in groups for a in grp]
    out_shape = [jax.ShapeDtypeStruct(grp[0].shape, F32) for grp in groups for _ in range(4)]
    outs = pl.pallas_call(body, name=name, out_shape=out_shape)(parts, *flat)
    return [outs[4 * k:4 * k + 4] for k in range(n)]


def kernel(x, pool_w, pool_scale, w_q, w_kv, kv_norm_g, w_o, w_up, w_down, mix_pre_g, mix_post_g, mlp_pre_g, mlp_post_g, loss_target, m_pool_w, m_pool_scale, m_w_q, m_w_kv, m_kv_norm_g, m_w_o, m_w_up, m_w_down, m_mix_pre_g, m_mix_post_g, m_mlp_pre_g, m_mlp_post_g, v_pool_w, v_pool_scale, v_w_q, v_w_kv, v_kv_norm_g, v_w_o, v_w_up, v_w_down, v_mix_pre_g, v_mix_post_g, v_mlp_pre_g, v_mlp_post_g):
    _, S, D = x.shape
    x0 = x.reshape(S, D)
    target = loss_target.reshape(S, D)
    depth = w_up.shape[0]
    n_pool = pool_w.shape[0]
    F = w_up.shape[2] * N_DEV
    G = pool_w.shape[1]
    GC = D // G

    def vec(a, l):
        return a[l].reshape(1, D)

    n_att = depth - n_pool
    wq_s, wkv_s, wo_s = w_q.astype(BF16), w_kv.astype(BF16), w_o.astype(BF16)
    wup_s, wdn_s, pw_s = w_up.astype(BF16), w_down.astype(BF16), pool_w.astype(BF16)
    wq0, wkv, wup_a, wdn_a, pw, psc = all_gather(
        [(wq_s[0], 0), (wkv_s, 1), (wup_s[:n_pool], 2), (wdn_s[:n_pool], 1), (pw_s, 2), (pool_scale, 1)],
        name="gather_weights")
    late = Side("gather",
                [(wq_s[1:], 1, "wq", None), (wo_s, 1, "wo", None), (wup_s[n_pool:], 2, "wup", None),
                 (wdn_s[n_pool:], 1, "wdn", None)],
                {"wq": lax.empty((n_att - 1, D, D), BF16), "wo": lax.empty((n_att, D, D), BF16),
                 "wup": lax.empty((n_att, D, F), BF16), "wdn": lax.empty((n_att, F, D), BF16)})
    late_w = None

    def layer_of(early, key, l, n_early):
        return (early, l) if l < n_early else (late_w[key], l - n_early)

    saved = []
    xs = x0
    h1 = norm_only(xs, vec(mix_pre_g, 0), F32, name="norm_in")
    kv = hk = None
    dy = loss_rows = None
    for l in range(depth):
        is_pool = l < n_pool
        st = {"x": xs, "h1": h1}
        if is_pool:
            m = pool_fwd(h1, pw[l], psc[l].reshape(1, D), name=f"pool_fwd{l}")
        else:
            j = l - n_pool
            wq_j, wq_l = (wq0, None) if j == 0 else (late_w["wq"], j - 1)
            q = matmul(h1, wq_j, b_layer=wq_l, out_dtype=BF16, out_scale=HEAD_DIM ** -0.5, name=f"q_proj{j}")
            if j == 0:
                o, late_w = attn_fwd(q, kv, name=f"attn_fwd{j}", side=late)
            else:
                o = attn_fwd(q, kv, name=f"attn_fwd{j}")
            m = matmul(o, late_w["wo"], b_layer=j, name=f"o_proj{j}")
            st.update(q=q, o=o)
        x_mid, h2 = residual_norms(xs, m, vec(mix_post_g, l), [vec(mlp_pre_g, l)], [BF16], name=f"mix_out{l}")
        wup, lu = layer_of(wup_a, "wup", l, n_pool)
        wdn, ld = layer_of(wdn_a, "wdn", l, n_pool)
        u = matmul(h2, wup, b_layer=lu, out_dtype=BF16, name=f"mlp_up{l}")
        d = matmul(u, wdn, b_layer=ld, a_fn=_relu2, name=f"mlp_down{l}")
        st.update(m=m, x_mid=x_mid, h2=h2, u=u, d=d)
        saved.append(st)
        if l == depth - 1:
            dy, dd, loss_rows, dg_last = residual_loss(x_mid, d, vec(mlp_post_g, l), target, name="loss")
        elif l == n_pool - 1:
            xs, h1, hk = residual_norms(x_mid, d, vec(mlp_post_g, l), [vec(mix_pre_g, l + 1), kv_norm_g.reshape(1, D)],
                                        [BF16, BF16], name=f"mlp_out{l}")
            kv = matmul(hk, wkv, out_dtype=BF16, name="kv_proj")
        else:
            nxt_dt = F32 if l + 1 < n_pool else BF16
            xs, h1 = residual_norms(x_mid, d, vec(mlp_post_g, l), [vec(mix_pre_g, l + 1)], [nxt_dt], name=f"mlp_out{l}")
    loss = lax.psum(jnp.sum(loss_rows), MESH_AXES)

    recv = {"w_q": (w_q, BF16), "w_kv": (w_kv, BF16), "w_o": (w_o, BF16), "w_up": (w_up, BF16),
            "w_down": (w_down, BF16), "pool_w": (pool_w, F32), "pool_scale": (pool_scale, F32)}
    recv = {n: lax.empty((N_DEV,) + w.shape, dt) for n, (w, dt) in recv.items()}
    pending = []

    def send_pending():
        return Side("scatter", list(pending), {n: recv[n] for n in dict.fromkeys(it[2] for it in pending)})

    started = []

    def start_pending(tag, then):
        recv.update(own_blocks(pending, recv))
        begun, got, token = start_side(send_pending(), name=f"exchange_start_{tag}")
        recv.update(got)
        started.append(begun)
        pending.clear()
        return lax.optimization_barrier((then, token))[0]

    g_pw, g_psc = [None] * n_pool, [None] * n_pool
    gains = {k: [None] * depth for k in ("mix_pre", "mix_post", "mlp_pre", "mlp_post")}
    gains["mlp_post"][depth - 1] = dg_last
    dkvs = []
    g_kvn = None
    for l in reversed(range(depth)):
        st = saved[l]
        is_pool = l < n_pool
        wup, lu = layer_of(wup_a, "wup", l, n_pool)
        wdn, ld = layer_of(wdn_a, "wdn", l, n_pool)
        du = matmul(dd, wdn, b_layer=ld, tb=True, out_dtype=BF16, epi=_relu2_grad, epi_in=st["u"], name=f"mlp_du{l}")
        g_wdn = matmul(st["u"], dd, ta=True, a_fn=_relu2, out_dtype=BF16, name=f"mlp_dwdn{l}")
        g_wup = matmul(st["h2"], du, ta=True, out_dtype=BF16, name=f"mlp_dwup{l}")
        pending += [(g_wup, 1, "w_up", l), (g_wdn, 0, "w_down", l)]
        dh2 = matmul(du, wup, b_layer=lu, tb=True, name=f"mlp_dh{l}")
        if is_pool:
            dh2 = start_pending(f"l{l}", dh2)
        dxm, dm, gains["mlp_pre"][l], gains["mix_post"][l] = mid_bwd(
            dy, st["x_mid"], dh2, st["m"], vec(mlp_pre_g, l), vec(mix_post_g, l), F32 if is_pool else BF16,
            name=f"mid_bwd{l}")
        if is_pool:
            dh1, g_pw[l], g_psc[l] = pool_bwd(st["h1"], dm, pw[l], psc[l].reshape(1, D), name=f"pool_bwd{l}")
        else:
            j = l - n_pool
            wq_j, wq_l = (wq0, None) if j == 0 else (late_w["wq"], j - 1)
            do = matmul(dm, late_w["wo"], b_layer=j, tb=True, out_dtype=BF16, name=f"o_proj_dx{j}")
            g_wo = matmul(st["o"], dm, ta=True, out_dtype=BF16, name=f"o_proj_dw{j}")
            pending.append((g_wo, 0, "w_o", j))
            dq, dk, dv, got = attn_bwd(st["q"], kv, do, name=f"attn_bwd{j}", side=send_pending())
            recv.update(got)
            pending.clear()
            dkvs.append((dk, dv))
            g_wq = matmul(st["h1"], dq, ta=True, out_dtype=BF16, name=f"q_proj_dw{j}")
            pending.append((g_wq, 0, "w_q", j))
            dh1 = matmul(dq, wq_j, b_layer=wq_l, tb=True, name=f"q_proj_dx{j}")
        if l == n_pool:
            dkv = sum_concat_cast(dkvs, BF16, name="dkv_pack")
            g_wkv = matmul(hk, dkv, ta=True, out_dtype=BF16, name="kv_proj_dw")
            pending.append((g_wkv, 1, "w_kv", None))
            dhk = matmul(dkv, wkv, tb=True, name="kv_proj_dx")
            dhs, gs = [dh1, dhk], [vec(mix_pre_g, l), kv_norm_g.reshape(1, D)]
        else:
            dhs, gs = [dh1], [vec(mix_pre_g, l)]
        below = (saved[l - 1]["d"], vec(mlp_post_g, l - 1)) if l > 0 else None
        outs = pre_norm_bwd(dxm, st["x"], dhs, gs, below, name=f"mix_in_bwd{l}")
        dy, outs = outs[0], outs[1:]
        if below is not None:
            dd, outs = outs[0], outs[1:]
            gains["mlp_post"][l - 1] = outs[-1]
        gains["mix_pre"][l] = outs[0]
        if l == n_pool:
            g_kvn = outs[1]
    grad_x = dy.reshape(x.shape)

    g_pool_w = jnp.stack(g_pw)
    g_pool_scale = jnp.concatenate(g_psc, axis=0)
    zero_rows = jnp.zeros((8 - depth, D), F32)
    gain_rows = []
    for k in ("mix_pre", "mix_post", "mlp_pre", "mlp_post"):
        gain_rows += gains[k] + [zero_rows]
    gain_rows += [g_kvn, jnp.zeros((7, D), F32)]
    gain_pack = jnp.concatenate(gain_rows, axis=0)
    gain_parts = all_gather([(gain_pack, 0)], name="gather_gain_grads")[0]
    pending += [(g_pool_w, 2, "pool_w", None), (g_pool_scale, 1, "pool_scale", None)]
    dy = start_pending("pool", dy)
    sent = dict.fromkeys(n for side, _, _, _ in started for n in side.names)
    recv.update(wait_sides(started, {n: recv[n] for n in sent}, dy, name="exchange_wait"))
    big = {"w_q": (w_q, m_w_q, v_w_q), "w_kv": (w_kv, m_w_kv, v_w_kv), "w_o": (w_o, m_w_o, v_w_o),
           "w_up": (w_up, m_w_up, v_w_up), "w_down": (w_down, m_w_down, v_w_down),
           "pool_w": (pool_w, m_pool_w, v_pool_w), "pool_scale": (pool_scale, m_pool_scale, v_pool_scale)}
    res = {n: reduce_adamw(recv[n], *wmv, name=f"adamw_{n}") for n, wmv in big.items()}
    gain_groups = [(mix_pre_g, m_mix_pre_g, v_mix_pre_g), (mix_post_g, m_mix_post_g, v_mix_post_g),
                   (mlp_pre_g, m_mlp_pre_g, v_mlp_pre_g), (mlp_post_g, m_mlp_post_g, v_mlp_post_g),
                   (kv_norm_g.reshape(1, D), m_kv_norm_g.reshape(1, D), v_kv_norm_g.reshape(1, D))]
    gres = gains_adamw(gain_parts, gain_groups, name="adamw_gains")
    for n, r in zip(["mix_pre_g", "mix_post_g", "mlp_pre_g", "mlp_post_g"], gres[:4]):
        res[n] = r
    res["kv_norm_g"] = [a.reshape(D) for a in gres[4]]

    order = ["pool_w", "pool_scale", "w_q", "w_kv", "kv_norm_g", "w_o", "w_up", "w_down",
             "mix_pre_g", "mix_post_g", "mlp_pre_g", "mlp_post_g"]
    out = [loss, grad_x]
    for k in range(4):
        out += [res[n][k] for n in order]
    return tuple(out)
```

```python
import functools

import jax
import jax.numpy as jnp
from jax import lax
from jax.experimental import pallas as pl
from jax.experimental.pallas import tpu as pltpu

F32 = jnp.float32
BF16 = jnp.bfloat16

EPS = 1e-6
HEAD_DIM = 64
LANES = 128
POOL_WINDOWS = (2, 4, 8, 16)
HALO = 16
N_DEV = 8
MESH_AXES = ("x", "y", "c")

ADAM_LR = 0.001
ADAM_B1 = 0.9
ADAM_B2 = 0.999
ADAM_EPS = 1e-08
ADAM_WD = 0.01
ADAM_STEP = 10

ROW_TILE = 512
ATT_TILE = 256
ROW_CHUNK = 32
GAIN_ROWS = 40


def _tile(n, want):
    return want if n % want == 0 else n


def matmul(a, b, *, name, ta=False, tb=False, a_layer=None, b_layer=None, out_dtype=F32,
           a_fn=None, epi=None, epi_in=None, out_scale=None, tm=1024, tn=1024, tk=1024):
    a2 = a.shape[1:] if a_layer is not None else a.shape
    b2 = b.shape[1:] if b_layer is not None else b.shape
    (K, M) = a2 if ta else a2[::-1]
    if not ta:
        M, K = a2
    if tb:
        N, Kb = b2
    else:
        Kb, N = b2
    assert K == Kb, (a.shape, b.shape)
    tm, tn, tk = _tile(M, tm), _tile(N, tn), _tile(K, tk)
    nk = K // tk
    grid = (M // tm, N // tn, nk)

    def lead(layer, shape, imap):
        if layer is None:
            return pl.BlockSpec(shape, imap)
        return pl.BlockSpec((None,) + shape, lambda i, j, k: (layer,) + imap(i, j, k))

    a_spec = lead(a_layer, (tk, tm) if ta else (tm, tk), (lambda i, j, k: (k, i)) if ta else (lambda i, j, k: (i, k)))
    b_spec = lead(b_layer, (tn, tk) if tb else (tk, tn), (lambda i, j, k: (j, k)) if tb else (lambda i, j, k: (k, j)))
    in_specs = [a_spec, b_spec]
    operands = [a, b]
    if epi is not None:
        in_specs.append(pl.BlockSpec((tm, tn), lambda i, j, k: (i, j)))
        operands.append(epi_in)
    out_shape = jax.ShapeDtypeStruct((M, N), out_dtype)
    out_spec = pl.BlockSpec((tm, tn), lambda i, j, k: (i, j))
    dims = (((0 if ta else 1,), (1 if tb else 0,)), ((), ()))
    n_in = len(operands)

    def body(*refs):
        a_ref, b_ref = refs[0], refs[1]
        e_ref = refs[2] if epi is not None else None
        o_ref = refs[n_in]

        def product():
            av = a_ref[...]
            if a_fn is not None:
                av = a_fn(av)
            return lax.dot_general(av.astype(BF16), b_ref[...].astype(BF16), dims, preferred_element_type=F32)

        def finish(r):
            if epi is not None:
                r = epi(r, e_ref[...])
            if out_scale is not None:
                r = r * out_scale
            o_ref[...] = r.astype(out_dtype)

        if nk == 1:
            finish(product())
            return
        acc_ref = refs[n_in + 1]
        k = pl.program_id(2)

        @pl.when(k == 0)
        def _():
            acc_ref[...] = product()

        @pl.when(k > 0)
        def _():
            acc_ref[...] += product()

        @pl.when(k == nk - 1)
        def _():
            finish(acc_ref[...])

    return pl.pallas_call(
        body, name=name, grid=grid, in_specs=in_specs, out_specs=out_spec, out_shape=out_shape,
        scratch_shapes=[pltpu.VMEM((tm, tn), F32)] if nk > 1 else [],
        compiler_params=pltpu.CompilerParams(dimension_semantics=("parallel", "parallel", "arbitrary")),
    )(*operands)


def _relu2(u):
    r = jnp.maximum(u.astype(F32), 0.0)
    return r * r


def _relu2_grad(acc, u):
    return acc * (2.0 * jnp.maximum(u.astype(F32), 0.0))


def rowwise(fn, rows, vecs, out_rows, n_acc, *, name, tile=ROW_TILE):
    S = rows[0].shape[0]
    tile = _tile(S, tile)
    n_rows, n_vecs, n_out = len(rows), len(vecs), len(out_rows)
    acc_cols = [None] * n_acc

    def body(*refs):
        ins = [r[...] for r in refs[:n_rows + n_vecs]]
        outs = refs[n_rows + n_vecs:]
        ro, ac = fn(*ins)
        assert len(ro) == n_out and len(ac) == n_acc
        for r, o in zip(outs[:n_out], ro):
            r[...] = o.astype(r.dtype)
        i = pl.program_id(0)
        for r, a in zip(outs[n_out:], ac):
            @pl.when(i == 0)
            def _():
                r[...] = jnp.zeros_like(r)
            r[...] += a

    acc_shapes = jax.eval_shape(
        lambda *xs: fn(*xs)[1],
        *[jax.ShapeDtypeStruct((tile, r.shape[1]), r.dtype) for r in rows],
        *[jax.ShapeDtypeStruct(v.shape, v.dtype) for v in vecs])
    in_specs = [pl.BlockSpec((tile, r.shape[1]), lambda i: (i, 0)) for r in rows]
    in_specs += [pl.BlockSpec(v.shape, lambda i: (0, 0)) for v in vecs]
    out_specs = [pl.BlockSpec((tile, c), lambda i: (i, 0)) for c, _ in out_rows]
    out_specs += [pl.BlockSpec(a.shape, lambda i: (0, 0)) for a in acc_shapes]
    out_shape = [jax.ShapeDtypeStruct((S, c), dt) for c, dt in out_rows]
    out_shape += [jax.ShapeDtypeStruct(a.shape, F32) for a in acc_shapes]
    del acc_cols
    return pl.pallas_call(
        body, name=name, grid=(S // tile,), in_specs=in_specs, out_specs=out_specs, out_shape=out_shape,
        compiler_params=pltpu.CompilerParams(dimension_semantics=("arbitrary",)),
    )(*rows, *vecs)


def _rms(x, g):
    r = lax.rsqrt(jnp.mean(x * x, axis=-1, keepdims=True) + EPS)
    return x * r * g


def _rms_bwd(x, g, dy):
    r = lax.rsqrt(jnp.mean(x * x, axis=-1, keepdims=True) + EPS)
    xh = x * r
    dyg = dy * g
    dx = r * (dyg - xh * jnp.mean(dyg * xh, axis=-1, keepdims=True))
    dg = jnp.sum(dy * xh, axis=0, keepdims=True)
    return dx, dg


def norm_only(x, g, dtype, *, name):
    D = x.shape[1]
    return rowwise(lambda xv, gv: ([_rms(xv, gv)], []), [x], [g], [(D, dtype)], 0, name=name)[0]


def residual_norms(x, m, g_post, next_gs, next_dtypes, *, name):
    D = x.shape[1]

    def fn(xv, mv, gp, *gs):
        xn = xv + _rms(mv, gp)
        return [xn] + [_rms(xn, g) for g in gs], []

    return rowwise(fn, [x, m], [g_post] + list(next_gs), [(D, F32)] + [(D, dt) for dt in next_dtypes], 0, name=name)


def residual_loss(x, d, g_post, target, *, name):
    D = x.shape[1]

    def fn(xv, dv, tv, gp):
        e = xv + _rms(dv, gp) - tv
        dy = e * (1.0 / D)
        dd, dg = _rms_bwd(dv, gp, dy)
        return [dy, dd], [jnp.sum(e * e, axis=0, keepdims=True) * (0.5 / D), dg]

    return rowwise(fn, [x, d, target], [g_post], [(D, F32), (D, BF16)], 2, name=name)


def mid_bwd(dy, x_mid, dh2, m, g_mlp_pre, g_mix_post, dm_dtype, *, name):
    D = dy.shape[1]

    def fn(dyv, xm, dh, mv, gpre, gpost):
        dx, dg_pre = _rms_bwd(xm, gpre, dh)
        dxm = dyv + dx
        dm, dg_post = _rms_bwd(mv, gpost, dxm)
        return [dxm, dm], [dg_pre, dg_post]

    return rowwise(fn, [dy, x_mid, dh2, m], [g_mlp_pre, g_mix_post], [(D, F32), (D, dm_dtype)], 2, name=name)


def pre_norm_bwd(dxm, x, dhs, gs, below=None, *, name):
    D = x.shape[1]
    n = len(dhs)
    rows = [dxm, x] + list(dhs) + ([below[0]] if below else [])
    vecs = list(gs) + ([below[1]] if below else [])

    def fn(*xs):
        dxv, xv, dh = xs[0], xs[1], xs[2:2 + n]
        g = xs[len(rows):]
        out, accs = dxv, []
        for k in range(n):
            dx, dg = _rms_bwd(xv, g[k], dh[k])
            out = out + dx
            accs.append(dg)
        if below is None:
            return [out], accs
        dd, dg = _rms_bwd(xs[2 + n], g[n], out)
        return [out, dd], accs + [dg]

    return rowwise(fn, rows, vecs, [(D, F32)] + ([(D, BF16)] if below else []), n + (1 if below else 0), name=name)


def sum_concat_cast(pairs, dtype, *, name):
    C = pairs[0][0].shape[1]
    n = len(pairs)

    def fn(*xs):
        return [jnp.concatenate([sum(xs[:n]), sum(xs[n:])], axis=1)], []

    return rowwise(fn, [a for a, _ in pairs] + [b for _, b in pairs], [], [(2 * C, dtype)], 0, name=name)[0]


def _window_sum(e, window, total_rows, backward):
    s, k = e, 1
    while k < window:
        s = s + pltpu.roll(s, (total_rows - k) if backward else k, 0)
        k *= 2
    return s


def pool_fwd(h, w, scale, *, name):
    S, D = h.shape
    G = len(POOL_WINDOWS)
    GC = D // G
    tile = _tile(S, ROW_TILE)
    hb = tile // HALO

    def body(hc_ref, hp_ref, w_ref, sc_ref, o_ref):
        i = pl.program_id(0)
        prev = jnp.where(i > 0, hp_ref[...], 0.0)
        ext = jnp.concatenate([prev, hc_ref[...]], axis=0)
        t = i * tile + lax.broadcasted_iota(jnp.int32, (tile, 1), 0)
        outs = []
        for g, window in enumerate(POOL_WINDOWS):
            e = ext[:, g * GC:(g + 1) * GC]
            s = _window_sum(e, window, HALO + tile, False)[HALO:, :]
            cnt = jnp.minimum(t + 1, window).astype(F32)
            y = s / cnt - e[HALO:, :]
            outs.append(jnp.dot(y.astype(BF16), w_ref[g], preferred_element_type=F32))
        o_ref[...] = jnp.concatenate(outs, axis=1) * sc_ref[...]

    return pl.pallas_call(
        body, name=name, grid=(S // tile,),
        in_specs=[pl.BlockSpec((tile, D), lambda i: (i, 0)),
                  pl.BlockSpec((HALO, D), lambda i: (jnp.maximum(i * hb - 1, 0), 0)),
                  pl.BlockSpec((G, GC, GC), lambda i: (0, 0, 0)),
                  pl.BlockSpec((1, D), lambda i: (0, 0))],
        out_specs=pl.BlockSpec((tile, D), lambda i: (i, 0)),
        out_shape=jax.ShapeDtypeStruct((S, D), F32),
        compiler_params=pltpu.CompilerParams(dimension_semantics=("parallel",)),
    )(h, h, w, scale)


def pool_bwd(h, dm, w, scale, *, name):
    S, D = h.shape
    G = len(POOL_WINDOWS)
    GC = D // G
    tile = _tile(S, ROW_TILE)
    hb = tile // HALO
    n_tiles = S // tile
    last_halo = S // HALO - 1

    def body(hc_ref, hp_ref, dmc_ref, dmn_ref, w_ref, sc_ref, dh_ref, dw_ref, dsc_ref):
        i = pl.program_id(0)

        @pl.when(i == 0)
        def _():
            dw_ref[...] = jnp.zeros_like(dw_ref)
            dsc_ref[...] = jnp.zeros_like(dsc_ref)

        prev = jnp.where(i > 0, hp_ref[...], 0.0)
        ext = jnp.concatenate([prev, hc_ref[...]], axis=0)
        nxt = jnp.where(i < n_tiles - 1, dmn_ref[...], 0.0)
        dmc = dmc_ref[...]
        dm_ext = jnp.concatenate([dmc, nxt], axis=0)
        t = i * tile + lax.broadcasted_iota(jnp.int32, (tile, 1), 0)
        t_ext = i * tile + lax.broadcasted_iota(jnp.int32, (tile + HALO, 1), 0)
        dhs, dscs = [], []
        for g, window in enumerate(POOL_WINDOWS):
            cols = slice(g * GC, (g + 1) * GC)
            e = ext[:, cols]
            s = _window_sum(e, window, HALO + tile, False)[HALO:, :]
            y = (s / jnp.minimum(t + 1, window).astype(F32) - e[HALO:, :]).astype(BF16)
            wg = w_ref[g]
            ypre = jnp.dot(y, wg, preferred_element_type=F32)
            dscs.append(jnp.sum(dmc[:, cols] * ypre, axis=0, keepdims=True))
            dyp = (dm_ext[:, cols] * sc_ref[:, cols]).astype(BF16)
            dw_ref[g] += lax.dot_general(y, dyp[:tile, :], (((0,), (0,)), ((), ())), preferred_element_type=F32)
            dy = lax.dot_general(dyp, wg, (((1,), (1,)), ((), ())), preferred_element_type=F32)
            r = dy / jnp.minimum(t_ext + 1, window).astype(F32)
            sr = _window_sum(r, window, tile + HALO, True)
            dhs.append(sr[:tile, :] - dy[:tile, :])
        dh_ref[...] = jnp.concatenate(dhs, axis=1)
        dsc_ref[...] += jnp.concatenate(dscs, axis=1)

    return pl.pallas_call(
        body, name=name, grid=(n_tiles,),
        in_specs=[pl.BlockSpec((tile, D), lambda i: (i, 0)),
                  pl.BlockSpec((HALO, D), lambda i: (jnp.maximum(i * hb - 1, 0), 0)),
                  pl.BlockSpec((tile, D), lambda i: (i, 0)),
                  pl.BlockSpec((HALO, D), lambda i: (jnp.minimum((i + 1) * hb, last_halo), 0)),
                  pl.BlockSpec((G, GC, GC), lambda i: (0, 0, 0)),
                  pl.BlockSpec((1, D), lambda i: (0, 0))],
        out_specs=[pl.BlockSpec((tile, D), lambda i: (i, 0)),
                   pl.BlockSpec((G, GC, GC), lambda i: (0, 0, 0)),
                   pl.BlockSpec((1, D), lambda i: (0, 0))],
        out_shape=[jax.ShapeDtypeStruct((S, D), F32), jax.ShapeDtypeStruct((G, GC, GC), F32),
                   jax.ShapeDtypeStruct((1, D), F32)],
        compiler_params=pltpu.CompilerParams(dimension_semantics=("arbitrary",)),
    )(h, h, dm, dm, w, scale)


ATT_LANES = 256
N_PAIR = ATT_LANES // HEAD_DIM
KEY_TILES = 3
KEY_TILES_BWD = 3


def _scores(xs, kj):
    return [lax.dot_general(x, kj, (((1,), (1,)), ((), ())), preferred_element_type=F32) for x in xs]


def _softplus_parts(z, mask):
    sp = jnp.maximum(z, 0.0) + jnp.log(1.0 + jnp.exp(-jnp.abs(z)))
    logb = z - sp
    if mask is not None:
        sp = jnp.where(mask, sp, 0.0)
    return logb, sp.astype(BF16), jnp.sum(sp, axis=1, keepdims=True)


def _weights(logb, later, c, mask):
    a = jnp.exp(logb - (later + c))
    return a if mask is None else jnp.where(mask, a, 0.0)


def _tile_weights(zss, cs, u_later, mask):
    partss = [[_softplus_parts(z, mask) for z in zs] for zs in zss]
    laterss = [[jnp.dot(sp, u_later, preferred_element_type=F32) for _, sp, _ in parts] for parts in partss]
    out = []
    for parts, laters in zip(partss, laterss):
        out.append(([p[0] for p in parts], [_weights(p[0], later, c, mask) for p, later, c in zip(parts, laters, cs)]))
        cs = [c + p[2] for c, p in zip(cs, parts)]
    return out, tuple(cs)


def _tri(T, later):
    rows = lax.broadcasted_iota(jnp.int32, (T, T), 0)
    cols = lax.broadcasted_iota(jnp.int32, (T, T), 1)
    return jnp.where((rows > cols) if later else (rows < cols), 1.0, 0.0).astype(BF16)


def _head_masks(x2, axis=None):
    lane = lax.broadcasted_iota(jnp.int32, (1, ATT_LANES), 1)
    parts = [jnp.where((lane // HEAD_DIM) == hh, x2, jnp.zeros_like(x2)) for hh in range(N_PAIR)]
    return parts if axis is None else jnp.concatenate(parts, axis=axis)


def _cat_bf16(parts, axis):
    return jnp.concatenate([p.astype(BF16) for p in parts], axis=axis)


def _side_split(side, refs, n_in, n_out):
    if side is None:
        return refs, None
    n_src, n_buf = len(side.srcs), len(side.bufs)
    ins, rest = refs[:n_in], refs[n_in:]
    src_refs, rest = rest[:n_src], rest[n_src + n_buf:]
    outs, rest = rest[:n_out], rest[n_out:]
    buf_refs, rest = rest[:n_buf], rest[n_buf:]
    own_scratch, sems = rest[:len(rest) - 3], rest[len(rest) - 3:]
    return tuple(ins) + tuple(outs) + tuple(own_scratch), (src_refs, buf_refs, sems)


def _side_phase(side, side_refs, phase, when):
    if side is None:
        return

    @pl.when(when)
    def _():
        side.copies(phase, *side_refs)


def attn_fwd(q, kv, *, name, side=None):
    S, D = q.shape
    P = D // ATT_LANES
    T = _tile(S, ATT_TILE)
    nq = S // T

    def body(*refs):
        (q_ref, k_ref, v_ref, o_ref), side_refs = _side_split(side, refs, 3, 1)
        i = pl.program_id(1)
        p = pl.program_id(0)
        _side_phase(side, side_refs, 0, (p == 0) & (i == 0))
        u_later = _tri(T, True)
        qhs = _head_masks(q_ref[...])
        diag = lax.broadcasted_iota(jnp.int32, (T, T), 1) < lax.broadcasted_iota(jnp.int32, (T, T), 0)

        def rows_of(ref, j):
            return ref[pl.ds(pl.multiple_of(j * T, T), T), :]

        def step(js, carry, mask):
            cs, acc = carry
            zss = [_scores(qhs, rows_of(k_ref, j)) for j in js]
            vcat = jnp.concatenate([_head_masks(rows_of(v_ref, j), axis=0) for j in js], axis=0)
            per_tile, cs = _tile_weights(zss, cs, u_later, mask)
            acat = _cat_bf16([a for _, aa in per_tile for a in aa], 1)
            return cs, acc + jnp.dot(acat, vcat, preferred_element_type=F32)

        carry = step([i], ((jnp.zeros((T, 1), F32),) * N_PAIR, jnp.zeros((T, ATT_LANES), F32)), diag)
        carry = lax.fori_loop(0, i % KEY_TILES, lambda n, cr: step([i - 1 - n], cr, None), carry)
        first = i - 1 - i % KEY_TILES
        _, acc = lax.fori_loop(0, i // KEY_TILES,
                               lambda n, cr: step([first - KEY_TILES * n - t for t in range(KEY_TILES)], cr, None), carry)
        o_ref[...] = acc.astype(o_ref.dtype)
        _side_phase(side, side_refs, 1, (p == P - 1) & (i == nq - 1))

    sd = side
    outs = pl.pallas_call(
        body, name=name, grid=(P, nq),
        in_specs=[pl.BlockSpec((T, ATT_LANES), lambda p, i: (i, p)),
                  pl.BlockSpec((S, ATT_LANES), lambda p, i: (0, p)),
                  pl.BlockSpec((S, ATT_LANES), lambda p, i: (0, P + p))] + (sd.specs() if sd else []),
        out_specs=[pl.BlockSpec((T, ATT_LANES), lambda p, i: (i, p))] + (sd.out_specs() if sd else []),
        out_shape=[jax.ShapeDtypeStruct((S, D), BF16)] + (sd.out_shape() if sd else []),
        scratch_shapes=sd.scratch() if sd else [],
        input_output_aliases=sd.aliases(3, 1) if sd else {},
        compiler_params=pltpu.CompilerParams(dimension_semantics=("arbitrary", "arbitrary"),
                                             has_side_effects=sd is not None),
    )(q, kv, kv, *(sd.operands() if sd else []))
    return outs[0] if sd is None else (outs[0], sd.result(outs[1:]))


def attn_bwd(q, kv, do, *, name, side=None):
    S, D = q.shape
    P = D // ATT_LANES
    T = _tile(S, ATT_TILE)
    nb = S // T

    def body(*refs):
        (q_ref, k_ref, v_ref, do_ref, dq_ref, dk_ref, dv_ref, g_scr, s_scr), side_refs = _side_split(side, refs, 4, 3)
        i = pl.program_id(1)
        p = pl.program_id(0)
        _side_phase(side, side_refs, 0, (p == 0) & (i == 0))

        @pl.when(i == 0)
        def _():
            dk_ref[...] = jnp.zeros_like(dk_ref)
            dv_ref[...] = jnp.zeros_like(dv_ref)

        u_later = _tri(T, True)
        u_earlier = _tri(T, False)
        qhs = _head_masks(q_ref[...])
        dohs = _head_masks(do_ref[...])
        qcat = jnp.concatenate(qhs, axis=0)
        docat = jnp.concatenate(dohs, axis=0)
        diag = lax.broadcasted_iota(jnp.int32, (T, T), 1) < lax.broadcasted_iota(jnp.int32, (T, T), 0)
        tdot = (((0,), (0,)), ((), ()))

        def rows_of(ref, j):
            return ref.at[pl.ds(pl.multiple_of(j * T, T), T), :]

        def step1(js, cs, mask):
            dass = [_scores(dohs, rows_of(v_ref, j)[...]) for j in js]
            zss = [_scores(qhs, rows_of(k_ref, j)[...]) for j in js]
            per_tile, cs = _tile_weights(zss, cs, u_later, mask)
            for j, das, (logbs, aa) in zip(js, dass, per_tile):
                for hh in range(N_PAIR):
                    g_scr[hh, j] = (das[hh] * aa[hh]).astype(BF16)
                    sg = jnp.exp(logbs[hh])
                    if mask is not None:
                        sg = jnp.where(mask, sg, 0.0)
                    s_scr[hh, j] = sg.astype(BF16)
            for j, (_, aa) in zip(js, per_tile):
                rows_of(dv_ref, j)[...] += lax.dot_general(_cat_bf16(aa, 0), docat, tdot,
                                                          preferred_element_type=F32)
            return cs

        G = KEY_TILES_BWD
        cs = step1([i], (jnp.zeros((T, 1), F32),) * N_PAIR, diag)
        cs = lax.fori_loop(0, i % G, lambda n, c: step1([i - 1 - n], c, None), cs)
        first = i - 1 - i % G
        lax.fori_loop(0, i // G, lambda n, c: step1([first - G * n - t for t in range(G)], c, None), cs)

        def step2(js, carry):
            cs, acc = carry
            cumss = [[jnp.dot(g_scr[hh, j], u_earlier, preferred_element_type=F32) for hh in range(N_PAIR)]
                     for j in js]
            kcat = jnp.concatenate([_head_masks(rows_of(k_ref, j)[...], axis=0) for j in js], axis=0)
            dzss = []
            for j, cums in zip(js, cumss):
                dzs, new = [], []
                for hh in range(N_PAIR):
                    gf = g_scr[hh, j].astype(F32)
                    sg = s_scr[hh, j].astype(F32)
                    dzs.append((gf - sg * (gf + (cums[hh] + cs[hh]))).astype(BF16))
                    new.append(cs[hh] + jnp.sum(gf, axis=1, keepdims=True))
                cs = tuple(new)
                dzss.append(dzs)
            acc = acc + jnp.dot(jnp.concatenate([dz for dzs in dzss for dz in dzs], axis=1), kcat,
                                preferred_element_type=F32)
            for j, dzs in zip(js, dzss):
                rows_of(dk_ref, j)[...] += lax.dot_general(jnp.concatenate(dzs, axis=0), qcat, tdot,
                                                          preferred_element_type=F32)
            return cs, acc

        carry = ((jnp.zeros((T, 1), F32),) * N_PAIR, jnp.zeros((T, ATT_LANES), F32))
        carry = lax.fori_loop(0, (i + 1) // G, lambda n, cr: step2([G * n + t for t in range(G)], cr), carry)
        done = (i + 1) // G * G
        _, dq = lax.fori_loop(0, (i + 1) % G, lambda n, cr: step2([done + n], cr), carry)
        dq_ref[...] = (dq * (HEAD_DIM ** -0.5)).astype(dq_ref.dtype)
        _side_phase(side, side_refs, 1, (p == P - 1) & (i == nb - 1))

    sd = side
    outs = pl.pallas_call(
        body, name=name, grid=(P, nb),
        in_specs=[pl.BlockSpec((T, ATT_LANES), lambda p, i: (i, p)),
                  pl.BlockSpec((S, ATT_LANES), lambda p, i: (0, p)),
                  pl.BlockSpec((S, ATT_LANES), lambda p, i: (0, P + p)),
                  pl.BlockSpec((T, ATT_LANES), lambda p, i: (i, p))] + (sd.specs() if sd else []),
        out_specs=[pl.BlockSpec((T, ATT_LANES), lambda p, i: (i, p)),
                   pl.BlockSpec((S, ATT_LANES), lambda p, i: (0, p)),
                   pl.BlockSpec((S, ATT_LANES), lambda p, i: (0, p))] + (sd.out_specs() if sd else []),
        out_shape=[jax.ShapeDtypeStruct((S, D), BF16), jax.ShapeDtypeStruct((S, D), F32),
                   jax.ShapeDtypeStruct((S, D), F32)] + (sd.out_shape() if sd else []),
        scratch_shapes=[pltpu.VMEM((N_PAIR, nb, T, T), BF16), pltpu.VMEM((N_PAIR, nb, T, T), BF16)]
        + (sd.scratch() if sd else []),
        input_output_aliases=sd.aliases(4, 3) if sd else {},
        compiler_params=pltpu.CompilerParams(dimension_semantics=("arbitrary", "arbitrary"),
                                             has_side_effects=sd is not None),
    )(q, kv, kv, do, *(sd.operands() if sd else []))
    return tuple(outs[:3]) if sd is None else (*outs[:3], sd.result(outs[3:]))


def _window(ref, axis, dev, n):
    return ref.at[(slice(None),) * axis + (pl.ds(dev * n, n),)]


def all_gather(ops, *, name):
    n_ops = len(ops)
    out_shape = []
    for a, ax in ops:
        shp = list(a.shape)
        shp[ax] *= N_DEV
        out_shape.append(jax.ShapeDtypeStruct(tuple(shp), a.dtype))

    def body(*refs):
        ins, outs = refs[:n_ops], refs[n_ops:2 * n_ops]
        send_sems, recv_sems, local_sems = refs[2 * n_ops:]
        x, y, c = (lax.axis_index(n) for n in MESH_AXES)
        me, sibling = (x, y, c), (x, y, 1 - c)
        chips = [(1 - x, y), (x, 1 - y), (1 - x, 1 - y)]

        def rows(o, dev):
            px, py, pc = dev
            ax = ops[o][1]
            return _window(outs[o], ax, 4 * px + 2 * py + pc, ops[o][0].shape[ax])

        def copy(o, k, block, to, src=None):
            return pltpu.make_async_remote_copy(
                src_ref=rows(o, block) if src is None else src, dst_ref=rows(o, block),
                send_sem=send_sems.at[o, k], recv_sem=recv_sems.at[o, k],
                device_id=to, device_id_type=pl.DeviceIdType.MESH)

        mine, first, passed = [], [], []
        for o in range(n_ops):
            cp = pltpu.make_async_copy(ins[o], rows(o, me), local_sems.at[o])
            cp.start()
            mine.append(cp)
            first.append(copy(o, 0, me, sibling, src=ins[o]))
            first += [copy(o, 1 + j, me, (*chip, c), src=ins[o]) for j, chip in enumerate(chips)]
        for cp in first:
            cp.start()
        for j, chip in enumerate(chips):
            for o in range(n_ops):
                copy(o, 1 + j, (*chip, c), me).wait_recv()
                cp = copy(o, 4 + j, (*chip, c), sibling)
                cp.start()
                passed.append(cp)
        for o in range(n_ops):
            copy(o, 0, sibling, me).wait_recv()
            for j, chip in enumerate(chips):
                copy(o, 4 + j, (*chip, 1 - c), me).wait_recv()
        for cp in first + passed:
            cp.wait_send()
        for cp in mine:
            cp.wait()

    any_spec = pl.BlockSpec(memory_space=pl.ANY)
    return pl.pallas_call(
        body, name=name, in_specs=[any_spec] * n_ops, out_specs=[any_spec] * n_ops, out_shape=out_shape,
        scratch_shapes=[pltpu.SemaphoreType.DMA((n_ops, 7)), pltpu.SemaphoreType.DMA((n_ops, 7)),
                        pltpu.SemaphoreType.DMA((n_ops,))],
        compiler_params=pltpu.CompilerParams(has_side_effects=True),
    )(*[a for a, _ in ops])


class Side:
    def __init__(self, kind, items, bufs):
        self.kind, self.items = kind, items
        self.names = list(bufs)
        self.bufs = [bufs[n] for n in self.names]
        self.srcs = [it[0] for it in items]

    def operands(self):
        return self.srcs + self.bufs

    def specs(self):
        return [pl.BlockSpec(memory_space=pl.ANY)] * (len(self.srcs) + len(self.bufs))

    def out_specs(self):
        return [pl.BlockSpec(memory_space=pl.ANY)] * len(self.bufs)

    def out_shape(self):
        return [jax.ShapeDtypeStruct(b.shape, b.dtype) for b in self.bufs]

    def aliases(self, first_in, first_out):
        return {first_in + len(self.srcs) + k: first_out + k for k in range(len(self.bufs))}

    def scratch(self):
        n = len(self.items)
        return [pltpu.SemaphoreType.DMA((n, N_DEV - 1)), pltpu.SemaphoreType.DMA((n, N_DEV - 1)),
                pltpu.SemaphoreType.DMA((n,))]

    def result(self, outs):
        return dict(zip(self.names, outs))

    def copies(self, phase, src_refs, buf_refs, sems):
        send_sems, recv_sems, local_sems = sems
        pos = tuple(lax.axis_index(n) for n in MESH_AXES)
        me = 4 * pos[0] + 2 * pos[1] + pos[2]
        for o, (_, ax, name, layer) in enumerate(self.items):
            src, buf = src_refs[o], buf_refs[self.names.index(name)]
            if self.kind == "gather":
                whole = buf if layer is None else buf.at[layer]
                n = src.shape[ax]
                sent = lambda dev, src=src: src
                lands = lambda dev, whole=whole, ax=ax, n=n: _window(whole, ax, dev, n)
            else:
                n = src.shape[ax] // N_DEV
                sent = lambda dev, src=src, ax=ax, n=n: _window(src, ax, dev, n)
                lands = lambda dev, buf=buf, layer=layer: buf.at[dev] if layer is None else buf.at[dev, layer]
            if local_sems is not None:
                local = pltpu.make_async_copy(sent(me), lands(me), local_sems.at[o])
                if phase == 0:
                    local.start()
                else:
                    local.wait()
            for r in range(1, N_DEV):
                peer = tuple(1 - p if r & bit else p for p, bit in zip(pos, (4, 2, 1)))
                pid = 4 * peer[0] + 2 * peer[1] + peer[2]
                cp = pltpu.make_async_remote_copy(
                    src_ref=sent(pid) if phase == 0 else sent(me), dst_ref=lands(me) if phase == 0 else lands(pid),
                    send_sem=send_sems[o] if isinstance(send_sems, (list, tuple)) else send_sems.at[o, r - 1],
                    recv_sem=recv_sems[o] if isinstance(recv_sems, (list, tuple)) else recv_sems.at[o, r - 1],
                    device_id=peer, device_id_type=pl.DeviceIdType.MESH)
                if phase == 0:
                    cp.start()
                else:
                    cp.wait_recv()
                    cp.wait_send()


def _hbm(a):
    return pltpu.with_memory_space_constraint(a, pltpu.HBM)


def own_blocks(items, bufs):
    me = 4 * lax.axis_index("x") + 2 * lax.axis_index("y") + lax.axis_index("c")
    bufs = {name: bufs[name] for _, _, name, _ in items}
    for src, ax, name, layer in items:
        n = src.shape[ax] // N_DEV
        blk = lax.dynamic_slice_in_dim(src, me * n, n, axis=ax)
        lead = (me,) if layer is None else (me, layer)
        blk = blk.reshape((1,) * len(lead) + blk.shape)
        bufs[name] = lax.dynamic_update_slice(bufs[name], blk, lead + (0,) * src.ndim)
    return bufs


def start_side(side, *, name):
    n_src, n_buf = len(side.srcs), len(side.bufs)
    n = len(side.items)

    def body(*refs):
        src_refs, buf_refs = refs[:n_src], refs[n_src:n_src + n_buf]
        sems = refs[n_src + n_buf:n_src + n_buf + 2 * n]
        token = refs[-1]
        side.copies(0, src_refs, buf_refs, (list(sems[:n]), list(sems[n:]), None))
        token[...] = jnp.zeros_like(token)

    hbm = pl.BlockSpec(memory_space=pltpu.HBM)
    sem = pl.BlockSpec(memory_space=pltpu.SEMAPHORE)
    operands = side.operands()
    outs = pl.pallas_call(
        body, name=name,
        out_shape=(*[pltpu.SemaphoreType.DMA(())] * (2 * n),
                   *[pltpu.HBM(a.shape, a.dtype) for a in operands], jax.ShapeDtypeStruct((8, LANES), F32)),
        in_specs=[hbm] * len(operands),
        out_specs=(*[sem] * (2 * n), *[hbm] * len(operands), pl.BlockSpec(memory_space=pltpu.VMEM)),
        input_output_aliases={k: 2 * n + k for k in range(len(operands))},
        compiler_params=pltpu.CompilerParams(has_side_effects=pltpu.SideEffectType.DATAFLOW_SIDE_EFFECTING),
    )(*[_hbm(a) for a in operands])
    started = (side, list(outs[:n]), list(outs[n:2 * n]), list(outs[2 * n:2 * n + n_src]))
    return started, dict(zip(side.names, outs[2 * n + n_src:2 * n + n_src + n_buf])), outs[-1]


def wait_sides(started, bufs, after, *, name):
    names = list(bufs)
    flat = []
    for side, send_sems, recv_sems, srcs in started:
        flat += [*srcs, *send_sems, *recv_sems]
    n_buf = len(names)

    def body(*refs):
        buf_refs = refs[:n_buf]
        rest = refs[n_buf:]
        for side, _, _, srcs in started:
            n = len(srcs)
            src_refs, send_sems, recv_sems = rest[:n], list(rest[n:2 * n]), list(rest[2 * n:3 * n])
            rest = rest[3 * n:]
            side.copies(1, src_refs, [buf_refs[names.index(nm)] for nm in side.names], (send_sems, recv_sems, None))

    hbm = pl.BlockSpec(memory_space=pltpu.HBM)
    sem = pl.BlockSpec(memory_space=pltpu.SEMAPHORE)
    specs = []
    for _, _, _, srcs in started:
        specs += [hbm] * len(srcs) + [sem] * (2 * len(srcs))
    outs = pl.pallas_call(
        body, name=name,
        out_shape=tuple(pltpu.HBM(bufs[nm].shape, bufs[nm].dtype) for nm in names),
        in_specs=[hbm] * n_buf + specs + [pl.BlockSpec(memory_space=pl.ANY)],
        out_specs=tuple([hbm] * n_buf),
        input_output_aliases={k: k for k in range(n_buf)},
        compiler_params=pltpu.CompilerParams(has_side_effects=pltpu.SideEffectType.DATAFLOW_SIDE_EFFECTING),
    )(*[bufs[nm] for nm in names], *flat, after)
    return dict(zip(names, outs))


def _adamw(w, g, m, v):
    m = ADAM_B1 * m + (1.0 - ADAM_B1) * g
    v = ADAM_B2 * v + (1.0 - ADAM_B2) * (g * g)
    m_hat = m / (1.0 - ADAM_B1 ** ADAM_STEP)
    v_hat = v / (1.0 - ADAM_B2 ** ADAM_STEP)
    delta = -ADAM_LR * (m_hat / (jnp.sqrt(v_hat) + ADAM_EPS) + ADAM_WD * w)
    return delta, m, v


def reduce_adamw(parts, w, m, v, *, name):
    shape = w.shape
    C = shape[-1]
    R = w.size // C
    tile = _tile(R, 256)

    def body(p_ref, w_ref, m_ref, v_ref, g_ref, d_ref, nm_ref, nv_ref):
        g = p_ref[0].astype(F32)
        for s in range(1, N_DEV):
            g = g + p_ref[s].astype(F32)
        d, nm, nv = _adamw(w_ref[...], g, m_ref[...], v_ref[...])
        g_ref[...] = g
        d_ref[...] = d
        nm_ref[...] = nm
        nv_ref[...] = nv

    row = pl.BlockSpec((tile, C), lambda i: (i, 0))
    outs = pl.pallas_call(
        body, name=name, grid=(R // tile,),
        in_specs=[pl.BlockSpec((N_DEV, tile, C), lambda i: (0, i, 0)), row, row, row],
        out_specs=[row] * 4, out_shape=[jax.ShapeDtypeStruct((R, C), F32)] * 4,
        compiler_params=pltpu.CompilerParams(dimension_semantics=("parallel",)),
    )(parts.reshape(N_DEV, R, C), w.reshape(R, C), m.reshape(R, C), v.reshape(R, C))
    return [o.reshape(shape) for o in outs]


def gains_adamw(parts, groups, *, name):
    n = len(groups)

    def body(*refs):
        p_ref = refs[0]
        ins, outs = refs[1:1 + 3 * n], refs[1 + 3 * n:]
        for k in range(n):
            w_ref, m_ref, v_ref = ins[3 * k:3 * k + 3]
            L = w_ref.shape[0]
            g = p_ref[pl.ds(8 * k, L), :]
            for s in range(1, N_DEV):
                g = g + p_ref[pl.ds(s * GAIN_ROWS + 8 * k, L), :]
            d, nm, nv = _adamw(w_ref[...], g, m_ref[...], v_ref[...])
            for r, val in zip(outs[4 * k:4 * k + 4], (g, d, nm, nv)):
                r[...] = val

    flat = [a for grp in groups for a in grp]
    out_shape = [jax.ShapeDtypeStruct(grp[0].shape, F32) for grp in groups for _ in range(4)]
    outs = pl.pallas_call(body, name=name, out_shape=out_shape)(parts, *flat)
    return [outs[4 * k:4 * k + 4] for k in range(n)]


def kernel(x, pool_w, pool_scale, w_q, w_kv, kv_norm_g, w_o, w_up, w_down, mix_pre_g, mix_post_g, mlp_pre_g, mlp_post_g, loss_target, m_pool_w, m_pool_scale, m_w_q, m_w_kv, m_kv_norm_g, m_w_o, m_w_up, m_w_down, m_mix_pre_g, m_mix_post_g, m_mlp_pre_g, m_mlp_post_g, v_pool_w, v_pool_scale, v_w_q, v_w_kv, v_kv_norm_g, v_w_o, v_w_up, v_w_down, v_mix_pre_g, v_mix_post_g, v_mlp_pre_g, v_mlp_post_g):
    _, S, D = x.shape
    x0 = x.reshape(S, D)
    target = loss_target.reshape(S, D)
    depth = w_up.shape[0]
    n_pool = pool_w.shape[0]
    F = w_up.shape[2] * N_DEV
    G = pool_w.shape[1]
    GC = D // G

    def vec(a, l):
        return a[l].reshape(1, D)

    n_att = depth - n_pool
    wq_s, wkv_s, wo_s = w_q.astype(BF16), w_kv.astype(BF16), w_o.astype(BF16)
    wup_s, wdn_s, pw_s = w_up.astype(BF16), w_down.astype(BF16), pool_w.astype(BF16)
    wq0, wkv, wup_a, wdn_a, pw, psc = all_gather(
        [(wq_s[0], 0), (wkv_s, 1), (wup_s[:n_pool], 2), (wdn_s[:n_pool], 1), (pw_s, 2), (pool_scale, 1)],
        name="gather_weights")
    late = Side("gather",
                [(wq_s[1:], 1, "wq", None), (wo_s, 1, "wo", None), (wup_s[n_pool:], 2, "wup", None),
                 (wdn_s[n_pool:], 1, "wdn", None)],
                {"wq": lax.empty((n_att - 1, D, D), BF16), "wo": lax.empty((n_att, D, D), BF16),
                 "wup": lax.empty((n_att, D, F), BF16), "wdn": lax.empty((n_att, F, D), BF16)})
    late_w = None

    def layer_of(early, key, l, n_early):
        return (early, l) if l < n_early else (late_w[key], l - n_early)

    saved = []
    xs = x0
    h1 = norm_only(xs, vec(mix_pre_g, 0), F32, name="norm_in")
    kv = hk = None
    dy = loss_rows = None
    for l in range(depth):
        is_pool = l < n_pool
        st = {"x": xs, "h1": h1}
        if is_pool:
            m = pool_fwd(h1, pw[l], psc[l].reshape(1, D), name=f"pool_fwd{l}")
        else:
            j = l - n_pool
            wq_j, wq_l = (wq0, None) if j == 0 else (late_w["wq"], j - 1)
            q = matmul(h1, wq_j, b_layer=wq_l, out_dtype=BF16, out_scale=HEAD_DIM ** -0.5, name=f"q_proj{j}")
            if j == 0:
                o, late_w = attn_fwd(q, kv, name=f"attn_fwd{j}", side=late)
            else:
                o = attn_fwd(q, kv, name=f"attn_fwd{j}")
            m = matmul(o, late_w["wo"], b_layer=j, name=f"o_proj{j}")
            st.update(q=q, o=o)
        x_mid, h2 = residual_norms(xs, m, vec(mix_post_g, l), [vec(mlp_pre_g, l)], [BF16], name=f"mix_out{l}")
        wup, lu = layer_of(wup_a, "wup", l, n_pool)
        wdn, ld = layer_of(wdn_a, "wdn", l, n_pool)
        u = matmul(h2, wup, b_layer=lu, out_dtype=BF16, name=f"mlp_up{l}")
        d = matmul(u, wdn, b_layer=ld, a_fn=_relu2, name=f"mlp_down{l}")
        st.update(m=m, x_mid=x_mid, h2=h2, u=u, d=d)
        saved.append(st)
        if l == depth - 1:
            dy, dd, loss_rows, dg_last = residual_loss(x_mid, d, vec(mlp_post_g, l), target, name="loss")
        elif l == n_pool - 1:
            xs, h1, hk = residual_norms(x_mid, d, vec(mlp_post_g, l), [vec(mix_pre_g, l + 1), kv_norm_g.reshape(1, D)],
                                        [BF16, BF16], name=f"mlp_out{l}")
            kv = matmul(hk, wkv, out_dtype=BF16, name="kv_proj")
        else:
            nxt_dt = F32 if l + 1 < n_pool else BF16
            xs, h1 = residual_norms(x_mid, d, vec(mlp_post_g, l), [vec(mix_pre_g, l + 1)], [nxt_dt], name=f"mlp_out{l}")
    loss = lax.psum(jnp.sum(loss_rows), MESH_AXES)

    recv = {"w_q": (w_q, BF16), "w_kv": (w_kv, BF16), "w_o": (w_o, BF16), "w_up": (w_up, BF16),
            "w_down": (w_down, BF16), "pool_w": (pool_w, F32), "pool_scale": (pool_scale, F32)}
    recv = {n: lax.empty((N_DEV,) + w.shape, dt) for n, (w, dt) in recv.items()}
    pending = []

    def send_pending():
        return Side("scatter", list(pending), {n: recv[n] for n in dict.fromkeys(it[2] for it in pending)})

    started = []

    def start_pending(tag, then):
        recv.update(own_blocks(pending, recv))
        begun, got, token = start_side(send_pending(), name=f"exchange_start_{tag}")
        recv.update(got)
        started.append(begun)
        pending.clear()
        return lax.optimization_barrier((then, token))[0]

    g_pw, g_psc = [None] * n_pool, [None] * n_pool
    gains = {k: [None] * depth for k in ("mix_pre", "mix_post", "mlp_pre", "mlp_post")}
    gains["mlp_post"][depth - 1] = dg_last
    dkvs = []
    g_kvn = None
    for l in reversed(range(depth)):
        st = saved[l]
        is_pool = l < n_pool
        wup, lu = layer_of(wup_a, "wup", l, n_pool)
        wdn, ld = layer_of(wdn_a, "wdn", l, n_pool)
        du = matmul(dd, wdn, b_layer=ld, tb=True, out_dtype=BF16, epi=_relu2_grad, epi_in=st["u"], name=f"mlp_du{l}")
        g_wdn = matmul(st["u"], dd, ta=True, a_fn=_relu2, out_dtype=BF16, name=f"mlp_dwdn{l}")
        g_wup = matmul(st["h2"], du, ta=True, out_dtype=BF16, name=f"mlp_dwup{l}")
        pending += [(g_wup, 1, "w_up", l), (g_wdn, 0, "w_down", l)]
        if is_pool:
            du = start_pending(f"l{l}", du)
        dh2 = matmul(du, wup, b_layer=lu, tb=True, name=f"mlp_dh{l}")
        dxm, dm, gains["mlp_pre"][l], gains["mix_post"][l] = mid_bwd(
            dy, st["x_mid"], dh2, st["m"], vec(mlp_pre_g, l), vec(mix_post_g, l), F32 if is_pool else BF16,
            name=f"mid_bwd{l}")
        if is_pool:
            dh1, g_pw[l], g_psc[l] = pool_bwd(st["h1"], dm, pw[l], psc[l].reshape(1, D), name=f"pool_bwd{l}")
        else:
            j = l - n_pool
            wq_j, wq_l = (wq0, None) if j == 0 else (late_w["wq"], j - 1)
            do = matmul(dm, late_w["wo"], b_layer=j, tb=True, out_dtype=BF16, name=f"o_proj_dx{j}")
            g_wo = matmul(st["o"], dm, ta=True, out_dtype=BF16, name=f"o_proj_dw{j}")
            pending.append((g_wo, 0, "w_o", j))
            dq, dk, dv, got = attn_bwd(st["q"], kv, do, name=f"attn_bwd{j}", side=send_pending())
            recv.update(got)
            pending.clear()
            dkvs.append((dk, dv))
            g_wq = matmul(st["h1"], dq, ta=True, out_dtype=BF16, name=f"q_proj_dw{j}")
            pending.append((g_wq, 0, "w_q", j))
            dh1 = matmul(dq, wq_j, b_layer=wq_l, tb=True, name=f"q_proj_dx{j}")
        if l == n_pool:
            dkv = sum_concat_cast(dkvs, BF16, name="dkv_pack")
            g_wkv = matmul(hk, dkv, ta=True, out_dtype=BF16, name="kv_proj_dw")
            pending.append((g_wkv, 1, "w_kv", None))
            dhk = matmul(dkv, wkv, tb=True, name="kv_proj_dx")
            dhs, gs = [dh1, dhk], [vec(mix_pre_g, l), kv_norm_g.reshape(1, D)]
        else:
            dhs, gs = [dh1], [vec(mix_pre_g, l)]
        below = (saved[l - 1]["d"], vec(mlp_post_g, l - 1)) if l > 0 else None
        outs = pre_norm_bwd(dxm, st["x"], dhs, gs, below, name=f"mix_in_bwd{l}")
        dy, outs = outs[0], outs[1:]
        if below is not None:
            dd, outs = outs[0], outs[1:]
            gains["mlp_post"][l - 1] = outs[-1]
        gains["mix_pre"][l] = outs[0]
        if l == n_pool:
            g_kvn = outs[1]
    grad_x = dy.reshape(x.shape)

    g_pool_w = jnp.stack(g_pw)
    g_pool_scale = jnp.concatenate(g_psc, axis=0)
    zero_rows = jnp.zeros((8 - depth, D), F32)
    gain_rows = []
    for k in ("mix_pre", "mix_post", "mlp_pre", "mlp_post"):
        gain_rows += gains[k] + [zero_rows]
    gain_rows += [g_kvn, jnp.zeros((7, D), F32)]
    gain_pack = jnp.concatenate(gain_rows, axis=0)
    gain_parts = all_gather([(gain_pack, 0)], name="gather_gain_grads")[0]
    pending += [(g_pool_w, 2, "pool_w", None), (g_pool_scale, 1, "pool_scale", None)]
    dy = start_pending("pool", dy)
    sent = dict.fromkeys(n for side, _, _, _ in started for n in side.names)
    recv.update(wait_sides(started, {n: recv[n] for n in sent}, dy, name="exchange_wait"))
    big = {"w_q": (w_q, m_w_q, v_w_q), "w_kv": (w_kv, m_w_kv, v_w_kv), "w_o": (w_o, m_w_o, v_w_o),
           "w_up": (w_up, m_w_up, v_w_up), "w_down": (w_down, m_w_down, v_w_down),
           "pool_w": (pool_w, m_pool_w, v_pool_w), "pool_scale": (pool_scale, m_pool_scale, v_pool_scale)}
    res = {n: reduce_adamw(recv[n], *wmv, name=f"adamw_{n}") for n, wmv in big.items()}
    gain_groups = [(mix_pre_g, m_mix_pre_g, v_mix_pre_g), (mix_post_g, m_mix_post_g, v_mix_post_g),
                   (mlp_pre_g, m_mlp_pre_g, v_mlp_pre_g), (mlp_post_g, m_mlp_post_g, v_mlp_post_g),
                   (kv_norm_g.reshape(1, D), m_kv_norm_g.reshape(1, D), v_kv_norm_g.reshape(1, D))]
    gres = gains_adamw(gain_parts, gain_groups, name="adamw_gains")
    for n, r in zip(["mix_pre_g", "mix_post_g", "mlp_pre_g", "mlp_post_g"], gres[:4]):
        res[n] = r
    res["kv_norm_g"] = [a.reshape(D) for a in gres[4]]

    order = ["pool_w", "pool_scale", "w_q", "w_kv", "kv_norm_g", "w_o", "w_up", "w_down",
             "mix_pre_g", "mix_post_g", "mlp_pre_g", "mlp_post_g"]
    out = [loss, grad_x]
    for k in range(4):
        out += [res[n][k] for n in order]
    return tuple(out)
```

```python
import functools

import jax
import jax.numpy as jnp
from jax import lax
from jax.experimental import pallas as pl
from jax.experimental.pallas import tpu as pltpu

F32 = jnp.float32
BF16 = jnp.bfloat16

EPS = 1e-6
HEAD_DIM = 64
LANES = 128
POOL_WINDOWS = (2, 4, 8, 16)
HALO = 16
N_DEV = 8
MESH_AXES = ("x", "y", "c")

ADAM_LR = 0.001
ADAM_B1 = 0.9
ADAM_B2 = 0.999
ADAM_EPS = 1e-08
ADAM_WD = 0.01
ADAM_STEP = 10

ROW_TILE = 512
ATT_TILE = 256
ROW_CHUNK = 32
GAIN_ROWS = 40


def _tile(n, want):
    return want if n % want == 0 else n


def matmul(a, b, *, name, ta=False, tb=False, a_layer=None, b_layer=None, out_dtype=F32,
           a_fn=None, epi=None, epi_in=None, out_scale=None, tm=1024, tn=1024, tk=1024):
    a2 = a.shape[1:] if a_layer is not None else a.shape
    b2 = b.shape[1:] if b_layer is not None else b.shape
    (K, M) = a2 if ta else a2[::-1]
    if not ta:
        M, K = a2
    if tb:
        N, Kb = b2
    else:
        Kb, N = b2
    assert K == Kb, (a.shape, b.shape)
    tm, tn, tk = _tile(M, tm), _tile(N, tn), _tile(K, tk)
    nk = K // tk
    grid = (M // tm, N // tn, nk)

    def lead(layer, shape, imap):
        if layer is None:
            return pl.BlockSpec(shape, imap)
        return pl.BlockSpec((None,) + shape, lambda i, j, k: (layer,) + imap(i, j, k))

    a_spec = lead(a_layer, (tk, tm) if ta else (tm, tk), (lambda i, j, k: (k, i)) if ta else (lambda i, j, k: (i, k)))
    b_spec = lead(b_layer, (tn, tk) if tb else (tk, tn), (lambda i, j, k: (j, k)) if tb else (lambda i, j, k: (k, j)))
    in_specs = [a_spec, b_spec]
    operands = [a, b]
    if epi is not None:
        in_specs.append(pl.BlockSpec((tm, tn), lambda i, j, k: (i, j)))
        operands.append(epi_in)
    out_shape = jax.ShapeDtypeStruct((M, N), out_dtype)
    out_spec = pl.BlockSpec((tm, tn), lambda i, j, k: (i, j))
    dims = (((0 if ta else 1,), (1 if tb else 0,)), ((), ()))
    n_in = len(operands)

    def body(*refs):
        a_ref, b_ref = refs[0], refs[1]
        e_ref = refs[2] if epi is not None else None
        o_ref = refs[n_in]

        def product():
            av = a_ref[...]
            if a_fn is not None:
                av = a_fn(av)
            return lax.dot_general(av.astype(BF16), b_ref[...].astype(BF16), dims, preferred_element_type=F32)

        def finish(r):
            if epi is not None:
                r = epi(r, e_ref[...])
            if out_scale is not None:
                r = r * out_scale
            o_ref[...] = r.astype(out_dtype)

        if nk == 1:
            finish(product())
            return
        acc_ref = refs[n_in + 1]
        k = pl.program_id(2)

        @pl.when(k == 0)
        def _():
            acc_ref[...] = product()

        @pl.when(k > 0)
        def _():
            acc_ref[...] += product()

        @pl.when(k == nk - 1)
        def _():
            finish(acc_ref[...])

    return pl.pallas_call(
        body, name=name, grid=grid, in_specs=in_specs, out_specs=out_spec, out_shape=out_shape,
        scratch_shapes=[pltpu.VMEM((tm, tn), F32)] if nk > 1 else [],
        compiler_params=pltpu.CompilerParams(dimension_semantics=("parallel", "parallel", "arbitrary")),
    )(*operands)


def _relu2(u):
    r = jnp.maximum(u.astype(F32), 0.0)
    return r * r


def _relu2_grad(acc, u):
    return acc * (2.0 * jnp.maximum(u.astype(F32), 0.0))


def rowwise(fn, rows, vecs, out_rows, n_acc, *, name, tile=ROW_TILE):
    S = rows[0].shape[0]
    tile = _tile(S, tile)
    n_rows, n_vecs, n_out = len(rows), len(vecs), len(out_rows)
    acc_cols = [None] * n_acc

    def body(*refs):
        ins = [r[...] for r in refs[:n_rows + n_vecs]]
        outs = refs[n_rows + n_vecs:]
        ro, ac = fn(*ins)
        assert len(ro) == n_out and len(ac) == n_acc
        for r, o in zip(outs[:n_out], ro):
            r[...] = o.astype(r.dtype)
        i = pl.program_id(0)
        for r, a in zip(outs[n_out:], ac):
            @pl.when(i == 0)
            def _():
                r[...] = jnp.zeros_like(r)
            r[...] += a

    acc_shapes = jax.eval_shape(
        lambda *xs: fn(*xs)[1],
        *[jax.ShapeDtypeStruct((tile, r.shape[1]), r.dtype) for r in rows],
        *[jax.ShapeDtypeStruct(v.shape, v.dtype) for v in vecs])
    in_specs = [pl.BlockSpec((tile, r.shape[1]), lambda i: (i, 0)) for r in rows]
    in_specs += [pl.BlockSpec(v.shape, lambda i: (0, 0)) for v in vecs]
    out_specs = [pl.BlockSpec((tile, c), lambda i: (i, 0)) for c, _ in out_rows]
    out_specs += [pl.BlockSpec(a.shape, lambda i: (0, 0)) for a in acc_shapes]
    out_shape = [jax.ShapeDtypeStruct((S, c), dt) for c, dt in out_rows]
    out_shape += [jax.ShapeDtypeStruct(a.shape, F32) for a in acc_shapes]
    del acc_cols
    return pl.pallas_call(
        body, name=name, grid=(S // tile,), in_specs=in_specs, out_specs=out_specs, out_shape=out_shape,
        compiler_params=pltpu.CompilerParams(dimension_semantics=("arbitrary",)),
    )(*rows, *vecs)


def _rms(x, g):
    r = lax.rsqrt(jnp.mean(x * x, axis=-1, keepdims=True) + EPS)
    return x * r * g


def _rms_bwd(x, g, dy):
    r = lax.rsqrt(jnp.mean(x * x, axis=-1, keepdims=True) + EPS)
    xh = x * r
    dyg = dy * g
    dx = r * (dyg - xh * jnp.mean(dyg * xh, axis=-1, keepdims=True))
    dg = jnp.sum(dy * xh, axis=0, keepdims=True)
    return dx, dg


def norm_only(x, g, dtype, *, name):
    D = x.shape[1]
    return rowwise(lambda xv, gv: ([_rms(xv, gv)], []), [x], [g], [(D, dtype)], 0, name=name)[0]


def residual_norms(x, m, g_post, next_gs, next_dtypes, *, name):
    D = x.shape[1]

    def fn(xv, mv, gp, *gs):
        xn = xv + _rms(mv, gp)
        return [xn] + [_rms(xn, g) for g in gs], []

    return rowwise(fn, [x, m], [g_post] + list(next_gs), [(D, F32)] + [(D, dt) for dt in next_dtypes], 0, name=name)


def residual_loss(x, d, g_post, target, *, name):
    D = x.shape[1]

    def fn(xv, dv, tv, gp):
        e = xv + _rms(dv, gp) - tv
        dy = e * (1.0 / D)
        dd, dg = _rms_bwd(dv, gp, dy)
        return [dy, dd], [jnp.sum(e * e, axis=0, keepdims=True) * (0.5 / D), dg]

    return rowwise(fn, [x, d, target], [g_post], [(D, F32), (D, BF16)], 2, name=name)


def mid_bwd(dy, x_mid, dh2, m, g_mlp_pre, g_mix_post, dm_dtype, *, name):
    D = dy.shape[1]

    def fn(dyv, xm, dh, mv, gpre, gpost):
        dx, dg_pre = _rms_bwd(xm, gpre, dh)
        dxm = dyv + dx
        dm, dg_post = _rms_bwd(mv, gpost, dxm)
        return [dxm, dm], [dg_pre, dg_post]

    return rowwise(fn, [dy, x_mid, dh2, m], [g_mlp_pre, g_mix_post], [(D, F32), (D, dm_dtype)], 2, name=name)


def pre_norm_bwd(dxm, x, dhs, gs, below=None, *, name):
    D = x.shape[1]
    n = len(dhs)
    rows = [dxm, x] + list(dhs) + ([below[0]] if below else [])
    vecs = list(gs) + ([below[1]] if below else [])

    def fn(*xs):
        dxv, xv, dh = xs[0], xs[1], xs[2:2 + n]
        g = xs[len(rows):]
        out, accs = dxv, []
        for k in range(n):
            dx, dg = _rms_bwd(xv, g[k], dh[k])
            out = out + dx
            accs.append(dg)
        if below is None:
            return [out], accs
        dd, dg = _rms_bwd(xs[2 + n], g[n], out)
        return [out, dd], accs + [dg]

    return rowwise(fn, rows, vecs, [(D, F32)] + ([(D, BF16)] if below else []), n + (1 if below else 0), name=name)


def sum_concat_cast(pairs, dtype, *, name):
    C = pairs[0][0].shape[1]
    n = len(pairs)

    def fn(*xs):
        return [jnp.concatenate([sum(xs[:n]), sum(xs[n:])], axis=1)], []

    return rowwise(fn, [a for a, _ in pairs] + [b for _, b in pairs], [], [(2 * C, dtype)], 0, name=name)[0]


def _window_sum(e, window, total_rows, backward):
    s, k = e, 1
    while k < window:
        s = s + pltpu.roll(s, (total_rows - k) if backward else k, 0)
        k *= 2
    return s


def pool_fwd(h, w, scale, *, name):
    S, D = h.shape
    G = len(POOL_WINDOWS)
    GC = D // G
    tile = _tile(S, ROW_TILE)
    hb = tile // HALO

    def body(hc_ref, hp_ref, w_ref, sc_ref, o_ref):
        i = pl.program_id(0)
        prev = jnp.where(i > 0, hp_ref[...], 0.0)
        ext = jnp.concatenate([prev, hc_ref[...]], axis=0)
        t = i * tile + lax.broadcasted_iota(jnp.int32, (tile, 1), 0)
        outs = []
        for g, window in enumerate(POOL_WINDOWS):
            e = ext[:, g * GC:(g + 1) * GC]
            s = _window_sum(e, window, HALO + tile, False)[HALO:, :]
            cnt = jnp.minimum(t + 1, window).astype(F32)
            y = s / cnt - e[HALO:, :]
            outs.append(jnp.dot(y.astype(BF16), w_ref[g], preferred_element_type=F32))
        o_ref[...] = jnp.concatenate(outs, axis=1) * sc_ref[...]

    return pl.pallas_call(
        body, name=name, grid=(S // tile,),
        in_specs=[pl.BlockSpec((tile, D), lambda i: (i, 0)),
                  pl.BlockSpec((HALO, D), lambda i: (jnp.maximum(i * hb - 1, 0), 0)),
                  pl.BlockSpec((G, GC, GC), lambda i: (0, 0, 0)),
                  pl.BlockSpec((1, D), lambda i: (0, 0))],
        out_specs=pl.BlockSpec((tile, D), lambda i: (i, 0)),
        out_shape=jax.ShapeDtypeStruct((S, D), F32),
        compiler_params=pltpu.CompilerParams(dimension_semantics=("parallel",)),
    )(h, h, w, scale)


def pool_bwd(h, dm, w, scale, *, name):
    S, D = h.shape
    G = len(POOL_WINDOWS)
    GC = D // G
    tile = _tile(S, ROW_TILE)
    hb = tile // HALO
    n_tiles = S // tile
    last_halo = S // HALO - 1

    def body(hc_ref, hp_ref, dmc_ref, dmn_ref, w_ref, sc_ref, dh_ref, dw_ref, dsc_ref):
        i = pl.program_id(0)

        @pl.when(i == 0)
        def _():
            dw_ref[...] = jnp.zeros_like(dw_ref)
            dsc_ref[...] = jnp.zeros_like(dsc_ref)

        prev = jnp.where(i > 0, hp_ref[...], 0.0)
        ext = jnp.concatenate([prev, hc_ref[...]], axis=0)
        nxt = jnp.where(i < n_tiles - 1, dmn_ref[...], 0.0)
        dmc = dmc_ref[...]
        dm_ext = jnp.concatenate([dmc, nxt], axis=0)
        t = i * tile + lax.broadcasted_iota(jnp.int32, (tile, 1), 0)
        t_ext = i * tile + lax.broadcasted_iota(jnp.int32, (tile + HALO, 1), 0)
        dhs, dscs = [], []
        for g, window in enumerate(POOL_WINDOWS):
            cols = slice(g * GC, (g + 1) * GC)
            e = ext[:, cols]
            s = _window_sum(e, window, HALO + tile, False)[HALO:, :]
            y = (s / jnp.minimum(t + 1, window).astype(F32) - e[HALO:, :]).astype(BF16)
            wg = w_ref[g]
            ypre = jnp.dot(y, wg, preferred_element_type=F32)
            dscs.append(jnp.sum(dmc[:, cols] * ypre, axis=0, keepdims=True))
            dyp = (dm_ext[:, cols] * sc_ref[:, cols]).astype(BF16)
            dw_ref[g] += lax.dot_general(y, dyp[:tile, :], (((0,), (0,)), ((), ())), preferred_element_type=F32)
            dy = lax.dot_general(dyp, wg, (((1,), (1,)), ((), ())), preferred_element_type=F32)
            r = dy / jnp.minimum(t_ext + 1, window).astype(F32)
            sr = _window_sum(r, window, tile + HALO, True)
            dhs.append(sr[:tile, :] - dy[:tile, :])
        dh_ref[...] = jnp.concatenate(dhs, axis=1)
        dsc_ref[...] += jnp.concatenate(dscs, axis=1)

    return pl.pallas_call(
        body, name=name, grid=(n_tiles,),
        in_specs=[pl.BlockSpec((tile, D), lambda i: (i, 0)),
                  pl.BlockSpec((HALO, D), lambda i: (jnp.maximum(i * hb - 1, 0), 0)),
                  pl.BlockSpec((tile, D), lambda i: (i, 0)),
                  pl.BlockSpec((HALO, D), lambda i: (jnp.minimum((i + 1) * hb, last_halo), 0)),
                  pl.BlockSpec((G, GC, GC), lambda i: (0, 0, 0)),
                  pl.BlockSpec((1, D), lambda i: (0, 0))],
        out_specs=[pl.BlockSpec((tile, D), lambda i: (i, 0)),
                   pl.BlockSpec((G, GC, GC), lambda i: (0, 0, 0)),
                   pl.BlockSpec((1, D), lambda i: (0, 0))],
        out_shape=[jax.ShapeDtypeStruct((S, D), F32), jax.ShapeDtypeStruct((G, GC, GC), F32),
                   jax.ShapeDtypeStruct((1, D), F32)],
        compiler_params=pltpu.CompilerParams(dimension_semantics=("arbitrary",)),
    )(h, h, dm, dm, w, scale)


ATT_LANES = 256
N_PAIR = ATT_LANES // HEAD_DIM
KEY_TILES = 3
KEY_TILES_BWD = 3


def _scores(xs, kj):
    return [lax.dot_general(x, kj, (((1,), (1,)), ((), ())), preferred_element_type=F32) for x in xs]


def _softplus_parts(z, mask):
    sp = jnp.maximum(z, 0.0) + jnp.log(1.0 + jnp.exp(-jnp.abs(z)))
    logb = z - sp
    if mask is not None:
        sp = jnp.where(mask, sp, 0.0)
    return logb, sp.astype(BF16), jnp.sum(sp, axis=1, keepdims=True)


def _weights(logb, later, c, mask):
    a = jnp.exp(logb - (later + c))
    return a if mask is None else jnp.where(mask, a, 0.0)


def _tile_weights(zss, cs, u_later, mask):
    partss = [[_softplus_parts(z, mask) for z in zs] for zs in zss]
    laterss = [[jnp.dot(sp, u_later, preferred_element_type=F32) for _, sp, _ in parts] for parts in partss]
    out = []
    for parts, laters in zip(partss, laterss):
        out.append(([p[0] for p in parts], [_weights(p[0], later, c, mask) for p, later, c in zip(parts, laters, cs)]))
        cs = [c + p[2] for c, p in zip(cs, parts)]
    return out, tuple(cs)


def _tri(T, later):
    rows = lax.broadcasted_iota(jnp.int32, (T, T), 0)
    cols = lax.broadcasted_iota(jnp.int32, (T, T), 1)
    return jnp.where((rows > cols) if later else (rows < cols), 1.0, 0.0).astype(BF16)


def _head_masks(x2, axis=None):
    lane = lax.broadcasted_iota(jnp.int32, (1, ATT_LANES), 1)
    parts = [jnp.where((lane // HEAD_DIM) == hh, x2, jnp.zeros_like(x2)) for hh in range(N_PAIR)]
    return parts if axis is None else jnp.concatenate(parts, axis=axis)


def _cat_bf16(parts, axis):
    return jnp.concatenate([p.astype(BF16) for p in parts], axis=axis)


def _side_split(side, refs, n_in, n_out):
    if side is None:
        return refs, None
    n_src, n_buf = len(side.srcs), len(side.bufs)
    ins, rest = refs[:n_in], refs[n_in:]
    src_refs, rest = rest[:n_src], rest[n_src + n_buf:]
    outs, rest = rest[:n_out], rest[n_out:]
    buf_refs, rest = rest[:n_buf], rest[n_buf:]
    own_scratch, sems = rest[:len(rest) - 3], rest[len(rest) - 3:]
    return tuple(ins) + tuple(outs) + tuple(own_scratch), (src_refs, buf_refs, sems)


def _side_phase(side, side_refs, phase, when):
    if side is None:
        return

    @pl.when(when)
    def _():
        side.copies(phase, *side_refs)


def attn_fwd(q, kv, *, name, side=None):
    S, D = q.shape
    P = D // ATT_LANES
    T = _tile(S, ATT_TILE)
    nq = S // T

    def body(*refs):
        (q_ref, k_ref, v_ref, o_ref), side_refs = _side_split(side, refs, 3, 1)
        i = pl.program_id(1)
        p = pl.program_id(0)
        _side_phase(side, side_refs, 0, (p == 0) & (i == 0))
        u_later = _tri(T, True)
        qhs = _head_masks(q_ref[...])
        diag = lax.broadcasted_iota(jnp.int32, (T, T), 1) < lax.broadcasted_iota(jnp.int32, (T, T), 0)

        def rows_of(ref, j):
            return ref[pl.ds(pl.multiple_of(j * T, T), T), :]

        def step(js, carry, mask):
            cs, acc = carry
            zss = [_scores(qhs, rows_of(k_ref, j)) for j in js]
            vcat = jnp.concatenate([_head_masks(rows_of(v_ref, j), axis=0) for j in js], axis=0)
            per_tile, cs = _tile_weights(zss, cs, u_later, mask)
            acat = _cat_bf16([a for _, aa in per_tile for a in aa], 1)
            return cs, acc + jnp.dot(acat, vcat, preferred_element_type=F32)

        carry = step([i], ((jnp.zeros((T, 1), F32),) * N_PAIR, jnp.zeros((T, ATT_LANES), F32)), diag)
        carry = lax.fori_loop(0, i % KEY_TILES, lambda n, cr: step([i - 1 - n], cr, None), carry)
        first = i - 1 - i % KEY_TILES
        _, acc = lax.fori_loop(0, i // KEY_TILES,
                               lambda n, cr: step([first - KEY_TILES * n - t for t in range(KEY_TILES)], cr, None), carry)
        o_ref[...] = acc.astype(o_ref.dtype)
        _side_phase(side, side_refs, 1, (p == P - 1) & (i == nq - 1))

    sd = side
    outs = pl.pallas_call(
        body, name=name, grid=(P, nq),
        in_specs=[pl.BlockSpec((T, ATT_LANES), lambda p, i: (i, p)),
                  pl.BlockSpec((S, ATT_LANES), lambda p, i: (0, p)),
                  pl.BlockSpec((S, ATT_LANES), lambda p, i: (0, P + p))] + (sd.specs() if sd else []),
        out_specs=[pl.BlockSpec((T, ATT_LANES), lambda p, i: (i, p))] + (sd.out_specs() if sd else []),
        out_shape=[jax.ShapeDtypeStruct((S, D), BF16)] + (sd.out_shape() if sd else []),
        scratch_shapes=sd.scratch() if sd else [],
        input_output_aliases=sd.aliases(3, 1) if sd else {},
        compiler_params=pltpu.CompilerParams(dimension_semantics=("arbitrary", "arbitrary"),
                                             has_side_effects=sd is not None),
    )(q, kv, kv, *(sd.operands() if sd else []))
    return outs[0] if sd is None else (outs[0], sd.result(outs[1:]))


def attn_bwd(q, kv, do, *, name, side=None):
    S, D = q.shape
    P = D // ATT_LANES
    T = _tile(S, ATT_TILE)
    nb = S // T

    def body(*refs):
        (q_ref, k_ref, v_ref, do_ref, dq_ref, dk_ref, dv_ref, g_scr, s_scr), side_refs = _side_split(side, refs, 4, 3)
        i = pl.program_id(1)
        p = pl.program_id(0)
        _side_phase(side, side_refs, 0, (p == 0) & (i == 0))

        @pl.when(i == 0)
        def _():
            dk_ref[...] = jnp.zeros_like(dk_ref)
            dv_ref[...] = jnp.zeros_like(dv_ref)

        u_later = _tri(T, True)
        u_earlier = _tri(T, False)
        qhs = _head_masks(q_ref[...])
        dohs = _head_masks(do_ref[...])
        qcat = jnp.concatenate(qhs, axis=0)
        docat = jnp.concatenate(dohs, axis=0)
        diag = lax.broadcasted_iota(jnp.int32, (T, T), 1) < lax.broadcasted_iota(jnp.int32, (T, T), 0)
        tdot = (((0,), (0,)), ((), ()))

        def rows_of(ref, j):
            return ref.at[pl.ds(pl.multiple_of(j * T, T), T), :]

        def step1(js, cs, mask):
            dass = [_scores(dohs, rows_of(v_ref, j)[...]) for j in js]
            zss = [_scores(qhs, rows_of(k_ref, j)[...]) for j in js]
            per_tile, cs = _tile_weights(zss, cs, u_later, mask)
            for j, das, (logbs, aa) in zip(js, dass, per_tile):
                for hh in range(N_PAIR):
                    g_scr[hh, j] = (das[hh] * aa[hh]).astype(BF16)
                    sg = jnp.exp(logbs[hh])
                    if mask is not None:
                        sg = jnp.where(mask, sg, 0.0)
                    s_scr[hh, j] = sg.astype(BF16)
            for j, (_, aa) in zip(js, per_tile):
                rows_of(dv_ref, j)[...] += lax.dot_general(_cat_bf16(aa, 0), docat, tdot,
                                                          preferred_element_type=F32)
            return cs

        G = KEY_TILES_BWD
        cs = step1([i], (jnp.zeros((T, 1), F32),) * N_PAIR, diag)
        cs = lax.fori_loop(0, i % G, lambda n, c: step1([i - 1 - n], c, None), cs)
        first = i - 1 - i % G
        lax.fori_loop(0, i // G, lambda n, c: step1([first - G * n - t for t in range(G)], c, None), cs)

        def step2(js, carry):
            cs, acc = carry
            cumss = [[jnp.dot(g_scr[hh, j], u_earlier, preferred_element_type=F32) for hh in range(N_PAIR)]
                     for j in js]
            kcat = jnp.concatenate([_head_masks(rows_of(k_ref, j)[...], axis=0) for j in js], axis=0)
            dzss = []
            for j, cums in zip(js, cumss):
                dzs, new = [], []
                for hh in range(N_PAIR):
                    gf = g_scr[hh, j].astype(F32)
                    sg = s_scr[hh, j].astype(F32)
                    dzs.append((gf - sg * (gf + (cums[hh] + cs[hh]))).astype(BF16))
                    new.append(cs[hh] + jnp.sum(gf, axis=1, keepdims=True))
                cs = tuple(new)
                dzss.append(dzs)
            acc = acc + jnp.dot(jnp.concatenate([dz for dzs in dzss for dz in dzs], axis=1), kcat,
                                preferred_element_type=F32)
            for j, dzs in zip(js, dzss):
                rows_of(dk_ref, j)[...] += lax.dot_general(jnp.concatenate(dzs, axis=0), qcat, tdot,
                                                          preferred_element_type=F32)
            return cs, acc

        carry = ((jnp.zeros((T, 1), F32),) * N_PAIR, jnp.zeros((T, ATT_LANES), F32))
        carry = lax.fori_loop(0, (i + 1) // G, lambda n, cr: step2([G * n + t for t in range(G)], cr), carry)
        done = (i + 1) // G * G
        _, dq = lax.fori_loop(0, (i + 1) % G, lambda n, cr: step2([done + n], cr), carry)
        dq_ref[...] = (dq * (HEAD_DIM ** -0.5)).astype(dq_ref.dtype)
        _side_phase(side, side_refs, 1, (p == P - 1) & (i == nb - 1))

    sd = side
    outs = pl.pallas_call(
        body, name=name, grid=(P, nb),
        in_specs=[pl.BlockSpec((T, ATT_LANES), lambda p, i: (i, p)),
                  pl.BlockSpec((S, ATT_LANES), lambda p, i: (0, p)),
                  pl.BlockSpec((S, ATT_LANES), lambda p, i: (0, P + p)),
                  pl.BlockSpec((T, ATT_LANES), lambda p, i: (i, p))] + (sd.specs() if sd else []),
        out_specs=[pl.BlockSpec((T, ATT_LANES), lambda p, i: (i, p)),
                   pl.BlockSpec((S, ATT_LANES), lambda p, i: (0, p)),
                   pl.BlockSpec((S, ATT_LANES), lambda p, i: (0, p))] + (sd.out_specs() if sd else []),
        out_shape=[jax.ShapeDtypeStruct((S, D), BF16), jax.ShapeDtypeStruct((S, D), F32),
                   jax.ShapeDtypeStruct((S, D), F32)] + (sd.out_shape() if sd else []),
        scratch_shapes=[pltpu.VMEM((N_PAIR, nb, T, T), BF16), pltpu.VMEM((N_PAIR, nb, T, T), BF16)]
        + (sd.scratch() if sd else []),
        input_output_aliases=sd.aliases(4, 3) if sd else {},
        compiler_params=pltpu.CompilerParams(dimension_semantics=("arbitrary", "arbitrary"),
                                             has_side_effects=sd is not None),
    )(q, kv, kv, do, *(sd.operands() if sd else []))
    return tuple(outs[:3]) if sd is None else (*outs[:3], sd.result(outs[3:]))


def _window(ref, axis, dev, n):
    return ref.at[(slice(None),) * axis + (pl.ds(dev * n, n),)]


def all_gather(ops, *, name):
    n_ops = len(ops)
    out_shape = []
    for a, ax in ops:
        shp = list(a.shape)
        shp[ax] *= N_DEV
        out_shape.append(jax.ShapeDtypeStruct(tuple(shp), a.dtype))

    def body(*refs):
        ins, outs = refs[:n_ops], refs[n_ops:2 * n_ops]
        send_sems, recv_sems, local_sems = refs[2 * n_ops:]
        x, y, c = (lax.axis_index(n) for n in MESH_AXES)
        me, sibling = (x, y, c), (x, y, 1 - c)
        chips = [(1 - x, y), (x, 1 - y), (1 - x, 1 - y)]

        def rows(o, dev):
            px, py, pc = dev
            ax = ops[o][1]
            return _window(outs[o], ax, 4 * px + 2 * py + pc, ops[o][0].shape[ax])

        def copy(o, k, block, to, src=None):
            return pltpu.make_async_remote_copy(
                src_ref=rows(o, block) if src is None else src, dst_ref=rows(o, block),
                send_sem=send_sems.at[o, k], recv_sem=recv_sems.at[o, k],
                device_id=to, device_id_type=pl.DeviceIdType.MESH)

        mine, first, passed = [], [], []
        for o in range(n_ops):
            cp = pltpu.make_async_copy(ins[o], rows(o, me), local_sems.at[o])
            cp.start()
            mine.append(cp)
            first.append(copy(o, 0, me, sibling, src=ins[o]))
            first += [copy(o, 1 + j, me, (*chip, c), src=ins[o]) for j, chip in enumerate(chips)]
        for cp in first:
            cp.start()
        for j, chip in enumerate(chips):
            for o in range(n_ops):
                copy(o, 1 + j, (*chip, c), me).wait_recv()
                cp = copy(o, 4 + j, (*chip, c), sibling)
                cp.start()
                passed.append(cp)
        for o in range(n_ops):
            copy(o, 0, sibling, me).wait_recv()
            for j, chip in enumerate(chips):
                copy(o, 4 + j, (*chip, 1 - c), me).wait_recv()
        for cp in first + passed:
            cp.wait_send()
        for cp in mine:
            cp.wait()

    any_spec = pl.BlockSpec(memory_space=pl.ANY)
    return pl.pallas_call(
        body, name=name, in_specs=[any_spec] * n_ops, out_specs=[any_spec] * n_ops, out_shape=out_shape,
        scratch_shapes=[pltpu.SemaphoreType.DMA((n_ops, 7)), pltpu.SemaphoreType.DMA((n_ops, 7)),
                        pltpu.SemaphoreType.DMA((n_ops,))],
        compiler_params=pltpu.CompilerParams(has_side_effects=True),
    )(*[a for a, _ in ops])


class Side:
    def __init__(self, kind, items, bufs):
        self.kind, self.items = kind, items
        self.names = list(bufs)
        self.bufs = [bufs[n] for n in self.names]
        self.srcs = [it[0] for it in items]

    def operands(self):
        return self.srcs + self.bufs

    def specs(self):
        return [pl.BlockSpec(memory_space=pl.ANY)] * (len(self.srcs) + len(self.bufs))

    def out_specs(self):
        return [pl.BlockSpec(memory_space=pl.ANY)] * len(self.bufs)

    def out_shape(self):
        return [jax.ShapeDtypeStruct(b.shape, b.dtype) for b in self.bufs]

    def aliases(self, first_in, first_out):
        return {first_in + len(self.srcs) + k: first_out + k for k in range(len(self.bufs))}

    def scratch(self):
        n = len(self.items)
        return [pltpu.SemaphoreType.DMA((n, N_DEV - 1)), pltpu.SemaphoreType.DMA((n, N_DEV - 1)),
                pltpu.SemaphoreType.DMA((n,))]

    def result(self, outs):
        return dict(zip(self.names, outs))

    def copies(self, phase, src_refs, buf_refs, sems):
        send_sems, recv_sems, local_sems = sems
        pos = tuple(lax.axis_index(n) for n in MESH_AXES)
        me = 4 * pos[0] + 2 * pos[1] + pos[2]
        for o, (_, ax, name, layer) in enumerate(self.items):
            src, buf = src_refs[o], buf_refs[self.names.index(name)]
            if self.kind == "gather":
                whole = buf if layer is None else buf.at[layer]
                n = src.shape[ax]
                sent = lambda dev, src=src: src
                lands = lambda dev, whole=whole, ax=ax, n=n: _window(whole, ax, dev, n)
            else:
                n = src.shape[ax] // N_DEV
                sent = lambda dev, src=src, ax=ax, n=n: _window(src, ax, dev, n)
                lands = lambda dev, buf=buf, layer=layer: buf.at[dev] if layer is None else buf.at[dev, layer]
            if local_sems is not None:
                local = pltpu.make_async_copy(sent(me), lands(me), local_sems.at[o])
                if phase == 0:
                    local.start()
                else:
                    local.wait()
            for r in range(1, N_DEV):
                peer = tuple(1 - p if r & bit else p for p, bit in zip(pos, (4, 2, 1)))
                pid = 4 * peer[0] + 2 * peer[1] + peer[2]
                cp = pltpu.make_async_remote_copy(
                    src_ref=sent(pid) if phase == 0 else sent(me), dst_ref=lands(me) if phase == 0 else lands(pid),
                    send_sem=send_sems[o] if isinstance(send_sems, (list, tuple)) else send_sems.at[o, r - 1],
                    recv_sem=recv_sems[o] if isinstance(recv_sems, (list, tuple)) else recv_sems.at[o, r - 1],
                    device_id=peer, device_id_type=pl.DeviceIdType.MESH)
                if phase == 0:
                    cp.start()
                else:
                    cp.wait_recv()
                    cp.wait_send()


def _hbm(a):
    return pltpu.with_memory_space_constraint(a, pltpu.HBM)


def own_blocks(items, bufs):
    me = 4 * lax.axis_index("x") + 2 * lax.axis_index("y") + lax.axis_index("c")
    bufs = {name: bufs[name] for _, _, name, _ in items}
    for src, ax, name, layer in items:
        n = src.shape[ax] // N_DEV
        blk = lax.dynamic_slice_in_dim(src, me * n, n, axis=ax)
        lead = (me,) if layer is None else (me, layer)
        blk = blk.reshape((1,) * len(lead) + blk.shape)
        bufs[name] = lax.dynamic_update_slice(bufs[name], blk, lead + (0,) * src.ndim)
    return bufs


def start_side(side, *, name):
    n_src, n_buf = len(side.srcs), len(side.bufs)
    n = len(side.items)

    def body(*refs):
        src_refs, buf_refs = refs[:n_src], refs[n_src:n_src + n_buf]
        sems = refs[n_src + n_buf:n_src + n_buf + 2 * n]
        token = refs[-1]
        side.copies(0, src_refs, buf_refs, (list(sems[:n]), list(sems[n:]), None))
        token[...] = jnp.zeros_like(token)

    hbm = pl.BlockSpec(memory_space=pltpu.HBM)
    sem = pl.BlockSpec(memory_space=pltpu.SEMAPHORE)
    operands = side.operands()
    outs = pl.pallas_call(
        body, name=name,
        out_shape=(*[pltpu.SemaphoreType.DMA(())] * (2 * n),
                   *[pltpu.HBM(a.shape, a.dtype) for a in operands], jax.ShapeDtypeStruct((8, LANES), F32)),
        in_specs=[hbm] * len(operands),
        out_specs=(*[sem] * (2 * n), *[hbm] * len(operands), pl.BlockSpec(memory_space=pltpu.VMEM)),
        input_output_aliases={k: 2 * n + k for k in range(len(operands))},
        compiler_params=pltpu.CompilerParams(has_side_effects=pltpu.SideEffectType.DATAFLOW_SIDE_EFFECTING),
    )(*[_hbm(a) for a in operands])
    started = (side, list(outs[:n]), list(outs[n:2 * n]), list(outs[2 * n:2 * n + n_src]))
    return started, dict(zip(side.names, outs[2 * n + n_src:2 * n + n_src + n_buf])), outs[-1]


def wait_sides(started, bufs, after, *, name):
    names = list(bufs)
    flat = []
    for side, send_sems, recv_sems, srcs in started:
        flat += [*srcs, *send_sems, *recv_sems]
    n_buf = len(names)

    def body(*refs):
        buf_refs = refs[:n_buf]
        rest = refs[n_buf:]
        for side, _, _, srcs in started:
            n = len(srcs)
            src_refs, send_sems, recv_sems = rest[:n], list(rest[n:2 * n]), list(rest[2 * n:3 * n])
            rest = rest[3 * n:]
            side.copies(1, src_refs, [buf_refs[names.index(nm)] for nm in side.names], (send_sems, recv_sems, None))

    hbm = pl.BlockSpec(memory_space=pltpu.HBM)
    sem = pl.BlockSpec(memory_space=pltpu.SEMAPHORE)
    specs = []
    for _, _, _, srcs in started:
        specs += [hbm] * len(srcs) + [sem] * (2 * len(srcs))
    outs = pl.pallas_call(
        body, name=name,
        out_shape=tuple(pltpu.HBM(bufs[nm].shape, bufs[nm].dtype) for nm in names),
        in_specs=[hbm] * n_buf + specs + [pl.BlockSpec(memory_space=pl.ANY)],
        out_specs=tuple([hbm] * n_buf),
        input_output_aliases={k: k for k in range(n_buf)},
        compiler_params=pltpu.CompilerParams(has_side_effects=pltpu.SideEffectType.DATAFLOW_SIDE_EFFECTING),
    )(*[bufs[nm] for nm in names], *flat, after)
    return dict(zip(names, outs))


def _adamw(w, g, m, v):
    m = ADAM_B1 * m + (1.0 - ADAM_B1) * g
    v = ADAM_B2 * v + (1.0 - ADAM_B2) * (g * g)
    m_hat = m / (1.0 - ADAM_B1 ** ADAM_STEP)
    v_hat = v / (1.0 - ADAM_B2 ** ADAM_STEP)
    delta = -ADAM_LR * (m_hat / (jnp.sqrt(v_hat) + ADAM_EPS) + ADAM_WD * w)
    return delta, m, v


def reduce_adamw(parts, w, m, v, *, name):
    shape = w.shape
    C = shape[-1]
    R = w.size // C
    tile = _tile(R, 256)

    def body(p_ref, w_ref, m_ref, v_ref, g_ref, d_ref, nm_ref, nv_ref):
        g = p_ref[0].astype(F32)
        for s in range(1, N_DEV):
            g = g + p_ref[s].astype(F32)
        d, nm, nv = _adamw(w_ref[...], g, m_ref[...], v_ref[...])
        g_ref[...] = g
        d_ref[...] = d
        nm_ref[...] = nm
        nv_ref[...] = nv

    row = pl.BlockSpec((tile, C), lambda i: (i, 0))
    outs = pl.pallas_call(
        body, name=name, grid=(R // tile,),
        in_specs=[pl.BlockSpec((N_DEV, tile, C), lambda i: (0, i, 0)), row, row, row],
        out_specs=[row] * 4, out_shape=[jax.ShapeDtypeStruct((R, C), F32)] * 4,
        compiler_params=pltpu.CompilerParams(dimension_semantics=("parallel",)),
    )(parts.reshape(N_DEV, R, C), w.reshape(R, C), m.reshape(R, C), v.reshape(R, C))
    return [o.reshape(shape) for o in outs]


def gains_adamw(parts, groups, *, name):
    n = len(groups)

    def body(*refs):
        p_ref = refs[0]
        ins, outs = refs[1:1 + 3 * n], refs[1 + 3 * n:]
        for k in range(n):
            w_ref, m_ref, v_ref = ins[3 * k:3 * k + 3]
            L = w_ref.shape[0]
            g = p_ref[pl.ds(8 * k, L), :]
            for s in range(1, N_DEV):
                g = g + p_ref[pl.ds(s * GAIN_ROWS + 8 * k, L), :]
            d, nm, nv = _adamw(w_ref[...], g, m_ref[...], v_ref[...])
            for r, val in zip(outs[4 * k:4 * k + 4], (g, d, nm, nv)):
                r[...] = val

    flat = [a for grp in groups for a in grp]
    out_shape = [jax.ShapeDtypeStruct(grp[0].shape, F32) for grp in groups for _ in range(4)]
    outs = pl.pallas_call(body, name=name, out_shape=out_shape)(parts, *flat)
    return [outs[4 * k:4 * k + 4] for k in range(n)]


def kernel(x, pool_w, pool_scale, w_q, w_kv, kv_norm_g, w_o, w_up, w_down, mix_pre_g, mix_post_g, mlp_pre_g, mlp_post_g, loss_target, m_pool_w, m_pool_scale, m_w_q, m_w_kv, m_kv_norm_g, m_w_o, m_w_up, m_w_down, m_mix_pre_g, m_mix_post_g, m_mlp_pre_g, m_mlp_post_g, v_pool_w, v_pool_scale, v_w_q, v_w_kv, v_kv_norm_g, v_w_o, v_w_up, v_w_down, v_mix_pre_g, v_mix_post_g, v_mlp_pre_g, v_mlp_post_g):
    _, S, D = x.shape
    x0 = x.reshape(S, D)
    target = loss_target.reshape(S, D)
    depth = w_up.shape[0]
    n_pool = pool_w.shape[0]
    F = w_up.shape[2] * N_DEV
    G = pool_w.shape[1]
    GC = D // G

    def vec(a, l):
        return a[l].reshape(1, D)

    n_att = depth - n_pool
    wq_s, wkv_s, wo_s = w_q.astype(BF16), w_kv.astype(BF16), w_o.astype(BF16)
    wup_s, wdn_s, pw_s = w_up.astype(BF16), w_down.astype(BF16), pool_w.astype(BF16)
    wq0, wkv, wup_a, wdn_a, pw, psc = all_gather(
        [(wq_s[0], 0), (wkv_s, 1), (wup_s[:n_pool], 2), (wdn_s[:n_pool], 1), (pw_s, 2), (pool_scale, 1)],
        name="gather_weights")
    late = Side("gather",
                [(wq_s[1:], 1, "wq", None), (wo_s, 1, "wo", None), (wup_s[n_pool:], 2, "wup", None),
                 (wdn_s[n_pool:], 1, "wdn", None)],
                {"wq": lax.empty((n_att - 1, D, D), BF16), "wo": lax.empty((n_att, D, D), BF16),
                 "wup": lax.empty((n_att, D, F), BF16), "wdn": lax.empty((n_att, F, D), BF16)})
    late_w = None

    def layer_of(early, key, l, n_early):
        return (early, l) if l < n_early else (late_w[key], l - n_early)

    saved = []
    xs = x0
    h1 = norm_only(xs, vec(mix_pre_g, 0), F32, name="norm_in")
    kv = hk = None
    dy = loss_rows = None
    for l in range(depth):
        is_pool = l < n_pool
        st = {"x": xs, "h1": h1}
        if is_pool:
            m = pool_fwd(h1, pw[l], psc[l].reshape(1, D), name=f"pool_fwd{l}")
        else:
            j = l - n_pool
            wq_j, wq_l = (wq0, None) if j == 0 else (late_w["wq"], j - 1)
            q = matmul(h1, wq_j, b_layer=wq_l, out_dtype=BF16, out_scale=HEAD_DIM ** -0.5, name=f"q_proj{j}")
            if j == 0:
                o, late_w = attn_fwd(q, kv, name=f"attn_fwd{j}", side=late)
            else:
                o = attn_fwd(q, kv, name=f"attn_fwd{j}")
            m = matmul(o, late_w["wo"], b_layer=j, name=f"o_proj{j}")
            st.update(q=q, o=o)
        x_mid, h2 = residual_norms(xs, m, vec(mix_post_g, l), [vec(mlp_pre_g, l)], [BF16], name=f"mix_out{l}")
        wup, lu = layer_of(wup_a, "wup", l, n_pool)
        wdn, ld = layer_of(wdn_a, "wdn", l, n_pool)
        u = matmul(h2, wup, b_layer=lu, out_dtype=BF16, tm=2048, name=f"mlp_up{l}")
        d = matmul(u, wdn, b_layer=ld, a_fn=_relu2, tm=2048, name=f"mlp_down{l}")
        st.update(m=m, x_mid=x_mid, h2=h2, u=u, d=d)
        saved.append(st)
        if l == depth - 1:
            dy, dd, loss_rows, dg_last = residual_loss(x_mid, d, vec(mlp_post_g, l), target, name="loss")
        elif l == n_pool - 1:
            xs, h1, hk = residual_norms(x_mid, d, vec(mlp_post_g, l), [vec(mix_pre_g, l + 1), kv_norm_g.reshape(1, D)],
                                        [BF16, BF16], name=f"mlp_out{l}")
            kv = matmul(hk, wkv, out_dtype=BF16, name="kv_proj")
        else:
            nxt_dt = F32 if l + 1 < n_pool else BF16
            xs, h1 = residual_norms(x_mid, d, vec(mlp_post_g, l), [vec(mix_pre_g, l + 1)], [nxt_dt], name=f"mlp_out{l}")
    loss = lax.psum(jnp.sum(loss_rows), MESH_AXES)

    recv = {"w_q": (w_q, BF16), "w_kv": (w_kv, BF16), "w_o": (w_o, BF16), "w_up": (w_up, BF16),
            "w_down": (w_down, BF16), "pool_w": (pool_w, F32), "pool_scale": (pool_scale, F32)}
    recv = {n: lax.empty((N_DEV,) + w.shape, dt) for n, (w, dt) in recv.items()}
    pending = []

    def send_pending():
        return Side("scatter", list(pending), {n: recv[n] for n in dict.fromkeys(it[2] for it in pending)})

    started = []

    def start_pending(tag, then):
        recv.update(own_blocks(pending, recv))
        begun, got, token = start_side(send_pending(), name=f"exchange_start_{tag}")
        recv.update(got)
        started.append(begun)
        pending.clear()
        return lax.optimization_barrier((then, token))[0]

    g_pw, g_psc = [None] * n_pool, [None] * n_pool
    gains = {k: [None] * depth for k in ("mix_pre", "mix_post", "mlp_pre", "mlp_post")}
    gains["mlp_post"][depth - 1] = dg_last
    dkvs = []
    g_kvn = None
    for l in reversed(range(depth)):
        st = saved[l]
        is_pool = l < n_pool
        wup, lu = layer_of(wup_a, "wup", l, n_pool)
        wdn, ld = layer_of(wdn_a, "wdn", l, n_pool)
        du = matmul(dd, wdn, b_layer=ld, tb=True, out_dtype=BF16, epi=_relu2_grad, epi_in=st["u"], tm=2048,
                    name=f"mlp_du{l}")
        g_wdn = matmul(st["u"], dd, ta=True, a_fn=_relu2, out_dtype=BF16, name=f"mlp_dwdn{l}")
        g_wup = matmul(st["h2"], du, ta=True, out_dtype=BF16, name=f"mlp_dwup{l}")
        pending += [(g_wup, 1, "w_up", l), (g_wdn, 0, "w_down", l)]
        if is_pool:
            du = start_pending(f"l{l}", du)
        dh2 = matmul(du, wup, b_layer=lu, tb=True, tm=2048, name=f"mlp_dh{l}")
        dxm, dm, gains["mlp_pre"][l], gains["mix_post"][l] = mid_bwd(
            dy, st["x_mid"], dh2, st["m"], vec(mlp_pre_g, l), vec(mix_post_g, l), F32 if is_pool else BF16,
            name=f"mid_bwd{l}")
        if is_pool:
            dh1, g_pw[l], g_psc[l] = pool_bwd(st["h1"], dm, pw[l], psc[l].reshape(1, D), name=f"pool_bwd{l}")
        else:
            j = l - n_pool
            wq_j, wq_l = (wq0, None) if j == 0 else (late_w["wq"], j - 1)
            do = matmul(dm, late_w["wo"], b_layer=j, tb=True, out_dtype=BF16, name=f"o_proj_dx{j}")
            g_wo = matmul(st["o"], dm, ta=True, out_dtype=BF16, name=f"o_proj_dw{j}")
            pending.append((g_wo, 0, "w_o", j))
            dq, dk, dv, got = attn_bwd(st["q"], kv, do, name=f"attn_bwd{j}", side=send_pending())
            recv.update(got)
            pending.clear()
            dkvs.append((dk, dv))
            g_wq = matmul(st["h1"], dq, ta=True, out_dtype=BF16, name=f"q_proj_dw{j}")
            pending.append((g_wq, 0, "w_q", j))
            dh1 = matmul(dq, wq_j, b_layer=wq_l, tb=True, name=f"q_proj_dx{j}")
        if l == n_pool:
            dkv = sum_concat_cast(dkvs, BF16, name="dkv_pack")
            g_wkv = matmul(hk, dkv, ta=True, out_dtype=BF16, name="kv_proj_dw")
            pending.append((g_wkv, 1, "w_kv", None))
            dhk = matmul(dkv, wkv, tb=True, name="kv_proj_dx")
            dhs, gs = [dh1, dhk], [vec(mix_pre_g, l), kv_norm_g.reshape(1, D)]
        else:
            dhs, gs = [dh1], [vec(mix_pre_g, l)]
        below = (saved[l - 1]["d"], vec(mlp_post_g, l - 1)) if l > 0 else None
        outs = pre_norm_bwd(dxm, st["x"], dhs, gs, below, name=f"mix_in_bwd{l}")
        dy, outs = outs[0], outs[1:]
        if below is not None:
            dd, outs = outs[0], outs[1:]
            gains["mlp_post"][l - 1] = outs[-1]
        gains["mix_pre"][l] = outs[0]
        if l == n_pool:
            g_kvn = outs[1]
    grad_x = dy.reshape(x.shape)

    g_pool_w = jnp.stack(g_pw)
    g_pool_scale = jnp.concatenate(g_psc, axis=0)
    zero_rows = jnp.zeros((8 - depth, D), F32)
    gain_rows = []
    for k in ("mix_pre", "mix_post", "mlp_pre", "mlp_post"):
        gain_rows += gains[k] + [zero_rows]
    gain_rows += [g_kvn, jnp.zeros((7, D), F32)]
    gain_pack = jnp.concatenate(gain_rows, axis=0)
    gain_parts = all_gather([(gain_pack, 0)], name="gather_gain_grads")[0]
    pending += [(g_pool_w, 2, "pool_w", None), (g_pool_scale, 1, "pool_scale", None)]
    dy = start_pending("pool", dy)
    sent = dict.fromkeys(n for side, _, _, _ in started for n in side.names)
    recv.update(wait_sides(started, {n: recv[n] for n in sent}, dy, name="exchange_wait"))
    big = {"w_q": (w_q, m_w_q, v_w_q), "w_kv": (w_kv, m_w_kv, v_w_kv), "w_o": (w_o, m_w_o, v_w_o),
           "w_up": (w_up, m_w_up, v_w_up), "w_down": (w_down, m_w_down, v_w_down),
           "pool_w": (pool_w, m_pool_w, v_pool_w), "pool_scale": (pool_scale, m_pool_scale, v_pool_scale)}
    res = {n: reduce_adamw(recv[n], *wmv, name=f"adamw_{n}") for n, wmv in big.items()}
    gain_groups = [(mix_pre_g, m_mix_pre_g, v_mix_pre_g), (mix_post_g, m_mix_post_g, v_mix_post_g),
                   (mlp_pre_g, m_mlp_pre_g, v_mlp_pre_g), (mlp_post_g, m_mlp_post_g, v_mlp_post_g),
                   (kv_norm_g.reshape(1, D), m_kv_norm_g.reshape(1, D), v_kv_norm_g.reshape(1, D))]
    gres = gains_adamw(gain_parts, gain_groups, name="adamw_gains")
    for n, r in zip(["mix_pre_g", "mix_post_g", "mlp_pre_g", "mlp_post_g"], gres[:4]):
        res[n] = r
    res["kv_norm_g"] = [a.reshape(D) for a in gres[4]]

    order = ["pool_w", "pool_scale", "w_q", "w_kv", "kv_norm_g", "w_o", "w_up", "w_down",
             "mix_pre_g", "mix_post_g", "mlp_pre_g", "mlp_post_g"]
    out = [loss, grad_x]
    for k in range(4):
        out += [res[n][k] for n in order]
    return tuple(out)
```

```python
import functools

import jax
import jax.numpy as jnp
from jax import lax
from jax.experimental import pallas as pl
from jax.experimental.pallas import tpu as pltpu

F32 = jnp.float32
BF16 = jnp.bfloat16

EPS = 1e-6
HEAD_DIM = 64
LANES = 128
POOL_WINDOWS = (2, 4, 8, 16)
HALO = 16
N_DEV = 8
MESH_AXES = ("x", "y", "c")

ADAM_LR = 0.001
ADAM_B1 = 0.9
ADAM_B2 = 0.999
ADAM_EPS = 1e-08
ADAM_WD = 0.01
ADAM_STEP = 10

ROW_TILE = 512
ATT_TILE = 256
ROW_CHUNK = 32
GAIN_ROWS = 40


def _tile(n, want):
    return want if n % want == 0 else n


def matmul(a, b, *, name, ta=False, tb=False, a_layer=None, b_layer=None, out_dtype=F32,
           a_fn=None, epi=None, epi_in=None, out_scale=None, tm=1024, tn=1024, tk=1024):
    a2 = a.shape[1:] if a_layer is not None else a.shape
    b2 = b.shape[1:] if b_layer is not None else b.shape
    (K, M) = a2 if ta else a2[::-1]
    if not ta:
        M, K = a2
    if tb:
        N, Kb = b2
    else:
        Kb, N = b2
    assert K == Kb, (a.shape, b.shape)
    tm, tn, tk = _tile(M, tm), _tile(N, tn), _tile(K, tk)
    nk = K // tk
    grid = (M // tm, N // tn, nk)

    def lead(layer, shape, imap):
        if layer is None:
            return pl.BlockSpec(shape, imap)
        return pl.BlockSpec((None,) + shape, lambda i, j, k: (layer,) + imap(i, j, k))

    a_spec = lead(a_layer, (tk, tm) if ta else (tm, tk), (lambda i, j, k: (k, i)) if ta else (lambda i, j, k: (i, k)))
    b_spec = lead(b_layer, (tn, tk) if tb else (tk, tn), (lambda i, j, k: (j, k)) if tb else (lambda i, j, k: (k, j)))
    in_specs = [a_spec, b_spec]
    operands = [a, b]
    if epi is not None:
        in_specs.append(pl.BlockSpec((tm, tn), lambda i, j, k: (i, j)))
        operands.append(epi_in)
    out_shape = jax.ShapeDtypeStruct((M, N), out_dtype)
    out_spec = pl.BlockSpec((tm, tn), lambda i, j, k: (i, j))
    dims = (((0 if ta else 1,), (1 if tb else 0,)), ((), ()))
    n_in = len(operands)

    def body(*refs):
        a_ref, b_ref = refs[0], refs[1]
        e_ref = refs[2] if epi is not None else None
        o_ref = refs[n_in]

        def product():
            av = a_ref[...]
            if a_fn is not None:
                av = a_fn(av)
            return lax.dot_general(av.astype(BF16), b_ref[...].astype(BF16), dims, preferred_element_type=F32)

        def finish(r):
            if epi is not None:
                r = epi(r, e_ref[...])
            if out_scale is not None:
                r = r * out_scale
            o_ref[...] = r.astype(out_dtype)

        if nk == 1:
            finish(product())
            return
        acc_ref = refs[n_in + 1]
        k = pl.program_id(2)

        @pl.when(k == 0)
        def _():
            acc_ref[...] = product()

        @pl.when(k > 0)
        def _():
            acc_ref[...] += product()

        @pl.when(k == nk - 1)
        def _():
            finish(acc_ref[...])

    return pl.pallas_call(
        body, name=name, grid=grid, in_specs=in_specs, out_specs=out_spec, out_shape=out_shape,
        scratch_shapes=[pltpu.VMEM((tm, tn), F32)] if nk > 1 else [],
        compiler_params=pltpu.CompilerParams(dimension_semantics=("parallel", "parallel", "arbitrary")),
    )(*operands)


def _relu2(u):
    r = jnp.maximum(u.astype(F32), 0.0)
    return r * r


def _relu2_grad(acc, u):
    return acc * (2.0 * jnp.maximum(u.astype(F32), 0.0))


def rowwise(fn, rows, vecs, out_rows, n_acc, *, name, tile=ROW_TILE):
    S = rows[0].shape[0]
    tile = _tile(S, tile)
    n_rows, n_vecs, n_out = len(rows), len(vecs), len(out_rows)
    acc_cols = [None] * n_acc

    def body(*refs):
        ins = [r[...] for r in refs[:n_rows + n_vecs]]
        outs = refs[n_rows + n_vecs:]
        ro, ac = fn(*ins)
        assert len(ro) == n_out and len(ac) == n_acc
        for r, o in zip(outs[:n_out], ro):
            r[...] = o.astype(r.dtype)
        i = pl.program_id(0)
        for r, a in zip(outs[n_out:], ac):
            @pl.when(i == 0)
            def _():
                r[...] = jnp.zeros_like(r)
            r[...] += a

    acc_shapes = jax.eval_shape(
        lambda *xs: fn(*xs)[1],
        *[jax.ShapeDtypeStruct((tile, r.shape[1]), r.dtype) for r in rows],
        *[jax.ShapeDtypeStruct(v.shape, v.dtype) for v in vecs])
    in_specs = [pl.BlockSpec((tile, r.shape[1]), lambda i: (i, 0)) for r in rows]
    in_specs += [pl.BlockSpec(v.shape, lambda i: (0, 0)) for v in vecs]
    out_specs = [pl.BlockSpec((tile, c), lambda i: (i, 0)) for c, _ in out_rows]
    out_specs += [pl.BlockSpec(a.shape, lambda i: (0, 0)) for a in acc_shapes]
    out_shape = [jax.ShapeDtypeStruct((S, c), dt) for c, dt in out_rows]
    out_shape += [jax.ShapeDtypeStruct(a.shape, F32) for a in acc_shapes]
    del acc_cols
    return pl.pallas_call(
        body, name=name, grid=(S // tile,), in_specs=in_specs, out_specs=out_specs, out_shape=out_shape,
        compiler_params=pltpu.CompilerParams(dimension_semantics=("arbitrary",)),
    )(*rows, *vecs)


def _rms(x, g):
    r = lax.rsqrt(jnp.mean(x * x, axis=-1, keepdims=True) + EPS)
    return x * r * g


def _rms_bwd(x, g, dy):
    r = lax.rsqrt(jnp.mean(x * x, axis=-1, keepdims=True) + EPS)
    xh = x * r
    dyg = dy * g
    dx = r * (dyg - xh * jnp.mean(dyg * xh, axis=-1, keepdims=True))
    dg = jnp.sum(dy * xh, axis=0, keepdims=True)
    return dx, dg


def norm_only(x, g, dtype, *, name):
    D = x.shape[1]
    return rowwise(lambda xv, gv: ([_rms(xv, gv)], []), [x], [g], [(D, dtype)], 0, name=name)[0]


def residual_norms(x, m, g_post, next_gs, next_dtypes, *, name):
    D = x.shape[1]

    def fn(xv, mv, gp, *gs):
        xn = xv + _rms(mv, gp)
        return [xn] + [_rms(xn, g) for g in gs], []

    return rowwise(fn, [x, m], [g_post] + list(next_gs), [(D, F32)] + [(D, dt) for dt in next_dtypes], 0, name=name)


def residual_loss(x, d, g_post, target, *, name):
    D = x.shape[1]

    def fn(xv, dv, tv, gp):
        e = xv + _rms(dv, gp) - tv
        dy = e * (1.0 / D)
        dd, dg = _rms_bwd(dv, gp, dy)
        return [dy, dd], [jnp.sum(e * e, axis=0, keepdims=True) * (0.5 / D), dg]

    return rowwise(fn, [x, d, target], [g_post], [(D, F32), (D, BF16)], 2, name=name)


def mid_bwd(dy, x_mid, dh2, m, g_mlp_pre, g_mix_post, dm_dtype, *, name):
    D = dy.shape[1]

    def fn(dyv, xm, dh, mv, gpre, gpost):
        dx, dg_pre = _rms_bwd(xm, gpre, dh)
        dxm = dyv + dx
        dm, dg_post = _rms_bwd(mv, gpost, dxm)
        return [dxm, dm], [dg_pre, dg_post]

    return rowwise(fn, [dy, x_mid, dh2, m], [g_mlp_pre, g_mix_post], [(D, F32), (D, dm_dtype)], 2, name=name)


def pre_norm_bwd(dxm, x, dhs, gs, below=None, *, name):
    D = x.shape[1]
    n = len(dhs)
    rows = [dxm, x] + list(dhs) + ([below[0]] if below else [])
    vecs = list(gs) + ([below[1]] if below else [])

    def fn(*xs):
        dxv, xv, dh = xs[0], xs[1], xs[2:2 + n]
        g = xs[len(rows):]
        out, accs = dxv, []
        for k in range(n):
            dx, dg = _rms_bwd(xv, g[k], dh[k])
            out = out + dx
            accs.append(dg)
        if below is None:
            return [out], accs
        dd, dg = _rms_bwd(xs[2 + n], g[n], out)
        return [out, dd], accs + [dg]

    return rowwise(fn, rows, vecs, [(D, F32)] + ([(D, BF16)] if below else []), n + (1 if below else 0), name=name)


def sum_concat_cast(pairs, dtype, *, name):
    C = pairs[0][0].shape[1]
    n = len(pairs)

    def fn(*xs):
        return [jnp.concatenate([sum(xs[:n]), sum(xs[n:])], axis=1)], []

    return rowwise(fn, [a for a, _ in pairs] + [b for _, b in pairs], [], [(2 * C, dtype)], 0, name=name)[0]


def _window_sum(e, window, total_rows, backward):
    s, k = e, 1
    while k < window:
        s = s + pltpu.roll(s, (total_rows - k) if backward else k, 0)
        k *= 2
    return s


def pool_fwd(h, w, scale, *, name):
    S, D = h.shape
    G = len(POOL_WINDOWS)
    GC = D // G
    tile = _tile(S, ROW_TILE)
    hb = tile // HALO

    def body(hc_ref, hp_ref, w_ref, sc_ref, o_ref):
        i = pl.program_id(0)
        prev = jnp.where(i > 0, hp_ref[...], 0.0)
        ext = jnp.concatenate([prev, hc_ref[...]], axis=0)
        t = i * tile + lax.broadcasted_iota(jnp.int32, (tile, 1), 0)
        outs = []
        for g, window in enumerate(POOL_WINDOWS):
            e = ext[:, g * GC:(g + 1) * GC]
            s = _window_sum(e, window, HALO + tile, False)[HALO:, :]
            cnt = jnp.minimum(t + 1, window).astype(F32)
            y = s / cnt - e[HALO:, :]
            outs.append(jnp.dot(y.astype(BF16), w_ref[g], preferred_element_type=F32))
        o_ref[...] = jnp.concatenate(outs, axis=1) * sc_ref[...]

    return pl.pallas_call(
        body, name=name, grid=(S // tile,),
        in_specs=[pl.BlockSpec((tile, D), lambda i: (i, 0)),
                  pl.BlockSpec((HALO, D), lambda i: (jnp.maximum(i * hb - 1, 0), 0)),
                  pl.BlockSpec((G, GC, GC), lambda i: (0, 0, 0)),
                  pl.BlockSpec((1, D), lambda i: (0, 0))],
        out_specs=pl.BlockSpec((tile, D), lambda i: (i, 0)),
        out_shape=jax.ShapeDtypeStruct((S, D), F32),
        compiler_params=pltpu.CompilerParams(dimension_semantics=("parallel",)),
    )(h, h, w, scale)


def pool_bwd(h, dm, w, scale, *, name):
    S, D = h.shape
    G = len(POOL_WINDOWS)
    GC = D // G
    tile = _tile(S, ROW_TILE)
    hb = tile // HALO
    n_tiles = S // tile
    last_halo = S // HALO - 1

    def body(hc_ref, hp_ref, dmc_ref, dmn_ref, w_ref, sc_ref, dh_ref, dw_ref, dsc_ref):
        i = pl.program_id(0)

        @pl.when(i == 0)
        def _():
            dw_ref[...] = jnp.zeros_like(dw_ref)
            dsc_ref[...] = jnp.zeros_like(dsc_ref)

        prev = jnp.where(i > 0, hp_ref[...], 0.0)
        ext = jnp.concatenate([prev, hc_ref[...]], axis=0)
        nxt = jnp.where(i < n_tiles - 1, dmn_ref[...], 0.0)
        dmc = dmc_ref[...]
        dm_ext = jnp.concatenate([dmc, nxt], axis=0)
        t = i * tile + lax.broadcasted_iota(jnp.int32, (tile, 1), 0)
        t_ext = i * tile + lax.broadcasted_iota(jnp.int32, (tile + HALO, 1), 0)
        dhs, dscs = [], []
        for g, window in enumerate(POOL_WINDOWS):
            cols = slice(g * GC, (g + 1) * GC)
            e = ext[:, cols]
            s = _window_sum(e, window, HALO + tile, False)[HALO:, :]
            y = (s / jnp.minimum(t + 1, window).astype(F32) - e[HALO:, :]).astype(BF16)
            wg = w_ref[g]
            ypre = jnp.dot(y, wg, preferred_element_type=F32)
            dscs.append(jnp.sum(dmc[:, cols] * ypre, axis=0, keepdims=True))
            dyp = (dm_ext[:, cols] * sc_ref[:, cols]).astype(BF16)
            dw_ref[g] += lax.dot_general(y, dyp[:tile, :], (((0,), (0,)), ((), ())), preferred_element_type=F32)
            dy = lax.dot_general(dyp, wg, (((1,), (1,)), ((), ())), preferred_element_type=F32)
            r = dy / jnp.minimum(t_ext + 1, window).astype(F32)
            sr = _window_sum(r, window, tile + HALO, True)
            dhs.append(sr[:tile, :] - dy[:tile, :])
        dh_ref[...] = jnp.concatenate(dhs, axis=1)
        dsc_ref[...] += jnp.concatenate(dscs, axis=1)

    return pl.pallas_call(
        body, name=name, grid=(n_tiles,),
        in_specs=[pl.BlockSpec((tile, D), lambda i: (i, 0)),
                  pl.BlockSpec((HALO, D), lambda i: (jnp.maximum(i * hb - 1, 0), 0)),
                  pl.BlockSpec((tile, D), lambda i: (i, 0)),
                  pl.BlockSpec((HALO, D), lambda i: (jnp.minimum((i + 1) * hb, last_halo), 0)),
                  pl.BlockSpec((G, GC, GC), lambda i: (0, 0, 0)),
                  pl.BlockSpec((1, D), lambda i: (0, 0))],
        out_specs=[pl.BlockSpec((tile, D), lambda i: (i, 0)),
                   pl.BlockSpec((G, GC, GC), lambda i: (0, 0, 0)),
                   pl.BlockSpec((1, D), lambda i: (0, 0))],
        out_shape=[jax.ShapeDtypeStruct((S, D), F32), jax.ShapeDtypeStruct((G, GC, GC), F32),
                   jax.ShapeDtypeStruct((1, D), F32)],
        compiler_params=pltpu.CompilerParams(dimension_semantics=("arbitrary",)),
    )(h, h, dm, dm, w, scale)


ATT_LANES = 256
N_PAIR = ATT_LANES // HEAD_DIM
KEY_TILES = 3
KEY_TILES_BWD = 3


def _scores(xs, kj):
    return [lax.dot_general(x, kj, (((1,), (1,)), ((), ())), preferred_element_type=F32) for x in xs]


def _softplus_parts(z, mask):
    sp = jnp.maximum(z, 0.0) + jnp.log(1.0 + jnp.exp(-jnp.abs(z)))
    logb = z - sp
    if mask is not None:
        sp = jnp.where(mask, sp, 0.0)
    return logb, sp.astype(BF16), jnp.sum(sp, axis=1, keepdims=True)


def _weights(logb, later, c, mask):
    a = jnp.exp(logb - (later + c))
    return a if mask is None else jnp.where(mask, a, 0.0)


def _tile_weights(zss, cs, u_later, masks):
    partss = [[_softplus_parts(z, m) for z in zs] for zs, m in zip(zss, masks)]
    laterss = [[jnp.dot(sp, u_later, preferred_element_type=F32) for _, sp, _ in parts] for parts in partss]
    out = []
    for parts, laters, m in zip(partss, laterss, masks):
        out.append(([p[0] for p in parts], [_weights(p[0], later, c, m) for p, later, c in zip(parts, laters, cs)]))
        cs = [c + p[2] for c, p in zip(cs, parts)]
    return out, tuple(cs)


def _tri(T, later):
    rows = lax.broadcasted_iota(jnp.int32, (T, T), 0)
    cols = lax.broadcasted_iota(jnp.int32, (T, T), 1)
    return jnp.where((rows > cols) if later else (rows < cols), 1.0, 0.0).astype(BF16)


def _head_masks(x2, axis=None):
    lane = lax.broadcasted_iota(jnp.int32, (1, ATT_LANES), 1)
    parts = [jnp.where((lane // HEAD_DIM) == hh, x2, jnp.zeros_like(x2)) for hh in range(N_PAIR)]
    return parts if axis is None else jnp.concatenate(parts, axis=axis)


def _cat_bf16(parts, axis):
    return jnp.concatenate([p.astype(BF16) for p in parts], axis=axis)


def _side_split(side, refs, n_in, n_out):
    if side is None:
        return refs, None
    n_src, n_buf = len(side.srcs), len(side.bufs)
    ins, rest = refs[:n_in], refs[n_in:]
    src_refs, rest = rest[:n_src], rest[n_src + n_buf:]
    outs, rest = rest[:n_out], rest[n_out:]
    buf_refs, rest = rest[:n_buf], rest[n_buf:]
    own_scratch, sems = rest[:len(rest) - 3], rest[len(rest) - 3:]
    return tuple(ins) + tuple(outs) + tuple(own_scratch), (src_refs, buf_refs, sems)


def _side_phase(side, side_refs, phase, when):
    if side is None:
        return

    @pl.when(when)
    def _():
        side.copies(phase, *side_refs)


def attn_fwd(q, kv, *, name, side=None):
    S, D = q.shape
    P = D // ATT_LANES
    T = _tile(S, ATT_TILE)
    nq = S // T

    def body(*refs):
        (q_ref, k_ref, v_ref, o_ref), side_refs = _side_split(side, refs, 3, 1)
        i = pl.program_id(1)
        p = pl.program_id(0)
        _side_phase(side, side_refs, 0, (p == 0) & (i == 0))
        u_later = _tri(T, True)
        qhs = _head_masks(q_ref[...])
        diag = lax.broadcasted_iota(jnp.int32, (T, T), 1) < lax.broadcasted_iota(jnp.int32, (T, T), 0)

        def rows_of(ref, j):
            return ref[pl.ds(pl.multiple_of(j * T, T), T), :]

        def step(js, carry, masks=None):
            cs, acc = carry
            zss = [_scores(qhs, rows_of(k_ref, j)) for j in js]
            vcat = jnp.concatenate([_head_masks(rows_of(v_ref, j), axis=0) for j in js], axis=0)
            per_tile, cs = _tile_weights(zss, cs, u_later, masks or [None] * len(js))
            acat = _cat_bf16([a for _, aa in per_tile for a in aa], 1)
            return cs, acc + jnp.dot(acat, vcat, preferred_element_type=F32)

        G = KEY_TILES
        carry = ((jnp.zeros((T, 1), F32),) * N_PAIR, jnp.zeros((T, ATT_LANES), F32))
        carry = lax.fori_loop(0, (i == 0).astype(jnp.int32), lambda n, cr: step([i], cr, [diag]), carry)
        carry = lax.fori_loop(0, (i > 0).astype(jnp.int32), lambda n, cr: step([i, i - 1], cr, [diag, None]), carry)
        left = jnp.maximum(i - 1, 0)
        carry = lax.fori_loop(0, left % G, lambda n, cr: step([left - 1 - n], cr), carry)
        first = left - 1 - left % G
        _, acc = lax.fori_loop(0, left // G, lambda n, cr: step([first - G * n - t for t in range(G)], cr), carry)
        o_ref[...] = acc.astype(o_ref.dtype)
        _side_phase(side, side_refs, 1, (p == P - 1) & (i == nq - 1))

    sd = side
    outs = pl.pallas_call(
        body, name=name, grid=(P, nq),
        in_specs=[pl.BlockSpec((T, ATT_LANES), lambda p, i: (i, p)),
                  pl.BlockSpec((S, ATT_LANES), lambda p, i: (0, p)),
                  pl.BlockSpec((S, ATT_LANES), lambda p, i: (0, P + p))] + (sd.specs() if sd else []),
        out_specs=[pl.BlockSpec((T, ATT_LANES), lambda p, i: (i, p))] + (sd.out_specs() if sd else []),
        out_shape=[jax.ShapeDtypeStruct((S, D), BF16)] + (sd.out_shape() if sd else []),
        scratch_shapes=sd.scratch() if sd else [],
        input_output_aliases=sd.aliases(3, 1) if sd else {},
        compiler_params=pltpu.CompilerParams(dimension_semantics=("arbitrary", "arbitrary"),
                                             has_side_effects=sd is not None),
    )(q, kv, kv, *(sd.operands() if sd else []))
    return outs[0] if sd is None else (outs[0], sd.result(outs[1:]))


def attn_bwd(q, kv, do, *, name, side=None):
    S, D = q.shape
    P = D // ATT_LANES
    T = _tile(S, ATT_TILE)
    nb = S // T

    def body(*refs):
        (q_ref, k_ref, v_ref, do_ref, dq_ref, dk_ref, dv_ref, g_scr, s_scr), side_refs = _side_split(side, refs, 4, 3)
        i = pl.program_id(1)
        p = pl.program_id(0)
        _side_phase(side, side_refs, 0, (p == 0) & (i == 0))

        @pl.when(i == 0)
        def _():
            dk_ref[...] = jnp.zeros_like(dk_ref)
            dv_ref[...] = jnp.zeros_like(dv_ref)

        u_later = _tri(T, True)
        u_earlier = _tri(T, False)
        qhs = _head_masks(q_ref[...])
        dohs = _head_masks(do_ref[...])
        qcat = jnp.concatenate(qhs, axis=0)
        docat = jnp.concatenate(dohs, axis=0)
        diag = lax.broadcasted_iota(jnp.int32, (T, T), 1) < lax.broadcasted_iota(jnp.int32, (T, T), 0)
        tdot = (((0,), (0,)), ((), ()))

        def rows_of(ref, j):
            return ref.at[pl.ds(pl.multiple_of(j * T, T), T), :]

        def step1(js, cs, masks=None):
            masks = masks or [None] * len(js)
            dass = [_scores(dohs, rows_of(v_ref, j)[...]) for j in js]
            zss = [_scores(qhs, rows_of(k_ref, j)[...]) for j in js]
            per_tile, cs = _tile_weights(zss, cs, u_later, masks)
            for j, das, (logbs, aa), m in zip(js, dass, per_tile, masks):
                for hh in range(N_PAIR):
                    g_scr[hh, j] = (das[hh] * aa[hh]).astype(BF16)
                    sg = jnp.exp(logbs[hh])
                    if m is not None:
                        sg = jnp.where(m, sg, 0.0)
                    s_scr[hh, j] = sg.astype(BF16)
            for j, (_, aa) in zip(js, per_tile):
                rows_of(dv_ref, j)[...] += lax.dot_general(_cat_bf16(aa, 0), docat, tdot,
                                                          preferred_element_type=F32)
            return cs

        G = KEY_TILES_BWD
        cs = (jnp.zeros((T, 1), F32),) * N_PAIR
        cs = lax.fori_loop(0, (i == 0).astype(jnp.int32), lambda n, c: step1([i], c, [diag]), cs)
        cs = lax.fori_loop(0, (i > 0).astype(jnp.int32), lambda n, c: step1([i, i - 1], c, [diag, None]), cs)
        left = jnp.maximum(i - 1, 0)
        cs = lax.fori_loop(0, left % G, lambda n, c: step1([left - 1 - n], c), cs)
        first = left - 1 - left % G
        lax.fori_loop(0, left // G, lambda n, c: step1([first - G * n - t for t in range(G)], c), cs)

        def step2(js, carry):
            cs, acc = carry
            cumss = [[jnp.dot(g_scr[hh, j], u_earlier, preferred_element_type=F32) for hh in range(N_PAIR)]
                     for j in js]
            kcat = jnp.concatenate([_head_masks(rows_of(k_ref, j)[...], axis=0) for j in js], axis=0)
            dzss = []
            for j, cums in zip(js, cumss):
                dzs, new = [], []
                for hh in range(N_PAIR):
                    gf = g_scr[hh, j].astype(F32)
                    sg = s_scr[hh, j].astype(F32)
                    dzs.append((gf - sg * (gf + (cums[hh] + cs[hh]))).astype(BF16))
                    new.append(cs[hh] + jnp.sum(gf, axis=1, keepdims=True))
                cs = tuple(new)
                dzss.append(dzs)
            acc = acc + jnp.dot(jnp.concatenate([dz for dzs in dzss for dz in dzs], axis=1), kcat,
                                preferred_element_type=F32)
            for j, dzs in zip(js, dzss):
                rows_of(dk_ref, j)[...] += lax.dot_general(jnp.concatenate(dzs, axis=0), qcat, tdot,
                                                          preferred_element_type=F32)
            return cs, acc

        carry = ((jnp.zeros((T, 1), F32),) * N_PAIR, jnp.zeros((T, ATT_LANES), F32))
        carry = lax.fori_loop(0, (i + 1) // G, lambda n, cr: step2([G * n + t for t in range(G)], cr), carry)
        done = (i + 1) // G * G
        _, dq = lax.fori_loop(0, (i + 1) % G, lambda n, cr: step2([done + n], cr), carry)
        dq_ref[...] = (dq * (HEAD_DIM ** -0.5)).astype(dq_ref.dtype)
        _side_phase(side, side_refs, 1, (p == P - 1) & (i == nb - 1))

    sd = side
    outs = pl.pallas_call(
        body, name=name, grid=(P, nb),
        in_specs=[pl.BlockSpec((T, ATT_LANES), lambda p, i: (i, p)),
                  pl.BlockSpec((S, ATT_LANES), lambda p, i: (0, p)),
                  pl.BlockSpec((S, ATT_LANES), lambda p, i: (0, P + p)),
                  pl.BlockSpec((T, ATT_LANES), lambda p, i: (i, p))] + (sd.specs() if sd else []),
        out_specs=[pl.BlockSpec((T, ATT_LANES), lambda p, i: (i, p)),
                   pl.BlockSpec((S, ATT_LANES), lambda p, i: (0, p)),
                   pl.BlockSpec((S, ATT_LANES), lambda p, i: (0, p))] + (sd.out_specs() if sd else []),
        out_shape=[jax.ShapeDtypeStruct((S, D), BF16), jax.ShapeDtypeStruct((S, D), F32),
                   jax.ShapeDtypeStruct((S, D), F32)] + (sd.out_shape() if sd else []),
        scratch_shapes=[pltpu.VMEM((N_PAIR, nb, T, T), BF16), pltpu.VMEM((N_PAIR, nb, T, T), BF16)]
        + (sd.scratch() if sd else []),
        input_output_aliases=sd.aliases(4, 3) if sd else {},
        compiler_params=pltpu.CompilerParams(dimension_semantics=("arbitrary", "arbitrary"),
                                             has_side_effects=sd is not None),
    )(q, kv, kv, do, *(sd.operands() if sd else []))
    return tuple(outs[:3]) if sd is None else (*outs[:3], sd.result(outs[3:]))


def _window(ref, axis, dev, n):
    return ref.at[(slice(None),) * axis + (pl.ds(dev * n, n),)]


def all_gather(ops, *, name):
    n_ops = len(ops)
    out_shape = []
    for a, ax in ops:
        shp = list(a.shape)
        shp[ax] *= N_DEV
        out_shape.append(jax.ShapeDtypeStruct(tuple(shp), a.dtype))

    def body(*refs):
        ins, outs = refs[:n_ops], refs[n_ops:2 * n_ops]
        send_sems, recv_sems, local_sems = refs[2 * n_ops:]
        x, y, c = (lax.axis_index(n) for n in MESH_AXES)
        me, sibling = (x, y, c), (x, y, 1 - c)
        chips = [(1 - x, y), (x, 1 - y), (1 - x, 1 - y)]

        def rows(o, dev):
            px, py, pc = dev
            ax = ops[o][1]
            return _window(outs[o], ax, 4 * px + 2 * py + pc, ops[o][0].shape[ax])

        def copy(o, k, block, to, src=None):
            return pltpu.make_async_remote_copy(
                src_ref=rows(o, block) if src is None else src, dst_ref=rows(o, block),
                send_sem=send_sems.at[o, k], recv_sem=recv_sems.at[o, k],
                device_id=to, device_id_type=pl.DeviceIdType.MESH)

        mine, first, passed = [], [], []
        for o in range(n_ops):
            cp = pltpu.make_async_copy(ins[o], rows(o, me), local_sems.at[o])
            cp.start()
            mine.append(cp)
            first.append(copy(o, 0, me, sibling, src=ins[o]))
            first += [copy(o, 1 + j, me, (*chip, c), src=ins[o]) for j, chip in enumerate(chips)]
        for cp in first:
            cp.start()
        for j, chip in enumerate(chips):
            for o in range(n_ops):
                copy(o, 1 + j, (*chip, c), me).wait_recv()
                cp = copy(o, 4 + j, (*chip, c), sibling)
                cp.start()
                passed.append(cp)
        for o in range(n_ops):
            copy(o, 0, sibling, me).wait_recv()
            for j, chip in enumerate(chips):
                copy(o, 4 + j, (*chip, 1 - c), me).wait_recv()
        for cp in first + passed:
            cp.wait_send()
        for cp in mine:
            cp.wait()

    any_spec = pl.BlockSpec(memory_space=pl.ANY)
    return pl.pallas_call(
        body, name=name, in_specs=[any_spec] * n_ops, out_specs=[any_spec] * n_ops, out_shape=out_shape,
        scratch_shapes=[pltpu.SemaphoreType.DMA((n_ops, 7)), pltpu.SemaphoreType.DMA((n_ops, 7)),
                        pltpu.SemaphoreType.DMA((n_ops,))],
        compiler_params=pltpu.CompilerParams(has_side_effects=True),
    )(*[a for a, _ in ops])


class Side:
    def __init__(self, kind, items, bufs):
        self.kind, self.items = kind, items
        self.names = list(bufs)
        self.bufs = [bufs[n] for n in self.names]
        self.srcs = [it[0] for it in items]

    def operands(self):
        return self.srcs + self.bufs

    def specs(self):
        return [pl.BlockSpec(memory_space=pl.ANY)] * (len(self.srcs) + len(self.bufs))

    def out_specs(self):
        return [pl.BlockSpec(memory_space=pl.ANY)] * len(self.bufs)

    def out_shape(self):
        return [jax.ShapeDtypeStruct(b.shape, b.dtype) for b in self.bufs]

    def aliases(self, first_in, first_out):
        return {first_in + len(self.srcs) + k: first_out + k for k in range(len(self.bufs))}

    def scratch(self):
        n = len(self.items)
        return [pltpu.SemaphoreType.DMA((n, N_DEV - 1)), pltpu.SemaphoreType.DMA((n, N_DEV - 1)),
                pltpu.SemaphoreType.DMA((n,))]

    def result(self, outs):
        return dict(zip(self.names, outs))

    def copies(self, phase, src_refs, buf_refs, sems):
        send_sems, recv_sems, local_sems = sems
        pos = tuple(lax.axis_index(n) for n in MESH_AXES)
        me = 4 * pos[0] + 2 * pos[1] + pos[2]
        for o, (_, ax, name, layer) in enumerate(self.items):
            src, buf = src_refs[o], buf_refs[self.names.index(name)]
            if self.kind == "gather":
                whole = buf if layer is None else buf.at[layer]
                n = src.shape[ax]
                sent = lambda dev, src=src: src
                lands = lambda dev, whole=whole, ax=ax, n=n: _window(whole, ax, dev, n)
            else:
                n = src.shape[ax] // N_DEV
                sent = lambda dev, src=src, ax=ax, n=n: _window(src, ax, dev, n)
                lands = lambda dev, buf=buf, layer=layer: buf.at[dev] if layer is None else buf.at[dev, layer]
            if local_sems is not None:
                local = pltpu.make_async_copy(sent(me), lands(me), local_sems.at[o])
                if phase == 0:
                    local.start()
                else:
                    local.wait()
            for r in range(1, N_DEV):
                peer = tuple(1 - p if r & bit else p for p, bit in zip(pos, (4, 2, 1)))
                pid = 4 * peer[0] + 2 * peer[1] + peer[2]
                cp = pltpu.make_async_remote_copy(
                    src_ref=sent(pid) if phase == 0 else sent(me), dst_ref=lands(me) if phase == 0 else lands(pid),
                    send_sem=send_sems[o] if isinstance(send_sems, (list, tuple)) else send_sems.at[o, r - 1],
                    recv_sem=recv_sems[o] if isinstance(recv_sems, (list, tuple)) else recv_sems.at[o, r - 1],
                    device_id=peer, device_id_type=pl.DeviceIdType.MESH)
                if phase == 0:
                    cp.start()
                else:
                    cp.wait_recv()
                    cp.wait_send()


def _hbm(a):
    return pltpu.with_memory_space_constraint(a, pltpu.HBM)


def own_blocks(items, bufs):
    me = 4 * lax.axis_index("x") + 2 * lax.axis_index("y") + lax.axis_index("c")
    bufs = {name: bufs[name] for _, _, name, _ in items}
    for src, ax, name, layer in items:
        n = src.shape[ax] // N_DEV
        blk = lax.dynamic_slice_in_dim(src, me * n, n, axis=ax)
        lead = (me,) if layer is None else (me, layer)
        blk = blk.reshape((1,) * len(lead) + blk.shape)
        bufs[name] = lax.dynamic_update_slice(bufs[name], blk, lead + (0,) * src.ndim)
    return bufs


def start_side(side, *, name):
    n_src, n_buf = len(side.srcs), len(side.bufs)
    n = len(side.items)

    def body(*refs):
        src_refs, buf_refs = refs[:n_src], refs[n_src:n_src + n_buf]
        sems = refs[n_src + n_buf:n_src + n_buf + 2 * n]
        token = refs[-1]
        side.copies(0, src_refs, buf_refs, (list(sems[:n]), list(sems[n:]), None))
        token[...] = jnp.zeros_like(token)

    hbm = pl.BlockSpec(memory_space=pltpu.HBM)
    sem = pl.BlockSpec(memory_space=pltpu.SEMAPHORE)
    operands = side.operands()
    outs = pl.pallas_call(
        body, name=name,
        out_shape=(*[pltpu.SemaphoreType.DMA(())] * (2 * n),
                   *[pltpu.HBM(a.shape, a.dtype) for a in operands], jax.ShapeDtypeStruct((8, LANES), F32)),
        in_specs=[hbm] * len(operands),
        out_specs=(*[sem] * (2 * n), *[hbm] * len(operands), pl.BlockSpec(memory_space=pltpu.VMEM)),
        input_output_aliases={k: 2 * n + k for k in range(len(operands))},
        compiler_params=pltpu.CompilerParams(has_side_effects=pltpu.SideEffectType.DATAFLOW_SIDE_EFFECTING),
    )(*[_hbm(a) for a in operands])
    started = (side, list(outs[:n]), list(outs[n:2 * n]), list(outs[2 * n:2 * n + n_src]))
    return started, dict(zip(side.names, outs[2 * n + n_src:2 * n + n_src + n_buf])), outs[-1]


def wait_sides(started, bufs, after, *, name):
    names = list(bufs)
    flat = []
    for side, send_sems, recv_sems, srcs in started:
        flat += [*srcs, *send_sems, *recv_sems]
    n_buf = len(names)

    def body(*refs):
        buf_refs = refs[:n_buf]
        rest = refs[n_buf:]
        for side, _, _, srcs in started:
            n = len(srcs)
            src_refs, send_sems, recv_sems = rest[:n], list(rest[n:2 * n]), list(rest[2 * n:3 * n])
            rest = rest[3 * n:]
            side.copies(1, src_refs, [buf_refs[names.index(nm)] for nm in side.names], (send_sems, recv_sems, None))

    hbm = pl.BlockSpec(memory_space=pltpu.HBM)
    sem = pl.BlockSpec(memory_space=pltpu.SEMAPHORE)
    specs = []
    for _, _, _, srcs in started:
        specs += [hbm] * len(srcs) + [sem] * (2 * len(srcs))
    outs = pl.pallas_call(
        body, name=name,
        out_shape=tuple(pltpu.HBM(bufs[nm].shape, bufs[nm].dtype) for nm in names),
        in_specs=[hbm] * n_buf + specs + [pl.BlockSpec(memory_space=pl.ANY)],
        out_specs=tuple([hbm] * n_buf),
        input_output_aliases={k: k for k in range(n_buf)},
        compiler_params=pltpu.CompilerParams(has_side_effects=pltpu.SideEffectType.DATAFLOW_SIDE_EFFECTING),
    )(*[bufs[nm] for nm in names], *flat, after)
    return dict(zip(names, outs))


def _adamw(w, g, m, v):
    m = ADAM_B1 * m + (1.0 - ADAM_B1) * g
    v = ADAM_B2 * v + (1.0 - ADAM_B2) * (g * g)
    m_hat = m / (1.0 - ADAM_B1 ** ADAM_STEP)
    v_hat = v / (1.0 - ADAM_B2 ** ADAM_STEP)
    delta = -ADAM_LR * (m_hat / (jnp.sqrt(v_hat) + ADAM_EPS) + ADAM_WD * w)
    return delta, m, v


def reduce_adamw(parts, w, m, v, *, name):
    shape = w.shape
    C = shape[-1]
    R = w.size // C
    tile = _tile(R, 256)

    def body(p_ref, w_ref, m_ref, v_ref, g_ref, d_ref, nm_ref, nv_ref):
        g = p_ref[0].astype(F32)
        for s in range(1, N_DEV):
            g = g + p_ref[s].astype(F32)
        d, nm, nv = _adamw(w_ref[...], g, m_ref[...], v_ref[...])
        g_ref[...] = g
        d_ref[...] = d
        nm_ref[...] = nm
        nv_ref[...] = nv

    row = pl.BlockSpec((tile, C), lambda i: (i, 0))
    outs = pl.pallas_call(
        body, name=name, grid=(R // tile,),
        in_specs=[pl.BlockSpec((N_DEV, tile, C), lambda i: (0, i, 0)), row, row, row],
        out_specs=[row] * 4, out_shape=[jax.ShapeDtypeStruct((R, C), F32)] * 4,
        compiler_params=pltpu.CompilerParams(dimension_semantics=("parallel",)),
    )(parts.reshape(N_DEV, R, C), w.reshape(R, C), m.reshape(R, C), v.reshape(R, C))
    return [o.reshape(shape) for o in outs]


def gains_adamw(parts, groups, *, name):
    n = len(groups)

    def body(*refs):
        p_ref = refs[0]
        ins, outs = refs[1:1 + 3 * n], refs[1 + 3 * n:]
        for k in range(n):
            w_ref, m_ref, v_ref = ins[3 * k:3 * k + 3]
            L = w_ref.shape[0]
            g = p_ref[pl.ds(8 * k, L), :]
            for s in range(1, N_DEV):
                g = g + p_ref[pl.ds(s * GAIN_ROWS + 8 * k, L), :]
            d, nm, nv = _adamw(w_ref[...], g, m_ref[...], v_ref[...])
            for r, val in zip(outs[4 * k:4 * k + 4], (g, d, nm, nv)):
                r[...] = val

    flat = [a for grp in groups for a in grp]
    out_shape = [jax.ShapeDtypeStruct(grp[0].shape, F32) for grp in groups for _ in range(4)]
    outs = pl.pallas_call(body, name=name, out_shape=out_shape)(parts, *flat)
    return [outs[4 * k:4 * k + 4] for k in range(n)]


def kernel(x, pool_w, pool_scale, w_q, w_kv, kv_norm_g, w_o, w_up, w_down, mix_pre_g, mix_post_g, mlp_pre_g, mlp_post_g, loss_target, m_pool_w, m_pool_scale, m_w_q, m_w_kv, m_kv_norm_g, m_w_o, m_w_up, m_w_down, m_mix_pre_g, m_mix_post_g, m_mlp_pre_g, m_mlp_post_g, v_pool_w, v_pool_scale, v_w_q, v_w_kv, v_kv_norm_g, v_w_o, v_w_up, v_w_down, v_mix_pre_g, v_mix_post_g, v_mlp_pre_g, v_mlp_post_g):
    _, S, D = x.shape
    x0 = x.reshape(S, D)
    target = loss_target.reshape(S, D)
    depth = w_up.shape[0]
    n_pool = pool_w.shape[0]
    F = w_up.shape[2] * N_DEV
    G = pool_w.shape[1]
    GC = D // G

    def vec(a, l):
        return a[l].reshape(1, D)

    n_att = depth - n_pool
    wq_s, wkv_s, wo_s = w_q.astype(BF16), w_kv.astype(BF16), w_o.astype(BF16)
    wup_s, wdn_s, pw_s = w_up.astype(BF16), w_down.astype(BF16), pool_w.astype(BF16)
    wq0, wkv, wup_a, wdn_a, pw, psc = all_gather(
        [(wq_s[0], 0), (wkv_s, 1), (wup_s[:n_pool], 2), (wdn_s[:n_pool], 1), (pw_s, 2), (pool_scale, 1)],
        name="gather_weights")
    late = Side("gather",
                [(wq_s[1:], 1, "wq", None), (wo_s, 1, "wo", None), (wup_s[n_pool:], 2, "wup", None),
                 (wdn_s[n_pool:], 1, "wdn", None)],
                {"wq": lax.empty((n_att - 1, D, D), BF16), "wo": lax.empty((n_att, D, D), BF16),
                 "wup": lax.empty((n_att, D, F), BF16), "wdn": lax.empty((n_att, F, D), BF16)})
    late_w = None

    def layer_of(early, key, l, n_early):
        return (early, l) if l < n_early else (late_w[key], l - n_early)

    saved = []
    xs = x0
    h1 = norm_only(xs, vec(mix_pre_g, 0), F32, name="norm_in")
    kv = hk = None
    dy = loss_rows = None
    for l in range(depth):
        is_pool = l < n_pool
        st = {"x": xs, "h1": h1}
        if is_pool:
            m = pool_fwd(h1, pw[l], psc[l].reshape(1, D), name=f"pool_fwd{l}")
        else:
            j = l - n_pool
            wq_j, wq_l = (wq0, None) if j == 0 else (late_w["wq"], j - 1)
            q = matmul(h1, wq_j, b_layer=wq_l, out_dtype=BF16, out_scale=HEAD_DIM ** -0.5, name=f"q_proj{j}")
            if j == 0:
                o, late_w = attn_fwd(q, kv, name=f"attn_fwd{j}", side=late)
            else:
                o = attn_fwd(q, kv, name=f"attn_fwd{j}")
            m = matmul(o, late_w["wo"], b_layer=j, name=f"o_proj{j}")
            st.update(q=q, o=o)
        x_mid, h2 = residual_norms(xs, m, vec(mix_post_g, l), [vec(mlp_pre_g, l)], [BF16], name=f"mix_out{l}")
        wup, lu = layer_of(wup_a, "wup", l, n_pool)
        wdn, ld = layer_of(wdn_a, "wdn", l, n_pool)
        u = matmul(h2, wup, b_layer=lu, out_dtype=BF16, tm=2048, name=f"mlp_up{l}")
        d = matmul(u, wdn, b_layer=ld, a_fn=_relu2, tm=2048, name=f"mlp_down{l}")
        st.update(m=m, x_mid=x_mid, h2=h2, u=u, d=d)
        saved.append(st)
        if l == depth - 1:
            dy, dd, loss_rows, dg_last = residual_loss(x_mid, d, vec(mlp_post_g, l), target, name="loss")
        elif l == n_pool - 1:
            xs, h1, hk = residual_norms(x_mid, d, vec(mlp_post_g, l), [vec(mix_pre_g, l + 1), kv_norm_g.reshape(1, D)],
                                        [BF16, BF16], name=f"mlp_out{l}")
            kv = matmul(hk, wkv, out_dtype=BF16, name="kv_proj")
        else:
            nxt_dt = F32 if l + 1 < n_pool else BF16
            xs, h1 = residual_norms(x_mid, d, vec(mlp_post_g, l), [vec(mix_pre_g, l + 1)], [nxt_dt], name=f"mlp_out{l}")
    loss = lax.psum(jnp.sum(loss_rows), MESH_AXES)

    recv = {"w_q": (w_q, BF16), "w_kv": (w_kv, BF16), "w_o": (w_o, BF16), "w_up": (w_up, BF16),
            "w_down": (w_down, BF16), "pool_w": (pool_w, F32), "pool_scale": (pool_scale, F32)}
    recv = {n: lax.empty((N_DEV,) + w.shape, dt) for n, (w, dt) in recv.items()}
    pending = []

    def send_pending():
        return Side("scatter", list(pending), {n: recv[n] for n in dict.fromkeys(it[2] for it in pending)})

    started = []

    def start_pending(tag, then):
        recv.update(own_blocks(pending, recv))
        begun, got, token = start_side(send_pending(), name=f"exchange_start_{tag}")
        recv.update(got)
        started.append(begun)
        pending.clear()
        return lax.optimization_barrier((then, token))[0]

    g_pw, g_psc = [None] * n_pool, [None] * n_pool
    gains = {k: [None] * depth for k in ("mix_pre", "mix_post", "mlp_pre", "mlp_post")}
    gains["mlp_post"][depth - 1] = dg_last
    dkvs = []
    g_kvn = None
    for l in reversed(range(depth)):
        st = saved[l]
        is_pool = l < n_pool
        wup, lu = layer_of(wup_a, "wup", l, n_pool)
        wdn, ld = layer_of(wdn_a, "wdn", l, n_pool)
        du = matmul(dd, wdn, b_layer=ld, tb=True, out_dtype=BF16, epi=_relu2_grad, epi_in=st["u"], tm=2048,
                    name=f"mlp_du{l}")
        g_wdn = matmul(st["u"], dd, ta=True, a_fn=_relu2, out_dtype=BF16, name=f"mlp_dwdn{l}")
        g_wup = matmul(st["h2"], du, ta=True, out_dtype=BF16, name=f"mlp_dwup{l}")
        pending += [(g_wup, 1, "w_up", l), (g_wdn, 0, "w_down", l)]
        if is_pool:
            du = start_pending(f"l{l}", du)
        dh2 = matmul(du, wup, b_layer=lu, tb=True, tm=2048, name=f"mlp_dh{l}")
        dxm, dm, gains["mlp_pre"][l], gains["mix_post"][l] = mid_bwd(
            dy, st["x_mid"], dh2, st["m"], vec(mlp_pre_g, l), vec(mix_post_g, l), F32 if is_pool else BF16,
            name=f"mid_bwd{l}")
        if is_pool:
            dh1, g_pw[l], g_psc[l] = pool_bwd(st["h1"], dm, pw[l], psc[l].reshape(1, D), name=f"pool_bwd{l}")
        else:
            j = l - n_pool
            wq_j, wq_l = (wq0, None) if j == 0 else (late_w["wq"], j - 1)
            do = matmul(dm, late_w["wo"], b_layer=j, tb=True, out_dtype=BF16, name=f"o_proj_dx{j}")
            g_wo = matmul(st["o"], dm, ta=True, out_dtype=BF16, name=f"o_proj_dw{j}")
            pending.append((g_wo, 0, "w_o", j))
            dq, dk, dv, got = attn_bwd(st["q"], kv, do, name=f"attn_bwd{j}", side=send_pending())
            recv.update(got)
            pending.clear()
            dkvs.append((dk, dv))
            g_wq = matmul(st["h1"], dq, ta=True, out_dtype=BF16, name=f"q_proj_dw{j}")
            pending.append((g_wq, 0, "w_q", j))
            dh1 = matmul(dq, wq_j, b_layer=wq_l, tb=True, name=f"q_proj_dx{j}")
        if l == n_pool:
            dkv = sum_concat_cast(dkvs, BF16, name="dkv_pack")
            g_wkv = matmul(hk, dkv, ta=True, out_dtype=BF16, name="kv_proj_dw")
            pending.append((g_wkv, 1, "w_kv", None))
            dhk = matmul(dkv, wkv, tb=True, name="kv_proj_dx")
            dhs, gs = [dh1, dhk], [vec(mix_pre_g, l), kv_norm_g.reshape(1, D)]
        else:
            dhs, gs = [dh1], [vec(mix_pre_g, l)]
        below = (saved[l - 1]["d"], vec(mlp_post_g, l - 1)) if l > 0 else None
        outs = pre_norm_bwd(dxm, st["x"], dhs, gs, below, name=f"mix_in_bwd{l}")
        dy, outs = outs[0], outs[1:]
        if below is not None:
            dd, outs = outs[0], outs[1:]
            gains["mlp_post"][l - 1] = outs[-1]
        gains["mix_pre"][l] = outs[0]
        if l == n_pool:
            g_kvn = outs[1]
    grad_x = dy.reshape(x.shape)

    g_pool_w = jnp.stack(g_pw)
    g_pool_scale = jnp.concatenate(g_psc, axis=0)
    zero_rows = jnp.zeros((8 - depth, D), F32)
    gain_rows = []
    for k in ("mix_pre", "mix_post", "mlp_pre", "mlp_post"):
        gain_rows += gains[k] + [zero_rows]
    gain_rows += [g_kvn, jnp.zeros((7, D), F32)]
    gain_pack = jnp.concatenate(gain_rows, axis=0)
    gain_parts = all_gather([(gain_pack, 0)], name="gather_gain_grads")[0]
    pending += [(g_pool_w, 2, "pool_w", None), (g_pool_scale, 1, "pool_scale", None)]
    dy = start_pending("pool", dy)
    sent = dict.fromkeys(n for side, _, _, _ in started for n in side.names)
    recv.update(wait_sides(started, {n: recv[n] for n in sent}, dy, name="exchange_wait"))
    big = {"w_q": (w_q, m_w_q, v_w_q), "w_kv": (w_kv, m_w_kv, v_w_kv), "w_o": (w_o, m_w_o, v_w_o),
           "w_up": (w_up, m_w_up, v_w_up), "w_down": (w_down, m_w_down, v_w_down),
           "pool_w": (pool_w, m_pool_w, v_pool_w), "pool_scale": (pool_scale, m_pool_scale, v_pool_scale)}
    res = {n: reduce_adamw(recv[n], *wmv, name=f"adamw_{n}") for n, wmv in big.items()}
    gain_groups = [(mix_pre_g, m_mix_pre_g, v_mix_pre_g), (mix_post_g, m_mix_post_g, v_mix_post_g),
                   (mlp_pre_g, m_mlp_pre_g, v_mlp_pre_g), (mlp_post_g, m_mlp_post_g, v_mlp_post_g),
                   (kv_norm_g.reshape(1, D), m_kv_norm_g.reshape(1, D), v_kv_norm_g.reshape(1, D))]
    gres = gains_adamw(gain_parts, gain_groups, name="adamw_gains")
    for n, r in zip(["mix_pre_g", "mix_post_g", "mlp_pre_g", "mlp_post_g"], gres[:4]):
        res[n] = r
    res["kv_norm_g"] = [a.reshape(D) for a in gres[4]]

    order = ["pool_w", "pool_scale", "w_q", "w_kv", "kv_norm_g", "w_o", "w_up", "w_down",
             "mix_pre_g", "mix_post_g", "mlp_pre_g", "mlp_post_g"]
    out = [loss, grad_x]
    for k in range(4):
        out += [res[n][k] for n in order]
    return tuple(out)
```

```python
import functools

import jax
import jax.numpy as jnp
from jax import lax
from jax.experimental import pallas as pl
from jax.experimental.pallas import tpu as pltpu

F32 = jnp.float32
BF16 = jnp.bfloat16

EPS = 1e-6
HEAD_DIM = 64
LANES = 128
POOL_WINDOWS = (2, 4, 8, 16)
HALO = 16
N_DEV = 8
MESH_AXES = ("x", "y", "c")

ADAM_LR = 0.001
ADAM_B1 = 0.9
ADAM_B2 = 0.999
ADAM_EPS = 1e-08
ADAM_WD = 0.01
ADAM_STEP = 10

ROW_TILE = 512
ATT_TILE = 256
ROW_CHUNK = 32
GAIN_ROWS = 40


def _tile(n, want):
    return want if n % want == 0 else n


def matmul(a, b, *, name, ta=False, tb=False, a_layer=None, b_layer=None, out_dtype=F32,
           a_fn=None, epi=None, epi_in=None, out_scale=None, tm=1024, tn=1024, tk=1024):
    a2 = a.shape[1:] if a_layer is not None else a.shape
    b2 = b.shape[1:] if b_layer is not None else b.shape
    (K, M) = a2 if ta else a2[::-1]
    if not ta:
        M, K = a2
    if tb:
        N, Kb = b2
    else:
        Kb, N = b2
    assert K == Kb, (a.shape, b.shape)
    tm, tn, tk = _tile(M, tm), _tile(N, tn), _tile(K, tk)
    nk = K // tk
    grid = (M // tm, N // tn, nk)

    def lead(layer, shape, imap):
        if layer is None:
            return pl.BlockSpec(shape, imap)
        return pl.BlockSpec((None,) + shape, lambda i, j, k: (layer,) + imap(i, j, k))

    a_spec = lead(a_layer, (tk, tm) if ta else (tm, tk), (lambda i, j, k: (k, i)) if ta else (lambda i, j, k: (i, k)))
    b_spec = lead(b_layer, (tn, tk) if tb else (tk, tn), (lambda i, j, k: (j, k)) if tb else (lambda i, j, k: (k, j)))
    in_specs = [a_spec, b_spec]
    operands = [a, b]
    if epi is not None:
        in_specs.append(pl.BlockSpec((tm, tn), lambda i, j, k: (i, j)))
        operands.append(epi_in)
    out_shape = jax.ShapeDtypeStruct((M, N), out_dtype)
    out_spec = pl.BlockSpec((tm, tn), lambda i, j, k: (i, j))
    dims = (((0 if ta else 1,), (1 if tb else 0,)), ((), ()))
    n_in = len(operands)

    def body(*refs):
        a_ref, b_ref = refs[0], refs[1]
        e_ref = refs[2] if epi is not None else None
        o_ref = refs[n_in]

        def product():
            av = a_ref[...]
            if a_fn is not None:
                av = a_fn(av)
            return lax.dot_general(av.astype(BF16), b_ref[...].astype(BF16), dims, preferred_element_type=F32)

        def finish(r):
            if epi is not None:
                r = epi(r, e_ref[...])
            if out_scale is not None:
                r = r * out_scale
            o_ref[...] = r.astype(out_dtype)

        if nk == 1:
            finish(product())
            return
        acc_ref = refs[n_in + 1]
        k = pl.program_id(2)

        @pl.when(k == 0)
        def _():
            acc_ref[...] = product()

        @pl.when(k > 0)
        def _():
            acc_ref[...] += product()

        @pl.when(k == nk - 1)
        def _():
            finish(acc_ref[...])

    return pl.pallas_call(
        body, name=name, grid=grid, in_specs=in_specs, out_specs=out_spec, out_shape=out_shape,
        scratch_shapes=[pltpu.VMEM((tm, tn), F32)] if nk > 1 else [],
        compiler_params=pltpu.CompilerParams(dimension_semantics=("parallel", "parallel", "arbitrary")),
    )(*operands)


def _relu2(u):
    r = jnp.maximum(u.astype(F32), 0.0)
    return r * r


def _relu2_grad(acc, u):
    return acc * (2.0 * jnp.maximum(u.astype(F32), 0.0))


def rowwise(fn, rows, vecs, out_rows, n_acc, *, name, tile=ROW_TILE):
    S = rows[0].shape[0]
    tile = _tile(S, tile)
    n_rows, n_vecs, n_out = len(rows), len(vecs), len(out_rows)
    acc_cols = [None] * n_acc

    def body(*refs):
        ins = [r[...] for r in refs[:n_rows + n_vecs]]
        outs = refs[n_rows + n_vecs:]
        ro, ac = fn(*ins)
        assert len(ro) == n_out and len(ac) == n_acc
        for r, o in zip(outs[:n_out], ro):
            r[...] = o.astype(r.dtype)
        i = pl.program_id(0)
        for r, a in zip(outs[n_out:], ac):
            @pl.when(i == 0)
            def _():
                r[...] = jnp.zeros_like(r)
            r[...] += a

    acc_shapes = jax.eval_shape(
        lambda *xs: fn(*xs)[1],
        *[jax.ShapeDtypeStruct((tile, r.shape[1]), r.dtype) for r in rows],
        *[jax.ShapeDtypeStruct(v.shape, v.dtype) for v in vecs])
    in_specs = [pl.BlockSpec((tile, r.shape[1]), lambda i: (i, 0)) for r in rows]
    in_specs += [pl.BlockSpec(v.shape, lambda i: (0, 0)) for v in vecs]
    out_specs = [pl.BlockSpec((tile, c), lambda i: (i, 0)) for c, _ in out_rows]
    out_specs += [pl.BlockSpec(a.shape, lambda i: (0, 0)) for a in acc_shapes]
    out_shape = [jax.ShapeDtypeStruct((S, c), dt) for c, dt in out_rows]
    out_shape += [jax.ShapeDtypeStruct(a.shape, F32) for a in acc_shapes]
    del acc_cols
    return pl.pallas_call(
        body, name=name, grid=(S // tile,), in_specs=in_specs, out_specs=out_specs, out_shape=out_shape,
        compiler_params=pltpu.CompilerParams(dimension_semantics=("arbitrary",)),
    )(*rows, *vecs)


def _rms(x, g):
    r = lax.rsqrt(jnp.mean(x * x, axis=-1, keepdims=True) + EPS)
    return x * r * g


def _rms_bwd(x, g, dy):
    r = lax.rsqrt(jnp.mean(x * x, axis=-1, keepdims=True) + EPS)
    xh = x * r
    dyg = dy * g
    dx = r * (dyg - xh * jnp.mean(dyg * xh, axis=-1, keepdims=True))
    dg = jnp.sum(dy * xh, axis=0, keepdims=True)
    return dx, dg


def norm_only(x, g, dtype, *, name):
    D = x.shape[1]
    return rowwise(lambda xv, gv: ([_rms(xv, gv)], []), [x], [g], [(D, dtype)], 0, name=name)[0]


def residual_norms(x, m, g_post, next_gs, next_dtypes, *, name):
    D = x.shape[1]

    def fn(xv, mv, gp, *gs):
        xn = xv + _rms(mv, gp)
        return [xn] + [_rms(xn, g) for g in gs], []

    return rowwise(fn, [x, m], [g_post] + list(next_gs), [(D, F32)] + [(D, dt) for dt in next_dtypes], 0, name=name)


def residual_loss(x, d, g_post, target, *, name):
    D = x.shape[1]

    def fn(xv, dv, tv, gp):
        e = xv + _rms(dv, gp) - tv
        dy = e * (1.0 / D)
        dd, dg = _rms_bwd(dv, gp, dy)
        return [dy, dd], [jnp.sum(e * e, axis=0, keepdims=True) * (0.5 / D), dg]

    return rowwise(fn, [x, d, target], [g_post], [(D, F32), (D, BF16)], 2, name=name)


def mid_bwd(dy, x_mid, dh2, m, g_mlp_pre, g_mix_post, dm_dtype, *, name):
    D = dy.shape[1]

    def fn(dyv, xm, dh, mv, gpre, gpost):
        dx, dg_pre = _rms_bwd(xm, gpre, dh)
        dxm = dyv + dx
        dm, dg_post = _rms_bwd(mv, gpost, dxm)
        return [dxm, dm], [dg_pre, dg_post]

    return rowwise(fn, [dy, x_mid, dh2, m], [g_mlp_pre, g_mix_post], [(D, F32), (D, dm_dtype)], 2, name=name)


def pre_norm_bwd(dxm, x, dhs, gs, below=None, *, name):
    D = x.shape[1]
    n = len(dhs)
    rows = [dxm, x] + list(dhs) + ([below[0]] if below else [])
    vecs = list(gs) + ([below[1]] if below else [])

    def fn(*xs):
        dxv, xv, dh = xs[0], xs[1], xs[2:2 + n]
        g = xs[len(rows):]
        out, accs = dxv, []
        for k in range(n):
            dx, dg = _rms_bwd(xv, g[k], dh[k])
            out = out + dx
            accs.append(dg)
        if below is None:
            return [out], accs
        dd, dg = _rms_bwd(xs[2 + n], g[n], out)
        return [out, dd], accs + [dg]

    return rowwise(fn, rows, vecs, [(D, F32)] + ([(D, BF16)] if below else []), n + (1 if below else 0), name=name)


def sum_concat_cast(pairs, dtype, *, name):
    C = pairs[0][0].shape[1]
    n = len(pairs)

    def fn(*xs):
        return [jnp.concatenate([sum(xs[:n]), sum(xs[n:])], axis=1)], []

    return rowwise(fn, [a for a, _ in pairs] + [b for _, b in pairs], [], [(2 * C, dtype)], 0, name=name)[0]


def _window_sum(e, window, total_rows, backward):
    s, k = e, 1
    while k < window:
        s = s + pltpu.roll(s, (total_rows - k) if backward else k, 0)
        k *= 2
    return s


def pool_fwd(h, w, scale, *, name):
    S, D = h.shape
    G = len(POOL_WINDOWS)
    GC = D // G
    tile = _tile(S, ROW_TILE)
    hb = tile // HALO

    def body(hc_ref, hp_ref, w_ref, sc_ref, o_ref):
        i = pl.program_id(0)
        prev = jnp.where(i > 0, hp_ref[...], 0.0)
        ext = jnp.concatenate([prev, hc_ref[...]], axis=0)
        t = i * tile + lax.broadcasted_iota(jnp.int32, (tile, 1), 0)
        outs = []
        for g, window in enumerate(POOL_WINDOWS):
            e = ext[:, g * GC:(g + 1) * GC]
            s = _window_sum(e, window, HALO + tile, False)[HALO:, :]
            cnt = jnp.minimum(t + 1, window).astype(F32)
            y = s / cnt - e[HALO:, :]
            outs.append(jnp.dot(y.astype(BF16), w_ref[g], preferred_element_type=F32))
        o_ref[...] = jnp.concatenate(outs, axis=1) * sc_ref[...]

    return pl.pallas_call(
        body, name=name, grid=(S // tile,),
        in_specs=[pl.BlockSpec((tile, D), lambda i: (i, 0)),
                  pl.BlockSpec((HALO, D), lambda i: (jnp.maximum(i * hb - 1, 0), 0)),
                  pl.BlockSpec((G, GC, GC), lambda i: (0, 0, 0)),
                  pl.BlockSpec((1, D), lambda i: (0, 0))],
        out_specs=pl.BlockSpec((tile, D), lambda i: (i, 0)),
        out_shape=jax.ShapeDtypeStruct((S, D), F32),
        compiler_params=pltpu.CompilerParams(dimension_semantics=("parallel",)),
    )(h, h, w, scale)


def pool_bwd(h, dm, w, scale, *, name):
    S, D = h.shape
    G = len(POOL_WINDOWS)
    GC = D // G
    tile = _tile(S, ROW_TILE)
    hb = tile // HALO
    n_tiles = S // tile
    last_halo = S // HALO - 1

    def body(hc_ref, hp_ref, dmc_ref, dmn_ref, w_ref, sc_ref, dh_ref, dw_ref, dsc_ref):
        i = pl.program_id(0)

        @pl.when(i == 0)
        def _():
            dw_ref[...] = jnp.zeros_like(dw_ref)
            dsc_ref[...] = jnp.zeros_like(dsc_ref)

        prev = jnp.where(i > 0, hp_ref[...], 0.0)
        ext = jnp.concatenate([prev, hc_ref[...]], axis=0)
        nxt = jnp.where(i < n_tiles - 1, dmn_ref[...], 0.0)
        dmc = dmc_ref[...]
        dm_ext = jnp.concatenate([dmc, nxt], axis=0)
        t = i * tile + lax.broadcasted_iota(jnp.int32, (tile, 1), 0)
        t_ext = i * tile + lax.broadcasted_iota(jnp.int32, (tile + HALO, 1), 0)
        dhs, dscs = [], []
        for g, window in enumerate(POOL_WINDOWS):
            cols = slice(g * GC, (g + 1) * GC)
            e = ext[:, cols]
            s = _window_sum(e, window, HALO + tile, False)[HALO:, :]
            y = (s / jnp.minimum(t + 1, window).astype(F32) - e[HALO:, :]).astype(BF16)
            wg = w_ref[g]
            ypre = jnp.dot(y, wg, preferred_element_type=F32)
            dscs.append(jnp.sum(dmc[:, cols] * ypre, axis=0, keepdims=True))
            dyp = (dm_ext[:, cols] * sc_ref[:, cols]).astype(BF16)
            dw_ref[g] += lax.dot_general(y, dyp[:tile, :], (((0,), (0,)), ((), ())), preferred_element_type=F32)
            dy = lax.dot_general(dyp, wg, (((1,), (1,)), ((), ())), preferred_element_type=F32)
            r = dy / jnp.minimum(t_ext + 1, window).astype(F32)
            sr = _window_sum(r, window, tile + HALO, True)
            dhs.append(sr[:tile, :] - dy[:tile, :])
        dh_ref[...] = jnp.concatenate(dhs, axis=1)
        dsc_ref[...] += jnp.concatenate(dscs, axis=1)

    return pl.pallas_call(
        body, name=name, grid=(n_tiles,),
        in_specs=[pl.BlockSpec((tile, D), lambda i: (i, 0)),
                  pl.BlockSpec((HALO, D), lambda i: (jnp.maximum(i * hb - 1, 0), 0)),
                  pl.BlockSpec((tile, D), lambda i: (i, 0)),
                  pl.BlockSpec((HALO, D), lambda i: (jnp.minimum((i + 1) * hb, last_halo), 0)),
                  pl.BlockSpec((G, GC, GC), lambda i: (0, 0, 0)),
                  pl.BlockSpec((1, D), lambda i: (0, 0))],
        out_specs=[pl.BlockSpec((tile, D), lambda i: (i, 0)),
                   pl.BlockSpec((G, GC, GC), lambda i: (0, 0, 0)),
                   pl.BlockSpec((1, D), lambda i: (0, 0))],
        out_shape=[jax.ShapeDtypeStruct((S, D), F32), jax.ShapeDtypeStruct((G, GC, GC), F32),
                   jax.ShapeDtypeStruct((1, D), F32)],
        compiler_params=pltpu.CompilerParams(dimension_semantics=("arbitrary",)),
    )(h, h, dm, dm, w, scale)


ATT_LANES = 256
N_PAIR = ATT_LANES // HEAD_DIM
KEY_TILES = 4
KEY_TILES_BWD = 3


def _scores(xs, kj):
    return [lax.dot_general(x, kj, (((1,), (1,)), ((), ())), preferred_element_type=F32) for x in xs]


def _softplus_parts(z, mask):
    sp = jnp.maximum(z, 0.0) + jnp.log(1.0 + jnp.exp(-jnp.abs(z)))
    logb = z - sp
    if mask is not None:
        sp = jnp.where(mask, sp, 0.0)
    return logb, sp.astype(BF16), jnp.sum(sp, axis=1, keepdims=True)


def _weights(logb, later, c, mask):
    a = jnp.exp(logb - (later + c))
    return a if mask is None else jnp.where(mask, a, 0.0)


def _tile_weights(zss, cs, u_later, mask):
    partss = [[_softplus_parts(z, mask) for z in zs] for zs in zss]
    laterss = [[jnp.dot(sp, u_later, preferred_element_type=F32) for _, sp, _ in parts] for parts in partss]
    out = []
    for parts, laters in zip(partss, laterss):
        out.append(([p[0] for p in parts], [_weights(p[0], later, c, mask) for p, later, c in zip(parts, laters, cs)]))
        cs = [c + p[2] for c, p in zip(cs, parts)]
    return out, tuple(cs)


def _tri(T, later):
    rows = lax.broadcasted_iota(jnp.int32, (T, T), 0)
    cols = lax.broadcasted_iota(jnp.int32, (T, T), 1)
    return jnp.where((rows > cols) if later else (rows < cols), 1.0, 0.0).astype(BF16)


def _head_masks(x2, axis=None):
    lane = lax.broadcasted_iota(jnp.int32, (1, ATT_LANES), 1)
    parts = [jnp.where((lane // HEAD_DIM) == hh, x2, jnp.zeros_like(x2)) for hh in range(N_PAIR)]
    return parts if axis is None else jnp.concatenate(parts, axis=axis)


def _cat_bf16(parts, axis):
    return jnp.concatenate([p.astype(BF16) for p in parts], axis=axis)


def _side_split(side, refs, n_in, n_out):
    if side is None:
        return refs, None
    n_src, n_buf = len(side.srcs), len(side.bufs)
    ins, rest = refs[:n_in], refs[n_in:]
    src_refs, rest = rest[:n_src], rest[n_src + n_buf:]
    outs, rest = rest[:n_out], rest[n_out:]
    buf_refs, rest = rest[:n_buf], rest[n_buf:]
    own_scratch, sems = rest[:len(rest) - 3], rest[len(rest) - 3:]
    return tuple(ins) + tuple(outs) + tuple(own_scratch), (src_refs, buf_refs, sems)


def _side_phase(side, side_refs, phase, when):
    if side is None:
        return

    @pl.when(when)
    def _():
        side.copies(phase, *side_refs)


def attn_fwd(q, kv, *, name, side=None):
    S, D = q.shape
    P = D // ATT_LANES
    T = _tile(S, ATT_TILE)
    nq = S // T

    def body(*refs):
        (q_ref, k_ref, v_ref, o_ref), side_refs = _side_split(side, refs, 3, 1)
        i = pl.program_id(1)
        p = pl.program_id(0)
        _side_phase(side, side_refs, 0, (p == 0) & (i == 0))
        u_later = _tri(T, True)
        qhs = _head_masks(q_ref[...])
        diag = lax.broadcasted_iota(jnp.int32, (T, T), 1) < lax.broadcasted_iota(jnp.int32, (T, T), 0)

        def rows_of(ref, j):
            return ref[pl.ds(pl.multiple_of(j * T, T), T), :]

        def step(js, carry, mask):
            cs, acc = carry
            zss = [_scores(qhs, rows_of(k_ref, j)) for j in js]
            vcat = jnp.concatenate([_head_masks(rows_of(v_ref, j), axis=0) for j in js], axis=0)
            per_tile, cs = _tile_weights(zss, cs, u_later, mask)
            acat = _cat_bf16([a for _, aa in per_tile for a in aa], 1)
            return cs, acc + jnp.dot(acat, vcat, preferred_element_type=F32)

        carry = step([i], ((jnp.zeros((T, 1), F32),) * N_PAIR, jnp.zeros((T, ATT_LANES), F32)), diag)
        carry = lax.fori_loop(0, i % KEY_TILES, lambda n, cr: step([i - 1 - n], cr, None), carry)
        first = i - 1 - i % KEY_TILES
        _, acc = lax.fori_loop(0, i // KEY_TILES,
                               lambda n, cr: step([first - KEY_TILES * n - t for t in range(KEY_TILES)], cr, None), carry)
        o_ref[...] = acc.astype(o_ref.dtype)
        _side_phase(side, side_refs, 1, (p == P - 1) & (i == nq - 1))

    sd = side
    outs = pl.pallas_call(
        body, name=name, grid=(P, nq),
        in_specs=[pl.BlockSpec((T, ATT_LANES), lambda p, i: (i, p)),
                  pl.BlockSpec((S, ATT_LANES), lambda p, i: (0, p)),
                  pl.BlockSpec((S, ATT_LANES), lambda p, i: (0, P + p))] + (sd.specs() if sd else []),
        out_specs=[pl.BlockSpec((T, ATT_LANES), lambda p, i: (i, p))] + (sd.out_specs() if sd else []),
        out_shape=[jax.ShapeDtypeStruct((S, D), BF16)] + (sd.out_shape() if sd else []),
        scratch_shapes=sd.scratch() if sd else [],
        input_output_aliases=sd.aliases(3, 1) if sd else {},
        compiler_params=pltpu.CompilerParams(dimension_semantics=("arbitrary", "arbitrary"),
                                             has_side_effects=sd is not None),
    )(q, kv, kv, *(sd.operands() if sd else []))
    return outs[0] if sd is None else (outs[0], sd.result(outs[1:]))


def attn_bwd(q, kv, do, *, name, side=None):
    S, D = q.shape
    P = D // ATT_LANES
    T = _tile(S, ATT_TILE)
    nb = S // T

    def body(*refs):
        (q_ref, k_ref, v_ref, do_ref, dq_ref, dk_ref, dv_ref, g_scr, s_scr), side_refs = _side_split(side, refs, 4, 3)
        i = pl.program_id(1)
        p = pl.program_id(0)
        _side_phase(side, side_refs, 0, (p == 0) & (i == 0))

        @pl.when(i == 0)
        def _():
            dk_ref[...] = jnp.zeros_like(dk_ref)
            dv_ref[...] = jnp.zeros_like(dv_ref)

        u_later = _tri(T, True)
        u_earlier = _tri(T, False)
        qhs = _head_masks(q_ref[...])
        dohs = _head_masks(do_ref[...])
        qcat = jnp.concatenate(qhs, axis=0)
        docat = jnp.concatenate(dohs, axis=0)
        diag = lax.broadcasted_iota(jnp.int32, (T, T), 1) < lax.broadcasted_iota(jnp.int32, (T, T), 0)
        tdot = (((0,), (0,)), ((), ()))

        def rows_of(ref, j):
            return ref.at[pl.ds(pl.multiple_of(j * T, T), T), :]

        def step1(js, cs, mask):
            dass = [_scores(dohs, rows_of(v_ref, j)[...]) for j in js]
            zss = [_scores(qhs, rows_of(k_ref, j)[...]) for j in js]
            per_tile, cs = _tile_weights(zss, cs, u_later, mask)
            for j, das, (logbs, aa) in zip(js, dass, per_tile):
                for hh in range(N_PAIR):
                    g_scr[hh, j] = (das[hh] * aa[hh]).astype(BF16)
                    sg = jnp.exp(logbs[hh])
                    if mask is not None:
                        sg = jnp.where(mask, sg, 0.0)
                    s_scr[hh, j] = sg.astype(BF16)
            for j, (_, aa) in zip(js, per_tile):
                rows_of(dv_ref, j)[...] += lax.dot_general(_cat_bf16(aa, 0), docat, tdot,
                                                          preferred_element_type=F32)
            return cs

        G = KEY_TILES_BWD
        cs = step1([i], (jnp.zeros((T, 1), F32),) * N_PAIR, diag)
        cs = lax.fori_loop(0, i % G, lambda n, c: step1([i - 1 - n], c, None), cs)
        first = i - 1 - i % G
        lax.fori_loop(0, i // G, lambda n, c: step1([first - G * n - t for t in range(G)], c, None), cs)

        def step2(js, carry):
            cs, acc = carry
            cumss = [[jnp.dot(g_scr[hh, j], u_earlier, preferred_element_type=F32) for hh in range(N_PAIR)]
                     for j in js]
            kcat = jnp.concatenate([_head_masks(rows_of(k_ref, j)[...], axis=0) for j in js], axis=0)
            dzss = []
            for j, cums in zip(js, cumss):
                dzs, new = [], []
                for hh in range(N_PAIR):
                    gf = g_scr[hh, j].astype(F32)
                    sg = s_scr[hh, j].astype(F32)
                    dzs.append((gf - sg * (gf + (cums[hh] + cs[hh]))).astype(BF16))
                    new.append(cs[hh] + jnp.sum(gf, axis=1, keepdims=True))
                cs = tuple(new)
                dzss.append(dzs)
            acc = acc + jnp.dot(jnp.concatenate([dz for dzs in dzss for dz in dzs], axis=1), kcat,
                                preferred_element_type=F32)
            for j, dzs in zip(js, dzss):
                rows_of(dk_ref, j)[...] += lax.dot_general(jnp.concatenate(dzs, axis=0), qcat, tdot,
                                                          preferred_element_type=F32)
            return cs, acc

        carry = ((jnp.zeros((T, 1), F32),) * N_PAIR, jnp.zeros((T, ATT_LANES), F32))
        carry = lax.fori_loop(0, (i + 1) // G, lambda n, cr: step2([G * n + t for t in range(G)], cr), carry)
        done = (i + 1) // G * G
        _, dq = lax.fori_loop(0, (i + 1) % G, lambda n, cr: step2([done + n], cr), carry)
        dq_ref[...] = (dq * (HEAD_DIM ** -0.5)).astype(dq_ref.dtype)
        _side_phase(side, side_refs, 1, (p == P - 1) & (i == nb - 1))

    sd = side
    outs = pl.pallas_call(
        body, name=name, grid=(P, nb),
        in_specs=[pl.BlockSpec((T, ATT_LANES), lambda p, i: (i, p)),
                  pl.BlockSpec((S, ATT_LANES), lambda p, i: (0, p)),
                  pl.BlockSpec((S, ATT_LANES), lambda p, i: (0, P + p)),
                  pl.BlockSpec((T, ATT_LANES), lambda p, i: (i, p))] + (sd.specs() if sd else []),
        out_specs=[pl.BlockSpec((T, ATT_LANES), lambda p, i: (i, p)),
                   pl.BlockSpec((S, ATT_LANES), lambda p, i: (0, p)),
                   pl.BlockSpec((S, ATT_LANES), lambda p, i: (0, p))] + (sd.out_specs() if sd else []),
        out_shape=[jax.ShapeDtypeStruct((S, D), BF16), jax.ShapeDtypeStruct((S, D), F32),
                   jax.ShapeDtypeStruct((S, D), F32)] + (sd.out_shape() if sd else []),
        scratch_shapes=[pltpu.VMEM((N_PAIR, nb, T, T), BF16), pltpu.VMEM((N_PAIR, nb, T, T), BF16)]
        + (sd.scratch() if sd else []),
        input_output_aliases=sd.aliases(4, 3) if sd else {},
        compiler_params=pltpu.CompilerParams(dimension_semantics=("arbitrary", "arbitrary"),
                                             has_side_effects=sd is not None),
    )(q, kv, kv, do, *(sd.operands() if sd else []))
    return tuple(outs[:3]) if sd is None else (*outs[:3], sd.result(outs[3:]))


def _window(ref, axis, dev, n):
    return ref.at[(slice(None),) * axis + (pl.ds(dev * n, n),)]


def all_gather(ops, *, name):
    n_ops = len(ops)
    out_shape = []
    for a, ax in ops:
        shp = list(a.shape)
        shp[ax] *= N_DEV
        out_shape.append(jax.ShapeDtypeStruct(tuple(shp), a.dtype))

    def body(*refs):
        ins, outs = refs[:n_ops], refs[n_ops:2 * n_ops]
        send_sems, recv_sems, local_sems = refs[2 * n_ops:]
        x, y, c = (lax.axis_index(n) for n in MESH_AXES)
        me, sibling = (x, y, c), (x, y, 1 - c)
        chips = [(1 - x, y), (x, 1 - y), (1 - x, 1 - y)]

        def rows(o, dev):
            px, py, pc = dev
            ax = ops[o][1]
            return _window(outs[o], ax, 4 * px + 2 * py + pc, ops[o][0].shape[ax])

        def copy(o, k, block, to, src=None):
            return pltpu.make_async_remote_copy(
                src_ref=rows(o, block) if src is None else src, dst_ref=rows(o, block),
                send_sem=send_sems.at[o, k], recv_sem=recv_sems.at[o, k],
                device_id=to, device_id_type=pl.DeviceIdType.MESH)

        mine, first, passed = [], [], []
        for o in range(n_ops):
            cp = pltpu.make_async_copy(ins[o], rows(o, me), local_sems.at[o])
            cp.start()
            mine.append(cp)
            first.append(copy(o, 0, me, sibling, src=ins[o]))
            first += [copy(o, 1 + j, me, (*chip, c), src=ins[o]) for j, chip in enumerate(chips)]
        for cp in first:
            cp.start()
        for j, chip in enumerate(chips):
            for o in range(n_ops):
                copy(o, 1 + j, (*chip, c), me).wait_recv()
                cp = copy(o, 4 + j, (*chip, c), sibling)
                cp.start()
                passed.append(cp)
        for o in range(n_ops):
            copy(o, 0, sibling, me).wait_recv()
            for j, chip in enumerate(chips):
                copy(o, 4 + j, (*chip, 1 - c), me).wait_recv()
        for cp in first + passed:
            cp.wait_send()
        for cp in mine:
            cp.wait()

    any_spec = pl.BlockSpec(memory_space=pl.ANY)
    return pl.pallas_call(
        body, name=name, in_specs=[any_spec] * n_ops, out_specs=[any_spec] * n_ops, out_shape=out_shape,
        scratch_shapes=[pltpu.SemaphoreType.DMA((n_ops, 7)), pltpu.SemaphoreType.DMA((n_ops, 7)),
                        pltpu.SemaphoreType.DMA((n_ops,))],
        compiler_params=pltpu.CompilerParams(has_side_effects=True),
    )(*[a for a, _ in ops])


class Side:
    def __init__(self, kind, items, bufs):
        self.kind, self.items = kind, items
        self.names = list(bufs)
        self.bufs = [bufs[n] for n in self.names]
        self.srcs = [it[0] for it in items]

    def operands(self):
        return self.srcs + self.bufs

    def specs(self):
        return [pl.BlockSpec(memory_space=pl.ANY)] * (len(self.srcs) + len(self.bufs))

    def out_specs(self):
        return [pl.BlockSpec(memory_space=pl.ANY)] * len(self.bufs)

    def out_shape(self):
        return [jax.ShapeDtypeStruct(b.shape, b.dtype) for b in self.bufs]

    def aliases(self, first_in, first_out):
        return {first_in + len(self.srcs) + k: first_out + k for k in range(len(self.bufs))}

    def scratch(self):
        n = len(self.items)
        return [pltpu.SemaphoreType.DMA((n, N_DEV - 1)), pltpu.SemaphoreType.DMA((n, N_DEV - 1)),
                pltpu.SemaphoreType.DMA((n,))]

    def result(self, outs):
        return dict(zip(self.names, outs))

    def copies(self, phase, src_refs, buf_refs, sems):
        send_sems, recv_sems, local_sems = sems
        pos = tuple(lax.axis_index(n) for n in MESH_AXES)
        me = 4 * pos[0] + 2 * pos[1] + pos[2]
        for o, (_, ax, name, layer) in enumerate(self.items):
            src, buf = src_refs[o], buf_refs[self.names.index(name)]
            if self.kind == "gather":
                whole = buf if layer is None else buf.at[layer]
                n = src.shape[ax]
                sent = lambda dev, src=src: src
                lands = lambda dev, whole=whole, ax=ax, n=n: _window(whole, ax, dev, n)
            else:
                n = src.shape[ax] // N_DEV
                sent = lambda dev, src=src, ax=ax, n=n: _window(src, ax, dev, n)
                lands = lambda dev, buf=buf, layer=layer: buf.at[dev] if layer is None else buf.at[dev, layer]
            if local_sems is not None:
                local = pltpu.make_async_copy(sent(me), lands(me), local_sems.at[o])
                if phase == 0:
                    local.start()
                else:
                    local.wait()
            for r in range(1, N_DEV):
                peer = tuple(1 - p if r & bit else p for p, bit in zip(pos, (4, 2, 1)))
                pid = 4 * peer[0] + 2 * peer[1] + peer[2]
                cp = pltpu.make_async_remote_copy(
                    src_ref=sent(pid) if phase == 0 else sent(me), dst_ref=lands(me) if phase == 0 else lands(pid),
                    send_sem=send_sems[o] if isinstance(send_sems, (list, tuple)) else send_sems.at[o, r - 1],
                    recv_sem=recv_sems[o] if isinstance(recv_sems, (list, tuple)) else recv_sems.at[o, r - 1],
                    device_id=peer, device_id_type=pl.DeviceIdType.MESH)
                if phase == 0:
                    cp.start()
                else:
                    cp.wait_recv()
                    cp.wait_send()


def _hbm(a):
    return pltpu.with_memory_space_constraint(a, pltpu.HBM)


def own_blocks(items, bufs):
    me = 4 * lax.axis_index("x") + 2 * lax.axis_index("y") + lax.axis_index("c")
    bufs = {name: bufs[name] for _, _, name, _ in items}
    for src, ax, name, layer in items:
        n = src.shape[ax] // N_DEV
        blk = lax.dynamic_slice_in_dim(src, me * n, n, axis=ax)
        lead = (me,) if layer is None else (me, layer)
        blk = blk.reshape((1,) * len(lead) + blk.shape)
        bufs[name] = lax.dynamic_update_slice(bufs[name], blk, lead + (0,) * src.ndim)
    return bufs


def start_side(side, *, name):
    n_src, n_buf = len(side.srcs), len(side.bufs)
    n = len(side.items)

    def body(*refs):
        src_refs, buf_refs = refs[:n_src], refs[n_src:n_src + n_buf]
        sems = refs[n_src + n_buf:n_src + n_buf + 2 * n]
        token = refs[-1]
        side.copies(0, src_refs, buf_refs, (list(sems[:n]), list(sems[n:]), None))
        token[...] = jnp.zeros_like(token)

    hbm = pl.BlockSpec(memory_space=pltpu.HBM)
    sem = pl.BlockSpec(memory_space=pltpu.SEMAPHORE)
    operands = side.operands()
    outs = pl.pallas_call(
        body, name=name,
        out_shape=(*[pltpu.SemaphoreType.DMA(())] * (2 * n),
                   *[pltpu.HBM(a.shape, a.dtype) for a in operands], jax.ShapeDtypeStruct((8, LANES), F32)),
        in_specs=[hbm] * len(operands),
        out_specs=(*[sem] * (2 * n), *[hbm] * len(operands), pl.BlockSpec(memory_space=pltpu.VMEM)),
        input_output_aliases={k: 2 * n + k for k in range(len(operands))},
        compiler_params=pltpu.CompilerParams(has_side_effects=pltpu.SideEffectType.DATAFLOW_SIDE_EFFECTING),
    )(*[_hbm(a) for a in operands])
    started = (side, list(outs[:n]), list(outs[n:2 * n]), list(outs[2 * n:2 * n + n_src]))
    return started, dict(zip(side.names, outs[2 * n + n_src:2 * n + n_src + n_buf])), outs[-1]


def wait_sides(started, bufs, after, *, name):
    names = list(bufs)
    flat = []
    for side, send_sems, recv_sems, srcs in started:
        flat += [*srcs, *send_sems, *recv_sems]
    n_buf = len(names)

    def body(*refs):
        buf_refs = refs[:n_buf]
        rest = refs[n_buf:]
        for side, _, _, srcs in started:
            n = len(srcs)
            src_refs, send_sems, recv_sems = rest[:n], list(rest[n:2 * n]), list(rest[2 * n:3 * n])
            rest = rest[3 * n:]
            side.copies(1, src_refs, [buf_refs[names.index(nm)] for nm in side.names], (send_sems, recv_sems, None))

    hbm = pl.BlockSpec(memory_space=pltpu.HBM)
    sem = pl.BlockSpec(memory_space=pltpu.SEMAPHORE)
    specs = []
    for _, _, _, srcs in started:
        specs += [hbm] * len(srcs) + [sem] * (2 * len(srcs))
    outs = pl.pallas_call(
        body, name=name,
        out_shape=tuple(pltpu.HBM(bufs[nm].shape, bufs[nm].dtype) for nm in names),
        in_specs=[hbm] * n_buf + specs + [pl.BlockSpec(memory_space=pl.ANY)],
        out_specs=tuple([hbm] * n_buf),
        input_output_aliases={k: k for k in range(n_buf)},
        compiler_params=pltpu.CompilerParams(has_side_effects=pltpu.SideEffectType.DATAFLOW_SIDE_EFFECTING),
    )(*[bufs[nm] for nm in names], *flat, after)
    return dict(zip(names, outs))


def _adamw(w, g, m, v):
    m = ADAM_B1 * m + (1.0 - ADAM_B1) * g
    v = ADAM_B2 * v + (1.0 - ADAM_B2) * (g * g)
    m_hat = m / (1.0 - ADAM_B1 ** ADAM_STEP)
    v_hat = v / (1.0 - ADAM_B2 ** ADAM_STEP)
    delta = -ADAM_LR * (m_hat / (jnp.sqrt(v_hat) + ADAM_EPS) + ADAM_WD * w)
    return delta, m, v


def reduce_adamw(parts, w, m, v, *, name):
    shape = w.shape
    C = shape[-1]
    R = w.size // C
    tile = _tile(R, 512)

    def body(p_ref, w_ref, m_ref, v_ref, g_ref, d_ref, nm_ref, nv_ref):
        g = p_ref[0].astype(F32)
        for s in range(1, N_DEV):
            g = g + p_ref[s].astype(F32)
        d, nm, nv = _adamw(w_ref[...], g, m_ref[...], v_ref[...])
        g_ref[...] = g
        d_ref[...] = d
        nm_ref[...] = nm
        nv_ref[...] = nv

    row = pl.BlockSpec((tile, C), lambda i: (i, 0))
    outs = pl.pallas_call(
        body, name=name, grid=(R // tile,),
        in_specs=[pl.BlockSpec((N_DEV, tile, C), lambda i: (0, i, 0)), row, row, row],
        out_specs=[row] * 4, out_shape=[jax.ShapeDtypeStruct((R, C), F32)] * 4,
        compiler_params=pltpu.CompilerParams(dimension_semantics=("parallel",)),
    )(parts.reshape(N_DEV, R, C), w.reshape(R, C), m.reshape(R, C), v.reshape(R, C))
    return [o.reshape(shape) for o in outs]


def gains_adamw(parts, groups, *, name):
    n = len(groups)

    def body(*refs):
        p_ref = refs[0]
        ins, outs = refs[1:1 + 3 * n], refs[1 + 3 * n:]
        for k in range(n):
            w_ref, m_ref, v_ref = ins[3 * k:3 * k + 3]
            L = w_ref.shape[0]
            g = p_ref[pl.ds(8 * k, L), :]
            for s in range(1, N_DEV):
                g = g + p_ref[pl.ds(s * GAIN_ROWS + 8 * k, L), :]
            d, nm, nv = _adamw(w_ref[...], g, m_ref[...], v_ref[...])
            for r, val in zip(outs[4 * k:4 * k + 4], (g, d, nm, nv)):
                r[...] = val

    flat = [a for grp in groups for a in grp]
    out_shape = [jax.ShapeDtypeStruct(grp[0].shape, F32) for grp in groups for _ in range(4)]
    outs = pl.pallas_call(body, name=name, out_shape=out_shape)(parts, *flat)
    return [outs[4 * k:4 * k + 4] for k in range(n)]


def kernel(x, pool_w, pool_scale, w_q, w_kv, kv_norm_g, w_o, w_up, w_down, mix_pre_g, mix_post_g, mlp_pre_g, mlp_post_g, loss_target, m_pool_w, m_pool_scale, m_w_q, m_w_kv, m_kv_norm_g, m_w_o, m_w_up, m_w_down, m_mix_pre_g, m_mix_post_g, m_mlp_pre_g, m_mlp_post_g, v_pool_w, v_pool_scale, v_w_q, v_w_kv, v_kv_norm_g, v_w_o, v_w_up, v_w_down, v_mix_pre_g, v_mix_post_g, v_mlp_pre_g, v_mlp_post_g):
    _, S, D = x.shape
    x0 = x.reshape(S, D)
    target = loss_target.reshape(S, D)
    depth = w_up.shape[0]
    n_pool = pool_w.shape[0]
    F = w_up.shape[2] * N_DEV
    G = pool_w.shape[1]
    GC = D // G

    def vec(a, l):
        return a[l].reshape(1, D)

    n_att = depth - n_pool
    wq_s, wkv_s, wo_s = w_q.astype(BF16), w_kv.astype(BF16), w_o.astype(BF16)
    wup_s, wdn_s, pw_s = w_up.astype(BF16), w_down.astype(BF16), pool_w.astype(BF16)
    wq0, wkv, wup_a, wdn_a, pw, psc = all_gather(
        [(wq_s[0], 0), (wkv_s, 1), (wup_s[:n_pool], 2), (wdn_s[:n_pool], 1), (pw_s, 2), (pool_scale, 1)],
        name="gather_weights")
    late = Side("gather",
                [(wq_s[1:], 1, "wq", None), (wo_s, 1, "wo", None), (wup_s[n_pool:], 2, "wup", None),
                 (wdn_s[n_pool:], 1, "wdn", None)],
                {"wq": lax.empty((n_att - 1, D, D), BF16), "wo": lax.empty((n_att, D, D), BF16),
                 "wup": lax.empty((n_att, D, F), BF16), "wdn": lax.empty((n_att, F, D), BF16)})
    late_w = None

    def layer_of(early, key, l, n_early):
        return (early, l) if l < n_early else (late_w[key], l - n_early)

    saved = []
    xs = x0
    h1 = norm_only(xs, vec(mix_pre_g, 0), F32, name="norm_in")
    kv = hk = None
    dy = loss_rows = None
    for l in range(depth):
        is_pool = l < n_pool
        st = {"x": xs, "h1": h1}
        if is_pool:
            m = pool_fwd(h1, pw[l], psc[l].reshape(1, D), name=f"pool_fwd{l}")
        else:
            j = l - n_pool
            wq_j, wq_l = (wq0, None) if j == 0 else (late_w["wq"], j - 1)
            q = matmul(h1, wq_j, b_layer=wq_l, out_dtype=BF16, out_scale=HEAD_DIM ** -0.5, name=f"q_proj{j}")
            if j == 0:
                o, late_w = attn_fwd(q, kv, name=f"attn_fwd{j}", side=late)
            else:
                o = attn_fwd(q, kv, name=f"attn_fwd{j}")
            m = matmul(o, late_w["wo"], b_layer=j, name=f"o_proj{j}")
            st.update(q=q, o=o)
        x_mid, h2 = residual_norms(xs, m, vec(mix_post_g, l), [vec(mlp_pre_g, l)], [BF16], name=f"mix_out{l}")
        wup, lu = layer_of(wup_a, "wup", l, n_pool)
        wdn, ld = layer_of(wdn_a, "wdn", l, n_pool)
        u = matmul(h2, wup, b_layer=lu, out_dtype=BF16, tm=2048, name=f"mlp_up{l}")
        d = matmul(u, wdn, b_layer=ld, a_fn=_relu2, tm=2048, name=f"mlp_down{l}")
        st.update(m=m, x_mid=x_mid, h2=h2, u=u, d=d)
        saved.append(st)
        if l == depth - 1:
            dy, dd, loss_rows, dg_last = residual_loss(x_mid, d, vec(mlp_post_g, l), target, name="loss")
        elif l == n_pool - 1:
            xs, h1, hk = residual_norms(x_mid, d, vec(mlp_post_g, l), [vec(mix_pre_g, l + 1), kv_norm_g.reshape(1, D)],
                                        [BF16, BF16], name=f"mlp_out{l}")
            kv = matmul(hk, wkv, out_dtype=BF16, name="kv_proj")
        else:
            nxt_dt = F32 if l + 1 < n_pool else BF16
            xs, h1 = residual_norms(x_mid, d, vec(mlp_post_g, l), [vec(mix_pre_g, l + 1)], [nxt_dt], name=f"mlp_out{l}")
    loss = lax.psum(jnp.sum(loss_rows), MESH_AXES)

    recv = {"w_q": (w_q, BF16), "w_kv": (w_kv, BF16), "w_o": (w_o, BF16), "w_up": (w_up, BF16),
            "w_down": (w_down, BF16), "pool_w": (pool_w, F32), "pool_scale": (pool_scale, F32)}
    recv = {n: lax.empty((N_DEV,) + w.shape, dt) for n, (w, dt) in recv.items()}
    pending = []

    def send_pending():
        return Side("scatter", list(pending), {n: recv[n] for n in dict.fromkeys(it[2] for it in pending)})

    started = []

    def start_pending(tag, then):
        recv.update(own_blocks(pending, recv))
        begun, got, token = start_side(send_pending(), name=f"exchange_start_{tag}")
        recv.update(got)
        started.append(begun)
        pending.clear()
        return lax.optimization_barrier((then, token))[0]

    g_pw, g_psc = [None] * n_pool, [None] * n_pool
    gains = {k: [None] * depth for k in ("mix_pre", "mix_post", "mlp_pre", "mlp_post")}
    gains["mlp_post"][depth - 1] = dg_last
    dkvs = []
    g_kvn = None
    for l in reversed(range(depth)):
        st = saved[l]
        is_pool = l < n_pool
        wup, lu = layer_of(wup_a, "wup", l, n_pool)
        wdn, ld = layer_of(wdn_a, "wdn", l, n_pool)
        du = matmul(dd, wdn, b_layer=ld, tb=True, out_dtype=BF16, epi=_relu2_grad, epi_in=st["u"], tm=2048,
                    name=f"mlp_du{l}")
        g_wdn = matmul(st["u"], dd, ta=True, a_fn=_relu2, out_dtype=BF16, name=f"mlp_dwdn{l}")
        g_wup = matmul(st["h2"], du, ta=True, out_dtype=BF16, name=f"mlp_dwup{l}")
        pending += [(g_wup, 1, "w_up", l), (g_wdn, 0, "w_down", l)]
        if is_pool:
            du = start_pending(f"l{l}", du)
        dh2 = matmul(du, wup, b_layer=lu, tb=True, tm=2048, name=f"mlp_dh{l}")
        dxm, dm, gains["mlp_pre"][l], gains["mix_post"][l] = mid_bwd(
            dy, st["x_mid"], dh2, st["m"], vec(mlp_pre_g, l), vec(mix_post_g, l), F32 if is_pool else BF16,
            name=f"mid_bwd{l}")
        if is_pool:
            dh1, g_pw[l], g_psc[l] = pool_bwd(st["h1"], dm, pw[l], psc[l].reshape(1, D), name=f"pool_bwd{l}")
        else:
            j = l - n_pool
            wq_j, wq_l = (wq0, None) if j == 0 else (late_w["wq"], j - 1)
            do = matmul(dm, late_w["wo"], b_layer=j, tb=True, out_dtype=BF16, name=f"o_proj_dx{j}")
            g_wo = matmul(st["o"], dm, ta=True, out_dtype=BF16, name=f"o_proj_dw{j}")
            pending.append((g_wo, 0, "w_o", j))
            dq, dk, dv, got = attn_bwd(st["q"], kv, do, name=f"attn_bwd{j}", side=send_pending())
            recv.update(got)
            pending.clear()
            dkvs.append((dk, dv))
            g_wq = matmul(st["h1"], dq, ta=True, out_dtype=BF16, name=f"q_proj_dw{j}")
            pending.append((g_wq, 0, "w_q", j))
            dh1 = matmul(dq, wq_j, b_layer=wq_l, tb=True, name=f"q_proj_dx{j}")
        if l == n_pool:
            dkv = sum_concat_cast(dkvs, BF16, name="dkv_pack")
            g_wkv = matmul(hk, dkv, ta=True, out_dtype=BF16, name="kv_proj_dw")
            pending.append((g_wkv, 1, "w_kv", None))
            dhk = matmul(dkv, wkv, tb=True, name="kv_proj_dx")
            dhs, gs = [dh1, dhk], [vec(mix_pre_g, l), kv_norm_g.reshape(1, D)]
        else:
            dhs, gs = [dh1], [vec(mix_pre_g, l)]
        below = (saved[l - 1]["d"], vec(mlp_post_g, l - 1)) if l > 0 else None
        outs = pre_norm_bwd(dxm, st["x"], dhs, gs, below, name=f"mix_in_bwd{l}")
        dy, outs = outs[0], outs[1:]
        if below is not None:
            dd, outs = outs[0], outs[1:]
            gains["mlp_post"][l - 1] = outs[-1]
        gains["mix_pre"][l] = outs[0]
        if l == n_pool:
            g_kvn = outs[1]
    grad_x = dy.reshape(x.shape)

    g_pool_w = jnp.stack(g_pw)
    g_pool_scale = jnp.concatenate(g_psc, axis=0)
    zero_rows = jnp.zeros((8 - depth, D), F32)
    gain_rows = []
    for k in ("mix_pre", "mix_post", "mlp_pre", "mlp_post"):
        gain_rows += gains[k] + [zero_rows]
    gain_rows += [g_kvn, jnp.zeros((7, D), F32)]
    gain_pack = jnp.concatenate(gain_rows, axis=0)
    gain_parts = all_gather([(gain_pack, 0)], name="gather_gain_grads")[0]
    pending += [(g_pool_w, 2, "pool_w", None), (g_pool_scale, 1, "pool_scale", None)]
    dy = start_pending("pool", dy)
    sent = dict.fromkeys(n for side, _, _, _ in started for n in side.names)
    recv.update(wait_sides(started, {n: recv[n] for n in sent}, dy, name="exchange_wait"))
    big = {"w_q": (w_q, m_w_q, v_w_q), "w_kv": (w_kv, m_w_kv, v_w_kv), "w_o": (w_o, m_w_o, v_w_o),
           "w_up": (w_up, m_w_up, v_w_up), "w_down": (w_down, m_w_down, v_w_down),
           "pool_w": (pool_w, m_pool_w, v_pool_w), "pool_scale": (pool_scale, m_pool_scale, v_pool_scale)}
    res = {n: reduce_adamw(recv[n], *wmv, name=f"adamw_{n}") for n, wmv in big.items()}
    gain_groups = [(mix_pre_g, m_mix_pre_g, v_mix_pre_g), (mix_post_g, m_mix_post_g, v_mix_post_g),
                   (mlp_pre_g, m_mlp_pre_g, v_mlp_pre_g), (mlp_post_g, m_mlp_post_g, v_mlp_post_g),
                   (kv_norm_g.reshape(1, D), m_kv_norm_g.reshape(1, D), v_kv_norm_g.reshape(1, D))]
    gres = gains_adamw(gain_parts, gain_groups, name="adamw_gains")
    for n, r in zip(["mix_pre_g", "mix_post_g", "mlp_pre_g", "mlp_post_g"], gres[:4]):
        res[n] = r
    res["kv_norm_g"] = [a.reshape(D) for a in gres[4]]

    order = ["pool_w", "pool_scale", "w_q", "w_kv", "kv_norm_g", "w_o", "w_up", "w_down",
             "mix_pre_g", "mix_post_g", "mlp_pre_g", "mlp_post_g"]
    out = [loss, grad_x]
    for k in range(4):
        out += [res[n][k] for n in order]
    return tuple(out)
```
